```python
import math
import jax, jax.numpy as jnp
from jax import lax
import numpy as np

D_MODEL = 2048
BATCH = 4
SEQ = 8192
DEPTH = 1
DEC_BATCH = 16
DEC_SEQ = 16
PAST_LEN = 1024

CHUNK = 64
RET_HEADS = 8
RET_QK_DIM = 256
RET_V_DIM = D_MODEL // RET_HEADS
RET_ROPE_THETA = 10000.0
ATT_HEADS = 32
ATT_KV_HEADS = 4
ATT_HEAD_DIM = D_MODEL // ATT_HEADS
ATT_GROUP = ATT_HEADS // ATT_KV_HEADS
WINDOW = 128
BAND_PREV = WINDOW // CHUNK
ROPE_DIM = ATT_HEAD_DIM // 4
ROPE_THETA = 500000.0
D_FF = ((8 * D_MODEL // 3 + 255) // 256) * 256
ALPHA = (2.0 * DEPTH) ** 0.25
BETA = (8.0 * DEPTH) ** -0.25
LN_EPS = 1e-5
NEG_INF = -1e30
PROJ_WIDTHS = (RET_HEADS * RET_QK_DIM, RET_HEADS * RET_QK_DIM, RET_HEADS * RET_V_DIM, RET_HEADS * RET_V_DIM,
               ATT_HEADS * ATT_HEAD_DIM, ATT_KV_HEADS * ATT_HEAD_DIM, ATT_KV_HEADS * ATT_HEAD_DIM,
               D_MODEL, D_MODEL)
PROJ_WIDTH = sum(PROJ_WIDTHS)

kernel_name = "hybrid_retention_swa_sink_streaming_step"


def layernorm(z, g, b):
    z32 = z.astype(jnp.float32)
    mu = jnp.mean(z32, axis=-1, keepdims=True)
    var = jnp.mean(jnp.square(z32 - mu), axis=-1, keepdims=True)
    return ((z32 - mu) * lax.rsqrt(var + LN_EPS) * g + b).astype(z.dtype)


def rotary(x, pos, rot_dim, theta):
    half = rot_dim // 2
    inv_freq = 1.0 / (theta ** (jnp.arange(half, dtype=jnp.float32) / half))
    ang = pos.astype(jnp.float32)[:, None] * inv_freq[None, :]
    cos = jnp.cos(ang)[None, :, None, :].astype(x.dtype)
    sin = jnp.sin(ang)[None, :, None, :].astype(x.dtype)
    x1, x2, rest = x[..., :half], x[..., half:rot_dim], x[..., rot_dim:]
    return jnp.concatenate([x1 * cos - x2 * sin, x1 * sin + x2 * cos, rest], axis=-1)


def ret_log_gamma():
    return jnp.log1p(-jnp.exp2(-5.0 - jnp.arange(RET_HEADS, dtype=jnp.float32)))


def retention_block(q, k, v, s_prev, log_gamma):
    L = q.shape[2]
    idx = jnp.arange(L, dtype=jnp.float32)
    diff = idx[:, None] - idx[None, :]
    decay = jnp.where(diff >= 0, jnp.exp(log_gamma[:, None, None] * jnp.maximum(diff, 0.0)), 0.0)
    inner = jnp.einsum('bhld,bhmd->bhlm', q, k) * decay
    o = jnp.einsum('bhlm,bhmv->bhlv', inner, v)
    q_dec = jnp.exp(log_gamma[:, None] * (idx + 1.0))
    o = o + jnp.einsum('bhld,bhdv->bhlv', q, s_prev) * q_dec[None, :, :, None]
    k_dec = jnp.exp(log_gamma[:, None] * (L - 1.0 - idx))
    s_new = s_prev * jnp.exp(log_gamma * L)[None, :, None, None] + \
        jnp.einsum('bhld,bhlv->bhdv', k * k_dec[None, :, :, None].astype(k.dtype), v)
    return o, s_new


def retention_prompt(q, k, v):
    B, T, H, dk = q.shape
    dv = v.shape[-1]
    nc = T // CHUNK
    log_gamma = ret_log_gamma()

    def to_chunks(a):
        return a.reshape(B, nc, CHUNK, H, a.shape[-1]).transpose(1, 0, 3, 2, 4)

    def step(s, qkv):
        o, s = retention_block(qkv[0], qkv[1], qkv[2], s, log_gamma)
        return s, o

    s0 = jnp.zeros((B, H, dk, dv), jnp.float32)
    s_fin, o = lax.scan(step, s0, (to_chunks(q), to_chunks(k), to_chunks(v)))
    o = o.transpose(1, 0, 3, 2, 4).reshape(B, T, H, dv)
    return o, s_fin


def retention_sample(q, k, v, state):
    tr = lambda a: a.transpose(0, 2, 1, 3)
    o, s_new = retention_block(tr(q), tr(k), tr(v), state.astype(jnp.float32), ret_log_gamma())
    return tr(o), s_new


def sink_softmax(scores, sink):
    sink = jnp.broadcast_to(sink, scores.shape[:-1] + (1,))
    p = jax.nn.softmax(jnp.concatenate([scores, sink], axis=-1), axis=-1)
    return p[..., :-1]


def swa_prompt(q, k, v, sinks):
    B, T, Hq, hd = q.shape
    nc = T // CHUNK
    nb = BAND_PREV + 1
    qb = q.reshape(B, nc, CHUNK, ATT_KV_HEADS, ATT_GROUP, hd)

    def band(a):
        a = a.reshape(B, nc, CHUNK, ATT_KV_HEADS, hd)
        a = jnp.concatenate([jnp.zeros((B, BAND_PREV, CHUNK, ATT_KV_HEADS, hd), a.dtype), a], axis=1)
        return jnp.concatenate([a[:, j:j + nc] for j in range(nb)], axis=2)

    kb, vb = band(k), band(v)
    s = jnp.einsum('bncxgd,bnkxd->bnxgck', qb, kb).astype(jnp.float32)
    key_chunk = jnp.arange(nc)[:, None] - BAND_PREV + (jnp.arange(nb * CHUNK) // CHUNK)[None, :]
    valid = key_chunk >= 0
    s = jnp.where(valid[None, :, None, None, None, :], s, NEG_INF)
    p = sink_softmax(s, sinks.reshape(1, 1, ATT_KV_HEADS, ATT_GROUP, 1, 1).astype(jnp.float32))
    o = jnp.einsum('bnxgck,bnkxd->bncxgd', p.astype(vb.dtype), vb)
    return o.reshape(B, T, Hq, hd), (k[:, T - WINDOW:], v[:, T - WINDOW:])


def swa_sample(q, k, v, cache_k, cache_v, sinks):
    B, L, Hq, hd = q.shape
    kk = jnp.concatenate([cache_k.astype(k.dtype), k], axis=1)
    vv = jnp.concatenate([cache_v.astype(v.dtype), v], axis=1)
    qg = q.reshape(B, L, ATT_KV_HEADS, ATT_GROUP, hd)
    s = jnp.einsum('blxgd,bkxd->bxglk', qg, kk).astype(jnp.float32)
    p = sink_softmax(s, sinks.reshape(1, ATT_KV_HEADS, ATT_GROUP, 1, 1).astype(jnp.float32))
    o = jnp.einsum('bxglk,bkxd->blxgd', p.astype(vv.dtype), vv)
    return o.reshape(B, L, Hq, hd), (k, v)


def mixer_projections(h, pos, w_in):
    B, T, _ = h.shape
    proj = jnp.einsum('btd,de->bte', h, w_in)
    points, acc = [], 0
    for w in PROJ_WIDTHS[:-1]:
        acc += w
        points.append(acc)
    q_r, k_r, v_r, g_r, q_a, k_a, v_a, gate_r, gate_a = jnp.split(proj, points, axis=-1)
    q_r = rotary(q_r.reshape(B, T, RET_HEADS, RET_QK_DIM), pos, RET_QK_DIM, RET_ROPE_THETA)
    k_r = rotary(k_r.reshape(B, T, RET_HEADS, RET_QK_DIM), pos, RET_QK_DIM, RET_ROPE_THETA) * (RET_QK_DIM ** -0.5)
    v_r = v_r.reshape(B, T, RET_HEADS, RET_V_DIM)
    q_a = rotary(q_a.reshape(B, T, ATT_HEADS, ATT_HEAD_DIM), pos, ROPE_DIM, ROPE_THETA) * (ATT_HEAD_DIM ** -0.5)
    k_a = rotary(k_a.reshape(B, T, ATT_KV_HEADS, ATT_HEAD_DIM), pos, ROPE_DIM, ROPE_THETA)
    v_a = v_a.reshape(B, T, ATT_KV_HEADS, ATT_HEAD_DIM)
    return q_r, k_r, v_r, g_r, q_a, k_a, v_a, gate_r, gate_a


def trunk_layer(x, c, pos, ret_fn, att_fn, w_ada, b_ada, w_in, gn_g, w_o, ln1_g, ln1_b,
                w_ffn_gate, w_ffn_up, w_ffn_down, ln2_g, ln2_b):
    B, T, D = x.shape
    mods = jnp.einsum('bd,de->be', jax.nn.silu(c), w_ada) + b_ada
    sh_a, sc_a, gt_a, sh_f, sc_f, gt_f = jnp.split(mods[:, None, :], 6, axis=-1)
    h = x * (1.0 + sc_a) + sh_a
    q_r, k_r, v_r, g_r, q_a, k_a, v_a, gate_r, gate_a = mixer_projections(h, pos, w_in)
    o_ret, ret_state = ret_fn(q_r, k_r, v_r)
    o32 = o_ret.astype(jnp.float32)
    mu = jnp.mean(o32, axis=-1, keepdims=True)
    var = jnp.mean(jnp.square(o32 - mu), axis=-1, keepdims=True)
    gn = ((o32 - mu) * lax.rsqrt(var + LN_EPS)).reshape(B, T, D) * gn_g
    ret_branch = jax.nn.silu(g_r) * gn.astype(x.dtype)
    o_att, kv_state = att_fn(q_a, k_a, v_a)
    att_branch = o_att.reshape(B, T, D)
    merged = jax.nn.sigmoid(gate_r) * ret_branch + jax.nn.sigmoid(gate_a) * att_branch
    mix = jnp.einsum('btd,de->bte', merged, w_o)
    x1 = layernorm(ALPHA * x + gt_a * mix, ln1_g, ln1_b)
    h2 = x1 * (1.0 + sc_f) + sh_f
    ff = jax.nn.silu(jnp.einsum('btd,df->btf', h2, w_ffn_gate)) * jnp.einsum('btd,df->btf', h2, w_ffn_up)
    ff = jnp.einsum('btf,fd->btd', ff, w_ffn_down)
    x2 = layernorm(ALPHA * x1 + gt_f * ff, ln2_g, ln2_b)
    return x2, ret_state, kv_state


def setup_inputs(seed: int = 0) -> dict:
    key = jax.random.key(seed)
    ks = jax.random.split(key, 24)
    nrm = lambda k, shape, scale: jax.random.normal(k, shape, jnp.float32) * scale
    win_rows = min(WINDOW, PAST_LEN)
    col_scale = jnp.concatenate([jnp.full((w,), BETA if i in (2, 6) else 1.0, jnp.float32)
                                 for i, w in enumerate(PROJ_WIDTHS)]) * (D_MODEL ** -0.5)
    return {
        "x_prompt": nrm(ks[0], (BATCH, SEQ, D_MODEL), 1.0),
        "x_sample": nrm(ks[1], (DEC_BATCH, DEC_SEQ, D_MODEL), 1.0),
        "c_prompt": nrm(ks[2], (BATCH, D_MODEL), 1.0),
        "c_sample": nrm(ks[3], (DEC_BATCH, D_MODEL), 1.0),
        "cache_attn_k": nrm(ks[4], (DEPTH, DEC_BATCH, win_rows, ATT_KV_HEADS, ATT_HEAD_DIM), 1.0),
        "cache_attn_v": nrm(ks[5], (DEPTH, DEC_BATCH, win_rows, ATT_KV_HEADS, ATT_HEAD_DIM), BETA),
        "state_ret": nrm(ks[6], (DEPTH, DEC_BATCH, RET_HEADS, RET_QK_DIM, RET_V_DIM), 0.1),
        "w_ada": nrm(ks[7], (DEPTH, D_MODEL, 6 * D_MODEL), 0.5 * D_MODEL ** -0.5),
        "b_ada": nrm(ks[8], (DEPTH, 6 * D_MODEL), 0.02),
        "w_in": nrm(ks[9], (DEPTH, D_MODEL, PROJ_WIDTH), 1.0) * col_scale,
        "gn_g": 1.0 + nrm(ks[10], (DEPTH, D_MODEL), 0.02),
        "attn_sinks": nrm(ks[11], (DEPTH, ATT_HEADS), 0.5),
        "w_o": nrm(ks[12], (DEPTH, D_MODEL, D_MODEL), BETA * D_MODEL ** -0.5),
        "ln1_g": 1.0 + nrm(ks[13], (DEPTH, D_MODEL), 0.02),
        "ln1_b": nrm(ks[14], (DEPTH, D_MODEL), 0.02),
        "w_ffn_gate": nrm(ks[15], (DEPTH, D_MODEL, D_FF), BETA * D_MODEL ** -0.5),
        "w_ffn_up": nrm(ks[16], (DEPTH, D_MODEL, D_FF), BETA * D_MODEL ** -0.5),
        "w_ffn_down": nrm(ks[17], (DEPTH, D_FF, D_MODEL), BETA * D_FF ** -0.5),
        "ln2_g": 1.0 + nrm(ks[18], (DEPTH, D_MODEL), 0.02),
        "ln2_b": nrm(ks[19], (DEPTH, D_MODEL), 0.02),
    }


def reference(x_prompt, x_sample, c_prompt, c_sample, cache_attn_k, cache_attn_v, state_ret,
              w_ada, b_ada, w_in, gn_g, attn_sinks, w_o, ln1_g, ln1_b,
              w_ffn_gate, w_ffn_up, w_ffn_down, ln2_g, ln2_b):
    pos_p = jnp.arange(x_prompt.shape[1])
    pos_s = PAST_LEN + jnp.arange(x_sample.shape[1])
    y_p, y_s = x_prompt, x_sample
    kp_l, vp_l, sp_l, ks_l, vs_l, ss_l = [], [], [], [], [], []
    for l in range(DEPTH):
        weights = (w_ada[l], b_ada[l], w_in[l], gn_g[l], w_o[l], ln1_g[l], ln1_b[l],
                   w_ffn_gate[l], w_ffn_up[l], w_ffn_down[l], ln2_g[l], ln2_b[l])
        sink_l = attn_sinks[l]
        y_p, s_p, (k_p, v_p) = trunk_layer(
            y_p, c_prompt, pos_p, retention_prompt,
            lambda q, k, v: swa_prompt(q, k, v, sink_l), *weights)
        ck, cv, sr = cache_attn_k[l], cache_attn_v[l], state_ret[l]
        y_s, s_s, (k_s, v_s) = trunk_layer(
            y_s, c_sample, pos_s,
            lambda q, k, v: retention_sample(q, k, v, sr),
            lambda q, k, v: swa_sample(q, k, v, ck, cv, sink_l), *weights)
        kp_l.append(k_p); vp_l.append(v_p); sp_l.append(s_p)
        ks_l.append(k_s); vs_l.append(v_s); ss_l.append(s_s)
    new_attn_k_prompt = jnp.stack(kp_l)
    new_attn_v_prompt = jnp.stack(vp_l)
    new_state_ret_prompt = jnp.stack(sp_l)
    new_attn_k_sample = jnp.stack(ks_l)
    new_attn_v_sample = jnp.stack(vs_l)
    new_state_ret_sample = jnp.stack(ss_l)
    return (y_p, y_s, new_attn_k_prompt, new_attn_v_prompt, new_state_ret_prompt,
            new_attn_k_sample, new_attn_v_sample, new_state_ret_sample)
```

```python
import functools

import jax
import jax.numpy as jnp
from jax import lax
from jax.experimental import pallas as pl
from jax.experimental.pallas import tpu as pltpu

F32 = jnp.float32
BF16 = jnp.bfloat16

D_MODEL = 2048
CHUNK = 64
PAST_LEN = 1024
RET_HEADS = 8
RET_DIM = 256
RET_ROPE_THETA = 10000.0
ATT_HEADS = 32
ATT_KV_HEADS = 4
ATT_HEAD_DIM = 64
ATT_GROUP = ATT_HEADS // ATT_KV_HEADS
WINDOW = 128
ROPE_DIM = ATT_HEAD_DIM // 4
ROPE_THETA = 500000.0
D_FF = 5632
DEPTH = 1
ALPHA = (2.0 * DEPTH) ** 0.25
LN_EPS = 1e-5
NEG_INF = -1e30

LANES = 128
KV_W = 2 * ATT_KV_HEADS * ATT_HEAD_DIM
MAIN_W = 7 * D_MODEL
PROJ_TN = 1024
VMEM_LIMIT = 56 * 1024 * 1024

COL_QR, COL_KR, COL_VR, COL_GR, COL_QA, COL_SGR, COL_SGA = (i * D_MODEL for i in range(7))


def _params(sem):
    return pltpu.CompilerParams(dimension_semantics=sem, vmem_limit_bytes=VMEM_LIMIT)


def _mods_kernel(c_ref, w_ref, b_ref, o_ref):
    c = c_ref[...]
    a = (c * jax.nn.sigmoid(c)).astype(BF16)
    o_ref[...] = jnp.dot(a, w_ref[...].astype(BF16), preferred_element_type=F32) + b_ref[...]


def _mods(c_all, w_ada, b_ada):
    rows = c_all.shape[0]
    n_out = w_ada.shape[1]
    tn = 1024
    return pl.pallas_call(
        _mods_kernel,
        grid=(n_out // tn,),
        in_specs=[pl.BlockSpec((rows, D_MODEL), lambda n: (0, 0)),
                  pl.BlockSpec((D_MODEL, tn), lambda n: (0, n)),
                  pl.BlockSpec((1, tn), lambda n: (0, n))],
        out_specs=pl.BlockSpec((rows, tn), lambda n: (0, n)),
        out_shape=jax.ShapeDtypeStruct((rows, n_out), F32),
        compiler_params=_params(("arbitrary",)),
        name="mods",
    )(c_all, w_ada, b_ada.reshape(1, n_out))


def _rope_tables(pos):
    posf = pos.astype(F32)[:, None]
    half_r = RET_DIM // 2
    inv_r = 1.0 / (RET_ROPE_THETA ** (jnp.arange(half_r, dtype=F32) / half_r))
    ang_r = posf * inv_r[None, :]
    half_a = ROPE_DIM // 2
    inv_a = 1.0 / (ROPE_THETA ** (jnp.arange(half_a, dtype=F32) / half_a))
    lane = jnp.arange(LANES)
    d = lane % ATT_HEAD_DIM
    ang_a = posf * inv_a[d % half_a][None, :]
    cos_a, sin_a = jnp.cos(ang_a), jnp.sin(ang_a)
    ca = jnp.where(d[None, :] < ROPE_DIM, cos_a, 1.0)
    s1 = jnp.where(d[None, :] < half_a, -sin_a, 0.0)
    s2 = jnp.where((d[None, :] >= half_a) & (d[None, :] < ROPE_DIM), sin_a, 0.0)
    return jnp.stack([jnp.cos(ang_r), jnp.sin(ang_r), ca, s1, s2])


def _rot_att(x, ca, s1, s2):
    half = ROPE_DIM // 2
    return x * ca + pltpu.roll(x, LANES - half, 1) * s1 + pltpu.roll(x, half, 1) * s2


def _proj_kernel(x_ref, sc_ref, sh_ref, tab_ref, w_ref, wkv_ref, o_ref, kv_ref, h_ref):
    n = pl.program_id(2)

    @pl.when(n == 0)
    def _():
        h = x_ref[...] * (1.0 + sc_ref[...]) + sh_ref[...]
        h_ref[...] = h.astype(BF16)
        kv = jnp.dot(h_ref[...], wkv_ref[...], preferred_element_type=F32)
        ca, s1, s2 = tab_ref[2], tab_ref[3], tab_ref[4]
        for s in range(KV_W // LANES):
            xs = kv[:, s * LANES:(s + 1) * LANES]
            if s < KV_W // LANES // 2:
                xs = _rot_att(xs, ca, s1, s2)
            kv_ref[:, s * LANES:(s + 1) * LANES] = xs.astype(BF16)

    acc = jnp.dot(h_ref[...], w_ref[...], preferred_element_type=F32)
    region = n // (D_MODEL // PROJ_TN)

    def rot_ret(scale):
        cr, sr = tab_ref[0], tab_ref[1]
        for j in range(PROJ_TN // RET_DIM):
            a = j * RET_DIM
            x1 = acc[:, a:a + LANES]
            x2 = acc[:, a + LANES:a + 2 * LANES]
            o_ref[:, a:a + LANES] = ((x1 * cr - x2 * sr) * scale).astype(BF16)
            o_ref[:, a + LANES:a + 2 * LANES] = ((x1 * sr + x2 * cr) * scale).astype(BF16)

    @pl.when(region == 0)
    def _():
        rot_ret(1.0)

    @pl.when(region == 1)
    def _():
        rot_ret(RET_DIM ** -0.5)

    @pl.when(region == 2)
    def _():
        o_ref[...] = acc.astype(BF16)

    @pl.when(region == 3)
    def _():
        o_ref[...] = (acc * jax.nn.sigmoid(acc)).astype(BF16)

    @pl.when(region == 4)
    def _():
        ca, s1, s2 = tab_ref[2], tab_ref[3], tab_ref[4]
        for s in range(PROJ_TN // LANES):
            xs = _rot_att(acc[:, s * LANES:(s + 1) * LANES], ca, s1, s2)
            o_ref[:, s * LANES:(s + 1) * LANES] = (xs * ATT_HEAD_DIM ** -0.5).astype(BF16)

    @pl.when(region >= 5)
    def _():
        o_ref[...] = jax.nn.sigmoid(acc).astype(BF16)


def _mod_spec(mod, tm):
    if mod.shape[1] == 1:
        return pl.BlockSpec((None, 1, D_MODEL), lambda b, t, *_: (b, 0, 0))
    return pl.BlockSpec((None, tm, D_MODEL), lambda b, t, *_: (b, t, 0))


def _proj(x, sc, sh, tab, w_main, w_kv, tm):
    B, T, _ = x.shape
    n_tiles = MAIN_W // PROJ_TN
    return pl.pallas_call(
        _proj_kernel,
        grid=(B, T // tm, n_tiles),
        in_specs=[pl.BlockSpec((None, tm, D_MODEL), lambda b, t, n: (b, t, 0)),
                  _mod_spec(sc, tm), _mod_spec(sh, tm),
                  pl.BlockSpec((5, tm, LANES), lambda b, t, n: (0, t, 0)),
                  pl.BlockSpec((D_MODEL, PROJ_TN), lambda b, t, n: (0, n)),
                  pl.BlockSpec((D_MODEL, KV_W), lambda b, t, n: (0, 0))],
        out_specs=[pl.BlockSpec((None, tm, PROJ_TN), lambda b, t, n: (b, t, n)),
                   pl.BlockSpec((None, tm, KV_W), lambda b, t, n: (b, t, 0))],
        out_shape=[jax.ShapeDtypeStruct((B, T, MAIN_W), BF16),
                   jax.ShapeDtypeStruct((B, T, KV_W), BF16)],
        scratch_shapes=[pltpu.VMEM((tm, D_MODEL), BF16)],
        compiler_params=_params(("arbitrary", "arbitrary", "arbitrary")),
        name="proj",
    )(x, sc, sh, tab, w_main, w_kv)


def _kv32_kernel(x_ref, sc_ref, sh_ref, tab_ref, wkv_ref, o_ref):
    h = (x_ref[...] * (1.0 + sc_ref[...]) + sh_ref[...]).astype(BF16)
    kv = jnp.dot(h, wkv_ref[...], preferred_element_type=F32)
    ca, s1, s2 = tab_ref[2], tab_ref[3], tab_ref[4]
    for s in range(KV_W // LANES):
        xs = kv[:, s * LANES:(s + 1) * LANES]
        if s < KV_W // LANES // 2:
            xs = _rot_att(xs, ca, s1, s2)
        o_ref[:, s * LANES:(s + 1) * LANES] = xs


def _kv32(x, sc, sh, tab, w_kv):
    B, R, _ = x.shape
    return pl.pallas_call(
        _kv32_kernel,
        grid=(B, 1),
        in_specs=[pl.BlockSpec((None, R, D_MODEL), lambda b, t: (b, 0, 0)),
                  _mod_spec(sc, R), _mod_spec(sh, R),
                  pl.BlockSpec((5, R, LANES), lambda b, t: (0, 0, 0)),
                  pl.BlockSpec((D_MODEL, KV_W), lambda b, t: (0, 0))],
        out_specs=pl.BlockSpec((None, R, KV_W), lambda b, t: (b, 0, 0)),
        out_shape=jax.ShapeDtypeStruct((B, R, KV_W), F32),
        compiler_params=_params(("arbitrary", "arbitrary")),
        name="kv32",
    )(x, sc, sh, tab, w_kv)


def _ret_kernel(lg_ref, q_ref, k_ref, v_ref, gs_ref, sg_ref, gn_ref, s0_ref, o_ref, sout_ref,
                s_ref, dm_ref, qd_ref, kd_ref, *, chunk, n_chunks):
    hh = pl.program_id(1)
    t = pl.program_id(2)
    lg = lg_ref[hh]

    @pl.when(t == 0)
    def _():
        s_ref[...] = s0_ref[...]
        i = lax.broadcasted_iota(jnp.int32, (chunk, chunk), 0)
        j = lax.broadcasted_iota(jnp.int32, (chunk, chunk), 1)
        diff = (i - j).astype(F32)
        dm_ref[...] = jnp.where(diff >= 0, jnp.exp(lg * jnp.maximum(diff, 0.0)), 0.0)
        r = lax.broadcasted_iota(jnp.int32, (chunk, RET_DIM), 0).astype(F32)
        qd_ref[...] = jnp.exp(lg * (r + 1.0))
        kd_ref[...] = jnp.exp(lg * (chunk - 1.0 - r))

    g_chunk = jnp.exp(jnp.full((1, RET_DIM), lg * chunk, F32))
    nt = (((1,), (1,)), ((), ()))
    tn = (((0,), (0,)), ((), ()))
    for c in range(n_chunks):
        rows = pl.ds(c * chunk, chunk)
        q, k, v = q_ref[rows, :], k_ref[rows, :], v_ref[rows, :]
        inner = lax.dot_general(q, k, nt, preferred_element_type=F32) * dm_ref[...]
        o = jnp.dot(inner.astype(BF16), v, preferred_element_type=F32)
        s_prev = s_ref[...]
        o = o + jnp.dot(q, s_prev.astype(BF16), preferred_element_type=F32) * qd_ref[...]
        kdec = (k.astype(F32) * kd_ref[...]).astype(BF16)
        s_ref[...] = s_prev * g_chunk + lax.dot_general(kdec, v, tn, preferred_element_type=F32)
        mu = jnp.mean(o, axis=-1, keepdims=True)
        xc = o - mu
        var = jnp.mean(xc * xc, axis=-1, keepdims=True)
        gn = xc * lax.rsqrt(var + LN_EPS) * gn_ref[...]
        out = sg_ref[rows, :].astype(F32) * (gs_ref[rows, :].astype(F32) * gn)
        o_ref[rows, :] = out.astype(BF16)

    @pl.when(t == pl.num_programs(2) - 1)
    def _():
        sout_ref[...] = s_ref[...]


def _retention(proj, gn_g, s0, chunk, tt):
    B, T, _ = proj.shape
    n_chunks = tt // chunk
    log_gamma = jnp.log1p(-jnp.exp2(-5.0 - jnp.arange(RET_HEADS, dtype=F32)))

    def col(off):
        base = off // RET_DIM
        return pl.BlockSpec((None, tt, RET_DIM), lambda b, h, t: (b, t, base + h))

    return pl.pallas_call(
        functools.partial(_ret_kernel, chunk=chunk, n_chunks=n_chunks),
        grid=(B, RET_HEADS, T // tt),
        in_specs=[pl.BlockSpec(memory_space=pltpu.SMEM),
                  col(COL_QR), col(COL_KR), col(COL_VR), col(COL_GR), col(COL_SGR),
                  pl.BlockSpec((1, RET_DIM), lambda b, h, t: (0, h)),
                  pl.BlockSpec((None, None, RET_DIM, RET_DIM), lambda b, h, t: (b, h, 0, 0))],
        out_specs=[pl.BlockSpec((None, tt, RET_DIM), lambda b, h, t: (b, t, h)),
                   pl.BlockSpec((None, None, RET_DIM, RET_DIM), lambda b, h, t: (b, h, 0, 0))],
        out_shape=[jax.ShapeDtypeStruct((B, T, D_MODEL), BF16),
                   jax.ShapeDtypeStruct((B, RET_HEADS, RET_DIM, RET_DIM), F32)],
        scratch_shapes=[pltpu.VMEM((RET_DIM, RET_DIM), F32),
                        pltpu.VMEM((chunk, chunk), F32),
                        pltpu.VMEM((chunk, RET_DIM), F32),
                        pltpu.VMEM((chunk, RET_DIM), F32)],
        compiler_params=_params(("arbitrary", "arbitrary", "arbitrary")),
        name="retention",
    )(log_gamma, proj, proj, proj, proj, proj, gn_g.reshape(1, D_MODEL), s0)


def _att_kernel(sink_ref, q_ref, halo_ref, cur_ref, sg_ref, o_ref, *, cq, n_chunks, own_valid, mask_first):
    i = pl.program_id(1)
    kv = jnp.concatenate([halo_ref[...], cur_ref[...]], axis=0).astype(F32)
    n_keys = kv.shape[0]
    lane = lax.broadcasted_iota(jnp.int32, (1, LANES), 1)
    lo = lane < ATT_HEAD_DIM
    ones_lo = jnp.broadcast_to(jnp.where(lo, 1.0, 0.0), (n_keys, LANES)).astype(BF16)
    ones_hi = jnp.broadcast_to(jnp.where(lo, 0.0, 1.0), (n_keys, LANES)).astype(BF16)
    own_ok = (lane % ATT_HEAD_DIM) < own_valid
    nt = (((1,), (1,)), ((), ()))
    pairs = ATT_GROUP // 2
    k_off = ATT_KV_HEADS * ATT_HEAD_DIM

    for y in range(ATT_KV_HEADS // 2):
        ks = kv[:, y * LANES:(y + 1) * LANES]
        kr = pltpu.roll(ks, ATT_HEAD_DIM, 1)
        vs = kv[:, k_off + y * LANES:k_off + (y + 1) * LANES]
        vr = pltpu.roll(vs, ATT_HEAD_DIM, 1)
        for par in range(2):
            x = 2 * y + par
            k_even, k_odd = (ks, kr) if par == 0 else (kr, ks)
            v_even, v_odd = (vs, vr) if par == 0 else (vr, vs)
            klo = jnp.where(lo, k_even, 0.0).astype(BF16)
            khi = jnp.where(lo, 0.0, k_odd).astype(BF16)
            vlo = jnp.concatenate([jnp.where(lo, v_even, 0.0).astype(BF16), ones_lo], axis=1)
            vhi = jnp.concatenate([jnp.where(lo, 0.0, v_odd).astype(BF16), ones_hi], axis=1)
            sink_b = jnp.concatenate(
                [jnp.broadcast_to(jnp.where(lo, sink_ref[x * ATT_GROUP + 2 * p],
                                            sink_ref[x * ATT_GROUP + 2 * p + 1]), (cq, LANES))
                 for p in range(pairs)], axis=0)
            for c in range(n_chunks):
                r0 = c * CHUNK
                segs = [slice(r0 + j * CHUNK, r0 + (j + 1) * CHUNK) for j in range(3)]
                kt = jnp.concatenate([a[sl] for sl in segs for a in (klo, khi)], axis=0)
                vb = jnp.concatenate([a[sl] for sl in segs for a in (vlo, vhi)], axis=0)
                col0 = x * ATT_GROUP * ATT_HEAD_DIM
                qp = jnp.concatenate(
                    [q_ref[c * cq:(c + 1) * cq, col0 + p * LANES:col0 + (p + 1) * LANES]
                     for p in range(pairs)], axis=0)
                s = lax.dot_general(qp, kt, nt, preferred_element_type=F32)
                sj = [s[:, j * LANES:(j + 1) * LANES] for j in range(3)]
                if mask_first:
                    for j in range(2):
                        if c + j < 2:
                            sj[j] = jnp.where(i * n_chunks + (c + j - 2) < 0, NEG_INF, sj[j])
                if own_valid < CHUNK:
                    sj[2] = jnp.where(own_ok, sj[2], NEG_INF)
                m3 = jnp.maximum(jnp.maximum(sj[0], sj[1]), sj[2])
                m_even = jnp.max(jnp.where(lo, m3, NEG_INF), axis=1, keepdims=True)
                m_odd = jnp.max(jnp.where(lo, NEG_INF, m3), axis=1, keepdims=True)
                m_b = jnp.maximum(jnp.where(lo, m_even, m_odd), sink_b)
                p_all = jnp.concatenate([jnp.exp(a - m_b).astype(BF16) for a in sj], axis=1)
                oe = jnp.dot(p_all, vb, preferred_element_type=F32)
                den = oe[:, LANES:] + jnp.exp(sink_b - m_b)
                res = oe[:, :LANES] / den
                for p in range(pairs):
                    cols = slice(col0 + p * LANES, col0 + (p + 1) * LANES)
                    rows = slice(c * cq, (c + 1) * cq)
                    gate = sg_ref[rows, cols].astype(F32)
                    o_ref[rows, cols] = (gate * res[p * cq:(p + 1) * cq]).astype(BF16)


def _attention(proj, halo_src, cur_src, sinks, cq, n_chunks, own_valid, mask_first):
    B, T, _ = proj.shape
    tq = cq * n_chunks
    cur_rows = CHUNK * n_chunks
    q_blk, sg_blk = COL_QA // D_MODEL, COL_SGA // D_MODEL
    halo_per_cur = cur_rows // WINDOW if mask_first else 0

    def halo_map(b, i):
        return (b, jnp.maximum(i * halo_per_cur - 1, 0), 0) if mask_first else (b, 0, 0)

    return pl.pallas_call(
        functools.partial(_att_kernel, cq=cq, n_chunks=n_chunks, own_valid=own_valid, mask_first=mask_first),
        grid=(B, T // tq),
        in_specs=[pl.BlockSpec(memory_space=pltpu.SMEM),
                  pl.BlockSpec((None, tq, D_MODEL), lambda b, i: (b, i, q_blk)),
                  pl.BlockSpec((None, WINDOW, KV_W), halo_map),
                  pl.BlockSpec((None, cur_rows, KV_W), lambda b, i: (b, i, 0)),
                  pl.BlockSpec((None, tq, D_MODEL), lambda b, i: (b, i, sg_blk))],
        out_specs=pl.BlockSpec((None, tq, D_MODEL), lambda b, i: (b, i, 0)),
        out_shape=jax.ShapeDtypeStruct((B, T, D_MODEL), BF16),
        compiler_params=_params(("arbitrary", "arbitrary")),
        name="attention",
    )(sinks, proj, halo_src, cur_src, proj)


def _layernorm(z, g, b):
    mu = jnp.mean(z, axis=-1, keepdims=True)
    zc = z - mu
    var = jnp.mean(zc * zc, axis=-1, keepdims=True)
    return zc * lax.rsqrt(var + LN_EPS) * g + b


def _out_kernel(r_ref, a_ref, x_ref, gt_ref, sc_ref, sh_ref, w_ref, g_ref, b_ref, x1_ref, h2_ref):
    merged = r_ref[...] + a_ref[...]
    mix = jnp.dot(merged, w_ref[...], preferred_element_type=F32)
    x1 = _layernorm(ALPHA * x_ref[...] + gt_ref[...] * mix, g_ref[...], b_ref[...])
    x1_ref[...] = x1
    h2_ref[...] = (x1 * (1.0 + sc_ref[...]) + sh_ref[...]).astype(BF16)


def _out_proj(ret_m, att_m, x, gt, sc, sh, w_o, g, b, tm):
    B, T, _ = x.shape
    tok = pl.BlockSpec((None, tm, D_MODEL), lambda bb, t: (bb, t, 0))
    vec = pl.BlockSpec((1, D_MODEL), lambda bb, t: (0, 0))
    return pl.pallas_call(
        _out_kernel,
        grid=(B, T // tm),
        in_specs=[tok, tok, tok, _mod_spec(gt, tm), _mod_spec(sc, tm), _mod_spec(sh, tm),
                  pl.BlockSpec((D_MODEL, D_MODEL), lambda bb, t: (0, 0)), vec, vec],
        out_specs=[tok, tok],
        out_shape=[jax.ShapeDtypeStruct((B, T, D_MODEL), F32),
                   jax.ShapeDtypeStruct((B, T, D_MODEL), BF16)],
        compiler_params=_params(("arbitrary", "arbitrary")),
        name="out_proj",
    )(ret_m, att_m, x, gt, sc, sh, w_o, g.reshape(1, D_MODEL), b.reshape(1, D_MODEL))


def _ffn_kernel(h_ref, x1_ref, gt_ref, wg_ref, wu_ref, wd_ref, g_ref, b_ref, o_ref, acc_ref):
    f = pl.program_id(2)
    h = h_ref[...]
    a = jnp.dot(h, wg_ref[...], preferred_element_type=F32)
    u = jnp.dot(h, wu_ref[...], preferred_element_type=F32)
    act = (a * jax.nn.sigmoid(a) * u).astype(BF16)
    part = jnp.dot(act, wd_ref[...], preferred_element_type=F32)

    @pl.when(f == 0)
    def _():
        acc_ref[...] = part

    @pl.when(f > 0)
    def _():
        acc_ref[...] += part

    @pl.when(f == pl.num_programs(2) - 1)
    def _():
        z = ALPHA * x1_ref[...] + gt_ref[...] * acc_ref[...]
        o_ref[...] = _layernorm(z, g_ref[...], b_ref[...])


def _ffn(h2, x1, gt, wg, wu, wd, g, b, tm, tf):
    B, T, _ = x1.shape
    tok = pl.BlockSpec((None, tm, D_MODEL), lambda bb, t, f: (bb, t, 0))
    vec = pl.BlockSpec((1, D_MODEL), lambda bb, t, f: (0, 0))
    return pl.pallas_call(
        _ffn_kernel,
        grid=(B, T // tm, D_FF // tf),
        in_specs=[tok, tok, _mod_spec(gt, tm),
                  pl.BlockSpec((D_MODEL, tf), lambda bb, t, f: (0, f)),
                  pl.BlockSpec((D_MODEL, tf), lambda bb, t, f: (0, f)),
                  pl.BlockSpec((tf, D_MODEL), lambda bb, t, f: (f, 0)), vec, vec],
        out_specs=tok,
        out_shape=jax.ShapeDtypeStruct((B, T, D_MODEL), F32),
        scratch_shapes=[pltpu.VMEM((tm, D_MODEL), F32)],
        compiler_params=_params(("arbitrary", "arbitrary", "arbitrary")),
        name="ffn",
    )(h2, x1, gt, wg, wu, wd, g.reshape(1, D_MODEL), b.reshape(1, D_MODEL))


def _largest_tile(total, cap):
    t = min(total, cap)
    while total % t:
        t //= 2
    return t


def kernel(x_prompt, x_sample, c_prompt, c_sample, cache_attn_k, cache_attn_v, state_ret, w_ada, b_ada, w_in,
           gn_g, attn_sinks, w_o, ln1_g, ln1_b, w_ffn_gate, w_ffn_up, w_ffn_down, ln2_g, ln2_b):
    B, T, _ = x_prompt.shape
    Bs, Ls, _ = x_sample.shape
    l = 0

    n_c = B + Bs
    pad = (-n_c) % 8
    c_all = jnp.concatenate([c_prompt, c_sample, jnp.zeros((pad, D_MODEL), F32)], axis=0)
    mods = _mods(c_all, w_ada[l], b_ada[l])
    mods_p = [m[:, None, :] for m in jnp.split(mods[:B], 6, axis=-1)]
    mods_s = [jnp.repeat(m, Ls, axis=0)[None] for m in jnp.split(mods[B:n_c], 6, axis=-1)]

    w = w_in[l]
    o_ka = 5 * D_MODEL
    o_gate = o_ka + KV_W
    w_main = jnp.concatenate([w[:, :o_ka], w[:, o_gate:]], axis=1).astype(BF16)
    w_kv = w[:, o_ka:o_gate].astype(BF16)
    wo, wg, wu, wd = (a[l].astype(BF16) for a in (w_o, w_ffn_gate, w_ffn_up, w_ffn_down))

    def layer(x, mods6, pos, s0, att_fn, tm_proj, tm_out, tm_ffn, ret_chunk, ret_tt):
        sh_a, sc_a, gt_a, sh_f, sc_f, gt_f = mods6
        tab = _rope_tables(pos)
        proj, kvb = _proj(x, sc_a, sh_a, tab, w_main, w_kv, tm_proj)
        ret_m, s_new = _retention(proj, gn_g[l], s0, ret_chunk, ret_tt)
        att_m = att_fn(proj, kvb)
        x1, h2 = _out_proj(ret_m, att_m, x, gt_a, sc_f, sh_f, wo, ln1_g[l], ln1_b[l], tm_out)
        y = _ffn(h2, x1, gt_f, wg, wu, wd, ln2_g[l], ln2_b[l], tm_ffn, 512)
        return y, s_new, tab

    pos_p = jnp.arange(T)
    n_chunks_p = _largest_tile(T // CHUNK, 4)
    y_p, s_p, tab_p = layer(
        x_prompt, mods_p, pos_p, jnp.zeros((B, RET_HEADS, RET_DIM, RET_DIM), F32),
        lambda proj, kvb: _attention(proj, kvb, kvb, attn_sinks[l], CHUNK, n_chunks_p, CHUNK, True),
        _largest_tile(T, 512), _largest_tile(T, 512), _largest_tile(T, 512),
        _largest_tile(T, 256), _largest_tile(T, 1024))
    kv_p = _kv32(x_prompt[:, T - WINDOW:], mods_p[1], mods_p[0], tab_p[:, T - WINDOW:], w_kv)
    k_p = kv_p[..., :KV_W // 2].reshape(B, WINDOW, ATT_KV_HEADS, ATT_HEAD_DIM)
    v_p = kv_p[..., KV_W // 2:].reshape(B, WINDOW, ATT_KV_HEADS, ATT_HEAD_DIM)

    R = Bs * Ls
    xs = x_sample.reshape(1, R, D_MODEL)
    pos_s = jnp.tile(PAST_LEN + jnp.arange(Ls), Bs)
    cache = jnp.concatenate([cache_attn_k[l].reshape(Bs, WINDOW, KV_W // 2),
                             cache_attn_v[l].reshape(Bs, WINDOW, KV_W // 2)], axis=-1).astype(BF16)

    def att_sample(proj, kvb):
        new = jnp.pad(kvb.reshape(Bs, Ls, KV_W), ((0, 0), (0, CHUNK - Ls), (0, 0)))
        o = _attention(proj.reshape(Bs, Ls, MAIN_W), cache, new, attn_sinks[l], Ls, 1, Ls, False)
        return o.reshape(1, R, D_MODEL)

    def ret_sample(proj, s0):
        return _retention(proj.reshape(Bs, Ls, MAIN_W), gn_g[l], s0, Ls, Ls)

    sh_a, sc_a, gt_a, sh_f, sc_f, gt_f = mods_s
    tab_s = _rope_tables(pos_s)
    proj_s, kvb_s = _proj(xs, sc_a, sh_a, tab_s, w_main, w_kv, R)
    ret_s, s_s = ret_sample(proj_s, state_ret[l])
    att_s = att_sample(proj_s, kvb_s)
    x1_s, h2_s = _out_proj(ret_s.reshape(1, R, D_MODEL), att_s, xs, gt_a, sc_f, sh_f, wo, ln1_g[l], ln1_b[l], R)
    y_s = _ffn(h2_s, x1_s, gt_f, wg, wu, wd, ln2_g[l], ln2_b[l], R, 512).reshape(Bs, Ls, D_MODEL)
    kv_s = _kv32(xs, sc_a, sh_a, tab_s, w_kv).reshape(Bs, Ls, KV_W)
    k_s = kv_s[..., :KV_W // 2].reshape(Bs, Ls, ATT_KV_HEADS, ATT_HEAD_DIM)
    v_s = kv_s[..., KV_W // 2:].reshape(Bs, Ls, ATT_KV_HEADS, ATT_HEAD_DIM)

    return (y_p, y_s, k_p[None], v_p[None], s_p[None], k_s[None], v_s[None], s_s[None])
```

```python
import functools

import jax
import jax.numpy as jnp
from jax import lax
from jax.experimental import pallas as pl
from jax.experimental.pallas import tpu as pltpu

F32 = jnp.float32
BF16 = jnp.bfloat16

D_MODEL = 2048
CHUNK = 64
PAST_LEN = 1024
RET_HEADS = 8
RET_DIM = 256
RET_ROPE_THETA = 10000.0
ATT_HEADS = 32
ATT_KV_HEADS = 4
ATT_HEAD_DIM = 64
ATT_GROUP = ATT_HEADS // ATT_KV_HEADS
WINDOW = 128
ROPE_DIM = ATT_HEAD_DIM // 4
ROPE_THETA = 500000.0
D_FF = 5632
DEPTH = 1
ALPHA = (2.0 * DEPTH) ** 0.25
LN_EPS = 1e-5
NEG_INF = -1e30

LANES = 128
KV_W = 2 * ATT_KV_HEADS * ATT_HEAD_DIM
MAIN_W = 7 * D_MODEL
PROJ_TN = D_MODEL
VMEM_LIMIT = 56 * 1024 * 1024

COL_QR, COL_KR, COL_VR, COL_GR, COL_QA, COL_SGR, COL_SGA = (i * D_MODEL for i in range(7))


def _params(sem):
    return pltpu.CompilerParams(dimension_semantics=sem, vmem_limit_bytes=VMEM_LIMIT)


def _mods_kernel(c_ref, w_ref, b_ref, o_ref):
    c = c_ref[...]
    a = (c * jax.nn.sigmoid(c)).astype(BF16)
    o_ref[...] = jnp.dot(a, w_ref[...].astype(BF16), preferred_element_type=F32) + b_ref[...]


def _mods(c_all, w_ada, b_ada):
    rows = c_all.shape[0]
    n_out = w_ada.shape[1]
    tn = 1024
    return pl.pallas_call(
        _mods_kernel,
        grid=(n_out // tn,),
        in_specs=[pl.BlockSpec((rows, D_MODEL), lambda n: (0, 0)),
                  pl.BlockSpec((D_MODEL, tn), lambda n: (0, n)),
                  pl.BlockSpec((1, tn), lambda n: (0, n))],
        out_specs=pl.BlockSpec((rows, tn), lambda n: (0, n)),
        out_shape=jax.ShapeDtypeStruct((rows, n_out), F32),
        compiler_params=_params(("arbitrary",)),
        name="mods",
    )(c_all, w_ada, b_ada.reshape(1, n_out))


def _rope_tables(pos):
    posf = pos.astype(F32)[:, None]
    half_r = RET_DIM // 2
    inv_r = 1.0 / (RET_ROPE_THETA ** (jnp.arange(half_r, dtype=F32) / half_r))
    ang_r = posf * inv_r[None, :]
    half_a = ROPE_DIM // 2
    inv_a = 1.0 / (ROPE_THETA ** (jnp.arange(half_a, dtype=F32) / half_a))
    d = jnp.arange(LANES) % ATT_HEAD_DIM
    ang_a = posf * inv_a[None, :]
    cos_a = jnp.tile(jnp.cos(ang_a), (1, LANES // half_a))
    sin_a = jnp.tile(jnp.sin(ang_a), (1, LANES // half_a))
    ca = jnp.where(d[None, :] < ROPE_DIM, cos_a, 1.0)
    s1 = jnp.where(d[None, :] < half_a, -sin_a, 0.0)
    s2 = jnp.where((d[None, :] >= half_a) & (d[None, :] < ROPE_DIM), sin_a, 0.0)
    return jnp.stack([jnp.cos(ang_r), jnp.sin(ang_r), ca, s1, s2])


def _rot_att(x, ca, s1, s2):
    half = ROPE_DIM // 2
    return x * ca + pltpu.roll(x, LANES - half, 1) * s1 + pltpu.roll(x, half, 1) * s2


def _proj_kernel(x_ref, sc_ref, sh_ref, tab_ref, wa_ref, wb_ref, wkv_ref, o_ref, kv_ref, h_ref):
    n = pl.program_id(2)

    @pl.when(n == 0)
    def _():
        h = x_ref[...] * (1.0 + sc_ref[...]) + sh_ref[...]
        h_ref[...] = h.astype(BF16)
        kv = jnp.dot(h_ref[...], wkv_ref[...], preferred_element_type=F32)
        ca, s1, s2 = tab_ref[2], tab_ref[3], tab_ref[4]
        for s in range(KV_W // LANES):
            xs = kv[:, s * LANES:(s + 1) * LANES]
            if s < KV_W // LANES // 2:
                xs = _rot_att(xs, ca, s1, s2)
            kv_ref[:, s * LANES:(s + 1) * LANES] = xs.astype(BF16)

    def matmul(w_ref=wa_ref):
        return jnp.dot(h_ref[...], w_ref[...], preferred_element_type=F32)

    def rot_ret(scale):
        acc = matmul()
        cr, sr = tab_ref[0], tab_ref[1]
        for j in range(PROJ_TN // RET_DIM):
            a = j * RET_DIM
            x1 = acc[:, a:a + LANES]
            x2 = acc[:, a + LANES:a + 2 * LANES]
            o_ref[:, a:a + LANES] = ((x1 * cr - x2 * sr) * scale).astype(BF16)
            o_ref[:, a + LANES:a + 2 * LANES] = ((x1 * sr + x2 * cr) * scale).astype(BF16)

    @pl.when(n == 0)
    def _():
        rot_ret(1.0)

    @pl.when(n == 1)
    def _():
        rot_ret(RET_DIM ** -0.5)

    @pl.when(n == 2)
    def _():
        o_ref[...] = matmul().astype(BF16)

    @pl.when(n == 3)
    def _():
        acc = matmul()
        o_ref[...] = (acc * jax.nn.sigmoid(acc)).astype(BF16)

    @pl.when(n == 4)
    def _():
        acc = matmul()
        ca, s1, s2 = tab_ref[2], tab_ref[3], tab_ref[4]
        for s in range(PROJ_TN // LANES):
            xs = _rot_att(acc[:, s * LANES:(s + 1) * LANES], ca, s1, s2)
            o_ref[:, s * LANES:(s + 1) * LANES] = (xs * ATT_HEAD_DIM ** -0.5).astype(BF16)

    @pl.when(n >= 5)
    def _():
        o_ref[...] = jax.nn.sigmoid(matmul(wb_ref)).astype(BF16)


def _mod_spec(mod, tm):
    if mod.shape[1] == 1:
        return pl.BlockSpec((None, 1, D_MODEL), lambda b, t, *_: (b, 0, 0))
    return pl.BlockSpec((None, tm, D_MODEL), lambda b, t, *_: (b, t, 0))


def _proj(x, sc, sh, tab, w_a, w_b, w_kv, tm):
    B, T, _ = x.shape
    n_a = w_a.shape[1] // PROJ_TN
    n_tiles = MAIN_W // PROJ_TN
    return pl.pallas_call(
        _proj_kernel,
        grid=(B, T // tm, n_tiles),
        in_specs=[pl.BlockSpec((None, tm, D_MODEL), lambda b, t, n: (b, t, 0)),
                  _mod_spec(sc, tm), _mod_spec(sh, tm),
                  pl.BlockSpec((5, tm, LANES), lambda b, t, n: (0, t, 0)),
                  pl.BlockSpec((D_MODEL, PROJ_TN), lambda b, t, n: (0, jnp.minimum(n, n_a - 1))),
                  pl.BlockSpec((D_MODEL, PROJ_TN), lambda b, t, n: (0, jnp.maximum(n - n_a, 0))),
                  pl.BlockSpec((D_MODEL, KV_W), lambda b, t, n: (0, 0))],
        out_specs=[pl.BlockSpec((None, tm, PROJ_TN), lambda b, t, n: (b, t, n)),
                   pl.BlockSpec((None, tm, KV_W), lambda b, t, n: (b, t, 0))],
        out_shape=[jax.ShapeDtypeStruct((B, T, MAIN_W), BF16),
                   jax.ShapeDtypeStruct((B, T, KV_W), BF16)],
        scratch_shapes=[pltpu.VMEM((tm, D_MODEL), BF16)],
        compiler_params=_params(("arbitrary", "arbitrary", "arbitrary")),
        name="proj",
    )(x, sc, sh, tab, w_a, w_b, w_kv)


def _kv32_kernel(x_ref, sc_ref, sh_ref, tab_ref, wkv_ref, o_ref):
    h = (x_ref[...] * (1.0 + sc_ref[...]) + sh_ref[...]).astype(BF16)
    kv = jnp.dot(h, wkv_ref[...], preferred_element_type=F32)
    ca, s1, s2 = tab_ref[2], tab_ref[3], tab_ref[4]
    for s in range(KV_W // LANES):
        xs = kv[:, s * LANES:(s + 1) * LANES]
        if s < KV_W // LANES // 2:
            xs = _rot_att(xs, ca, s1, s2)
        o_ref[:, s * LANES:(s + 1) * LANES] = xs


def _kv32(x, sc, sh, tab, w_kv):
    B, R, _ = x.shape
    return pl.pallas_call(
        _kv32_kernel,
        grid=(B, 1),
        in_specs=[pl.BlockSpec((None, R, D_MODEL), lambda b, t: (b, 0, 0)),
                  _mod_spec(sc, R), _mod_spec(sh, R),
                  pl.BlockSpec((5, R, LANES), lambda b, t: (0, 0, 0)),
                  pl.BlockSpec((D_MODEL, KV_W), lambda b, t: (0, 0))],
        out_specs=pl.BlockSpec((None, R, KV_W), lambda b, t: (b, 0, 0)),
        out_shape=jax.ShapeDtypeStruct((B, R, KV_W), F32),
        compiler_params=_params(("arbitrary", "arbitrary")),
        name="kv32",
    )(x, sc, sh, tab, w_kv)


def _ret_kernel(lg_ref, q_ref, k_ref, v_ref, gs_ref, sg_ref, gn_ref, s0_ref, o_ref, sout_ref,
                s_ref, dm_ref, qd_ref, kd_ref, *, chunk, n_chunks):
    hh = pl.program_id(1)
    t = pl.program_id(2)
    lg = lg_ref[hh]

    @pl.when(t == 0)
    def _():
        s_ref[...] = s0_ref[...]
        i = lax.broadcasted_iota(jnp.int32, (chunk, chunk), 0)
        j = lax.broadcasted_iota(jnp.int32, (chunk, chunk), 1)
        diff = (i - j).astype(F32)
        dm_ref[...] = jnp.where(diff >= 0, jnp.exp(lg * jnp.maximum(diff, 0.0)), 0.0)
        r = lax.broadcasted_iota(jnp.int32, (chunk, RET_DIM), 0).astype(F32)
        qd_ref[...] = jnp.exp(lg * (r + 1.0))
        kd_ref[...] = jnp.exp(lg * (chunk - 1.0 - r))

    g_chunk = jnp.exp(jnp.full((1, RET_DIM), lg * chunk, F32))
    nt = (((1,), (1,)), ((), ()))
    tn = (((0,), (0,)), ((), ()))
    for c in range(n_chunks):
        rows = pl.ds(c * chunk, chunk)
        q, k, v = q_ref[rows, :], k_ref[rows, :], v_ref[rows, :]
        inner = lax.dot_general(q, k, nt, preferred_element_type=F32) * dm_ref[...]
        o = jnp.dot(inner.astype(BF16), v, preferred_element_type=F32)
        s_prev = s_ref[...]
        o = o + jnp.dot(q, s_prev.astype(BF16), preferred_element_type=F32) * qd_ref[...]
        kdec = (k.astype(F32) * kd_ref[...]).astype(BF16)
        s_ref[...] = s_prev * g_chunk + lax.dot_general(kdec, v, tn, preferred_element_type=F32)
        mu = jnp.mean(o, axis=-1, keepdims=True)
        xc = o - mu
        var = jnp.mean(xc * xc, axis=-1, keepdims=True)
        gn = xc * lax.rsqrt(var + LN_EPS) * gn_ref[...]
        out = sg_ref[rows, :].astype(F32) * (gs_ref[rows, :].astype(F32) * gn)
        o_ref[rows, :] = out.astype(BF16)

    @pl.when(t == pl.num_programs(2) - 1)
    def _():
        sout_ref[...] = s_ref[...]


def _retention(proj, gn_g, s0, chunk, tt):
    B, T, _ = proj.shape
    n_chunks = tt // chunk
    log_gamma = jnp.log1p(-jnp.exp2(-5.0 - jnp.arange(RET_HEADS, dtype=F32)))

    def col(off):
        base = off // RET_DIM
        return pl.BlockSpec((None, tt, RET_DIM), lambda b, h, t: (b, t, base + h))

    return pl.pallas_call(
        functools.partial(_ret_kernel, chunk=chunk, n_chunks=n_chunks),
        grid=(B, RET_HEADS, T // tt),
        in_specs=[pl.BlockSpec(memory_space=pltpu.SMEM),
                  col(COL_QR), col(COL_KR), col(COL_VR), col(COL_GR), col(COL_SGR),
                  pl.BlockSpec((1, RET_DIM), lambda b, h, t: (0, h)),
                  pl.BlockSpec((None, None, RET_DIM, RET_DIM), lambda b, h, t: (b, h, 0, 0))],
        out_specs=[pl.BlockSpec((None, tt, RET_DIM), lambda b, h, t: (b, t, h)),
                   pl.BlockSpec((None, None, RET_DIM, RET_DIM), lambda b, h, t: (b, h, 0, 0))],
        out_shape=[jax.ShapeDtypeStruct((B, T, D_MODEL), BF16),
                   jax.ShapeDtypeStruct((B, RET_HEADS, RET_DIM, RET_DIM), F32)],
        scratch_shapes=[pltpu.VMEM((RET_DIM, RET_DIM), F32),
                        pltpu.VMEM((chunk, chunk), F32),
                        pltpu.VMEM((chunk, RET_DIM), F32),
                        pltpu.VMEM((chunk, RET_DIM), F32)],
        compiler_params=_params(("arbitrary", "arbitrary", "arbitrary")),
        name="retention",
    )(log_gamma, proj, proj, proj, proj, proj, gn_g.reshape(1, D_MODEL), s0)


def _att_kernel(sink_ref, q_ref, halo_ref, cur_ref, sg_ref, o_ref, *, cq, n_chunks, own_valid, mask_first):
    i = pl.program_id(1)
    kv = jnp.concatenate([halo_ref[...], cur_ref[...]], axis=0).astype(F32)
    n_keys = kv.shape[0]
    lane = lax.broadcasted_iota(jnp.int32, (1, LANES), 1)
    lo = lane < ATT_HEAD_DIM
    ones_lo = jnp.broadcast_to(jnp.where(lo, 1.0, 0.0), (n_keys, LANES)).astype(BF16)
    ones_hi = jnp.broadcast_to(jnp.where(lo, 0.0, 1.0), (n_keys, LANES)).astype(BF16)
    own_ok = (lane % ATT_HEAD_DIM) < own_valid
    nt = (((1,), (1,)), ((), ()))
    pairs = ATT_GROUP // 2
    k_off = ATT_KV_HEADS * ATT_HEAD_DIM

    for y in range(ATT_KV_HEADS // 2):
        ks = kv[:, y * LANES:(y + 1) * LANES]
        kr = pltpu.roll(ks, ATT_HEAD_DIM, 1)
        vs = kv[:, k_off + y * LANES:k_off + (y + 1) * LANES]
        vr = pltpu.roll(vs, ATT_HEAD_DIM, 1)
        for par in range(2):
            x = 2 * y + par
            k_even, k_odd = (ks, kr) if par == 0 else (kr, ks)
            v_even, v_odd = (vs, vr) if par == 0 else (vr, vs)
            klo = jnp.where(lo, k_even, 0.0).astype(BF16)
            khi = jnp.where(lo, 0.0, k_odd).astype(BF16)
            vlo = jnp.concatenate([jnp.where(lo, v_even, 0.0).astype(BF16), ones_lo], axis=1)
            vhi = jnp.concatenate([jnp.where(lo, 0.0, v_odd).astype(BF16), ones_hi], axis=1)
            sink_b = jnp.concatenate(
                [jnp.broadcast_to(jnp.where(lo, sink_ref[x * ATT_GROUP + 2 * p],
                                            sink_ref[x * ATT_GROUP + 2 * p + 1]), (cq, LANES))
                 for p in range(pairs)], axis=0)
            for c in range(n_chunks):
                r0 = c * CHUNK
                segs = [slice(r0 + j * CHUNK, r0 + (j + 1) * CHUNK) for j in range(3)]
                kt = jnp.concatenate([a[sl] for sl in segs for a in (klo, khi)], axis=0)
                vb = jnp.concatenate([a[sl] for sl in segs for a in (vlo, vhi)], axis=0)
                col0 = x * ATT_GROUP * ATT_HEAD_DIM
                qp = jnp.concatenate(
                    [q_ref[c * cq:(c + 1) * cq, col0 + p * LANES:col0 + (p + 1) * LANES]
                     for p in range(pairs)], axis=0)
                s = lax.dot_general(qp, kt, nt, preferred_element_type=F32)
                sj = [s[:, j * LANES:(j + 1) * LANES] for j in range(3)]
                if mask_first:
                    for j in range(2):
                        if c + j < 2:
                            sj[j] = jnp.where(i * n_chunks + (c + j - 2) < 0, NEG_INF, sj[j])
                if own_valid < CHUNK:
                    sj[2] = jnp.where(own_ok, sj[2], NEG_INF)
                m3 = jnp.maximum(jnp.maximum(sj[0], sj[1]), sj[2])
                m_even = jnp.max(jnp.where(lo, m3, NEG_INF), axis=1, keepdims=True)
                m_odd = jnp.max(jnp.where(lo, NEG_INF, m3), axis=1, keepdims=True)
                m_b = jnp.maximum(jnp.where(lo, m_even, m_odd), sink_b)
                p_all = jnp.concatenate([jnp.exp(a - m_b).astype(BF16) for a in sj], axis=1)
                oe = jnp.dot(p_all, vb, preferred_element_type=F32)
                den = oe[:, LANES:] + jnp.exp(sink_b - m_b)
                res = oe[:, :LANES] / den
                for p in range(pairs):
                    cols = slice(col0 + p * LANES, col0 + (p + 1) * LANES)
                    rows = slice(c * cq, (c + 1) * cq)
                    gate = sg_ref[rows, cols].astype(F32)
                    o_ref[rows, cols] = (gate * res[p * cq:(p + 1) * cq]).astype(BF16)


def _attention(proj, halo_src, cur_src, sinks, cq, n_chunks, own_valid, mask_first):
    B, T, _ = proj.shape
    tq = cq * n_chunks
    cur_rows = CHUNK * n_chunks
    q_blk, sg_blk = COL_QA // D_MODEL, COL_SGA // D_MODEL
    halo_per_cur = cur_rows // WINDOW if mask_first else 0

    def halo_map(b, i):
        return (b, jnp.maximum(i * halo_per_cur - 1, 0), 0) if mask_first else (b, 0, 0)

    return pl.pallas_call(
        functools.partial(_att_kernel, cq=cq, n_chunks=n_chunks, own_valid=own_valid, mask_first=mask_first),
        grid=(B, T // tq),
        in_specs=[pl.BlockSpec(memory_space=pltpu.SMEM),
                  pl.BlockSpec((None, tq, D_MODEL), lambda b, i: (b, i, q_blk)),
                  pl.BlockSpec((None, WINDOW, KV_W), halo_map),
                  pl.BlockSpec((None, cur_rows, KV_W), lambda b, i: (b, i, 0)),
                  pl.BlockSpec((None, tq, D_MODEL), lambda b, i: (b, i, sg_blk))],
        out_specs=pl.BlockSpec((None, tq, D_MODEL), lambda b, i: (b, i, 0)),
        out_shape=jax.ShapeDtypeStruct((B, T, D_MODEL), BF16),
        compiler_params=_params(("arbitrary", "arbitrary")),
        name="attention",
    )(sinks, proj, halo_src, cur_src, proj)


def _layernorm(z, g, b):
    mu = jnp.mean(z, axis=-1, keepdims=True)
    zc = z - mu
    var = jnp.mean(zc * zc, axis=-1, keepdims=True)
    return zc * lax.rsqrt(var + LN_EPS) * g + b


def _out_kernel(r_ref, a_ref, x_ref, gt_ref, sc_ref, sh_ref, w_ref, g_ref, b_ref, x1_ref, h2_ref):
    n_split = 2 if x_ref.shape[0] % 32 == 0 else 1
    rows_per = x_ref.shape[0] // n_split

    def mod_rows(ref, rows):
        return ref[...] if ref.shape[0] == 1 else ref[rows, :]

    for r in range(n_split):
        rows = slice(r * rows_per, (r + 1) * rows_per)
        merged = r_ref[rows, :] + a_ref[rows, :]
        mix = jnp.dot(merged, w_ref[...], preferred_element_type=F32)
        z = ALPHA * x_ref[rows, :] + mod_rows(gt_ref, rows) * mix
        x1 = _layernorm(z, g_ref[...], b_ref[...])
        x1_ref[rows, :] = x1
        h2_ref[rows, :] = (x1 * (1.0 + mod_rows(sc_ref, rows)) + mod_rows(sh_ref, rows)).astype(BF16)


def _out_proj(ret_m, att_m, x, gt, sc, sh, w_o, g, b, tm):
    B, T, _ = x.shape
    tok = pl.BlockSpec((None, tm, D_MODEL), lambda bb, t: (bb, t, 0))
    vec = pl.BlockSpec((1, D_MODEL), lambda bb, t: (0, 0))
    return pl.pallas_call(
        _out_kernel,
        grid=(B, T // tm),
        in_specs=[tok, tok, tok, _mod_spec(gt, tm), _mod_spec(sc, tm), _mod_spec(sh, tm),
                  pl.BlockSpec((D_MODEL, D_MODEL), lambda bb, t: (0, 0)), vec, vec],
        out_specs=[tok, tok],
        out_shape=[jax.ShapeDtypeStruct((B, T, D_MODEL), F32),
                   jax.ShapeDtypeStruct((B, T, D_MODEL), BF16)],
        compiler_params=_params(("arbitrary", "arbitrary")),
        name="out_proj",
    )(ret_m, att_m, x, gt, sc, sh, w_o, g.reshape(1, D_MODEL), b.reshape(1, D_MODEL))


def _ffn_kernel(h_ref, x1_ref, gt_ref, wg_ref, wu_ref, wd_ref, g_ref, b_ref, o_ref, acc_ref):
    f = pl.program_id(2)

    @pl.when((pl.program_id(0) == 0) & (pl.program_id(1) == 0) & (f == 0))
    def _():
        acc_ref[...] = jnp.zeros_like(acc_ref)

    h = h_ref[...]
    a = jnp.dot(h, wg_ref[...], preferred_element_type=F32)
    u = jnp.dot(h, wu_ref[...], preferred_element_type=F32)
    act = (a * jax.nn.sigmoid(a) * u).astype(BF16)
    part = jnp.dot(act, wd_ref[...], preferred_element_type=F32)
    acc_ref[...] = part + jnp.where(f == 0, 0.0, acc_ref[...])

    @pl.when(f == pl.num_programs(2) - 1)
    def _():
        z = ALPHA * x1_ref[...] + gt_ref[...] * acc_ref[...]
        o_ref[...] = _layernorm(z, g_ref[...], b_ref[...])


def _ffn(h2, x1, gt, wg, wu, wd, g, b, tm, tf):
    B, T, _ = x1.shape
    tok = pl.BlockSpec((None, tm, D_MODEL), lambda bb, t, f: (bb, t, 0))
    vec = pl.BlockSpec((1, D_MODEL), lambda bb, t, f: (0, 0))
    return pl.pallas_call(
        _ffn_kernel,
        grid=(B, T // tm, D_FF // tf),
        in_specs=[tok, tok, _mod_spec(gt, tm),
                  pl.BlockSpec((D_MODEL, tf), lambda bb, t, f: (0, f)),
                  pl.BlockSpec((D_MODEL, tf), lambda bb, t, f: (0, f)),
                  pl.BlockSpec((tf, D_MODEL), lambda bb, t, f: (f, 0)), vec, vec],
        out_specs=tok,
        out_shape=jax.ShapeDtypeStruct((B, T, D_MODEL), F32),
        scratch_shapes=[pltpu.VMEM((tm, D_MODEL), F32)],
        compiler_params=_params(("arbitrary", "arbitrary", "arbitrary")),
        name="ffn",
    )(h2, x1, gt, wg, wu, wd, g.reshape(1, D_MODEL), b.reshape(1, D_MODEL))


def _largest_tile(total, cap):
    t = min(total, cap)
    while total % t:
        t //= 2
    return t


def kernel(x_prompt, x_sample, c_prompt, c_sample, cache_attn_k, cache_attn_v, state_ret, w_ada, b_ada, w_in,
           gn_g, attn_sinks, w_o, ln1_g, ln1_b, w_ffn_gate, w_ffn_up, w_ffn_down, ln2_g, ln2_b):
    B, T, _ = x_prompt.shape
    Bs, Ls, _ = x_sample.shape
    l = 0

    n_c = B + Bs
    pad = (-n_c) % 8
    c_all = jnp.concatenate([c_prompt, c_sample, jnp.zeros((pad, D_MODEL), F32)], axis=0)
    mods = _mods(c_all, w_ada[l], b_ada[l])
    mods_p = [m[:, None, :] for m in jnp.split(mods[:B], 6, axis=-1)]
    mods_s = [jnp.repeat(m, Ls, axis=0)[None] for m in jnp.split(mods[B:n_c], 6, axis=-1)]

    w = w_in[l]
    o_ka = 5 * D_MODEL
    o_gate = o_ka + KV_W
    w_a = w[:, :o_ka].astype(BF16)
    w_b = w[:, o_gate:].astype(BF16)
    w_kv = w[:, o_ka:o_gate].astype(BF16)
    wo, wg, wu, wd = (a[l].astype(BF16) for a in (w_o, w_ffn_gate, w_ffn_up, w_ffn_down))

    def layer(x, mods6, pos, s0, att_fn, tm_proj, tm_out, tm_ffn, ret_chunk, ret_tt):
        sh_a, sc_a, gt_a, sh_f, sc_f, gt_f = mods6
        tab = _rope_tables(pos)
        proj, kvb = _proj(x, sc_a, sh_a, tab, w_a, w_b, w_kv, tm_proj)
        ret_m, s_new = _retention(proj, gn_g[l], s0, ret_chunk, ret_tt)
        att_m = att_fn(proj, kvb)
        x1, h2 = _out_proj(ret_m, att_m, x, gt_a, sc_f, sh_f, wo, ln1_g[l], ln1_b[l], tm_out)
        y = _ffn(h2, x1, gt_f, wg, wu, wd, ln2_g[l], ln2_b[l], tm_ffn, 512)
        return y, s_new, tab

    pos_p = jnp.arange(T)
    n_chunks_p = _largest_tile(T // CHUNK, 4)
    y_p, s_p, tab_p = layer(
        x_prompt, mods_p, pos_p, jnp.zeros((B, RET_HEADS, RET_DIM, RET_DIM), F32),
        lambda proj, kvb: _attention(proj, kvb, kvb, attn_sinks[l], CHUNK, n_chunks_p, CHUNK, True),
        _largest_tile(T, 512), _largest_tile(T, 512), _largest_tile(T, 512),
        _largest_tile(T, 256), _largest_tile(T, 1024))
    kv_p = _kv32(x_prompt[:, T - WINDOW:], mods_p[1], mods_p[0], tab_p[:, T - WINDOW:], w_kv)
    k_p = kv_p[..., :KV_W // 2].reshape(B, WINDOW, ATT_KV_HEADS, ATT_HEAD_DIM)
    v_p = kv_p[..., KV_W // 2:].reshape(B, WINDOW, ATT_KV_HEADS, ATT_HEAD_DIM)

    R = Bs * Ls
    xs = x_sample.reshape(1, R, D_MODEL)
    pos_s = jnp.tile(PAST_LEN + jnp.arange(Ls), Bs)
    cache = jnp.concatenate([cache_attn_k[l].reshape(Bs, WINDOW, KV_W // 2),
                             cache_attn_v[l].reshape(Bs, WINDOW, KV_W // 2)], axis=-1).astype(BF16)

    def att_sample(proj, kvb):
        new = jnp.pad(kvb.reshape(Bs, Ls, KV_W), ((0, 0), (0, CHUNK - Ls), (0, 0)))
        o = _attention(proj.reshape(Bs, Ls, MAIN_W), cache, new, attn_sinks[l], Ls, 1, Ls, False)
        return o.reshape(1, R, D_MODEL)

    def ret_sample(proj, s0):
        return _retention(proj.reshape(Bs, Ls, MAIN_W), gn_g[l], s0, Ls, Ls)

    sh_a, sc_a, gt_a, sh_f, sc_f, gt_f = mods_s
    tab_s = _rope_tables(pos_s)
    proj_s, kvb_s = _proj(xs, sc_a, sh_a, tab_s, w_a, w_b, w_kv, R)
    ret_s, s_s = ret_sample(proj_s, state_ret[l])
    att_s = att_sample(proj_s, kvb_s)
    x1_s, h2_s = _out_proj(ret_s.reshape(1, R, D_MODEL), att_s, xs, gt_a, sc_f, sh_f, wo, ln1_g[l], ln1_b[l], R)
    y_s = _ffn(h2_s, x1_s, gt_f, wg, wu, wd, ln2_g[l], ln2_b[l], R, 512).reshape(Bs, Ls, D_MODEL)
    kv_s = _kv32(xs, sc_a, sh_a, tab_s, w_kv).reshape(Bs, Ls, KV_W)
    k_s = kv_s[..., :KV_W // 2].reshape(Bs, Ls, ATT_KV_HEADS, ATT_HEAD_DIM)
    v_s = kv_s[..., KV_W // 2:].reshape(Bs, Ls, ATT_KV_HEADS, ATT_HEAD_DIM)

    return (y_p, y_s, k_p[None], v_p[None], s_p[None], k_s[None], v_s[None], s_s[None])
```

```python
import functools

import jax
import jax.numpy as jnp
from jax import lax
from jax.experimental import pallas as pl
from jax.experimental.pallas import tpu as pltpu

F32 = jnp.float32
BF16 = jnp.bfloat16

D_MODEL = 2048
CHUNK = 64
PAST_LEN = 1024
RET_HEADS = 8
RET_DIM = 256
RET_ROPE_THETA = 10000.0
ATT_HEADS = 32
ATT_KV_HEADS = 4
ATT_HEAD_DIM = 64
ATT_GROUP = ATT_HEADS // ATT_KV_HEADS
WINDOW = 128
ROPE_DIM = ATT_HEAD_DIM // 4
ROPE_THETA = 500000.0
D_FF = 5632
DEPTH = 1
ALPHA = (2.0 * DEPTH) ** 0.25
LN_EPS = 1e-5
NEG_INF = -1e30

LANES = 128
KV_W = 2 * ATT_KV_HEADS * ATT_HEAD_DIM
MAIN_W = 7 * D_MODEL
PROJ_TN = 1024
PROJ_SPLIT_ROWS = 256
OUT_SPLIT_ROWS = 256
VMEM_LIMIT = 56 * 1024 * 1024

COL_QR, COL_KR, COL_VR, COL_GR, COL_QA, COL_SGR, COL_SGA = (i * D_MODEL for i in range(7))


def _params(sem):
    return pltpu.CompilerParams(dimension_semantics=sem, vmem_limit_bytes=VMEM_LIMIT)


def _mods_kernel(c_ref, w_ref, b_ref, o_ref):
    c = c_ref[...]
    a = (c * jax.nn.sigmoid(c)).astype(BF16)
    o_ref[...] = jnp.dot(a, w_ref[...].astype(BF16), preferred_element_type=F32) + b_ref[...]


def _mods(c_all, w_ada, b_ada):
    rows = c_all.shape[0]
    n_out = w_ada.shape[1]
    tn = 1024
    return pl.pallas_call(
        _mods_kernel,
        grid=(n_out // tn,),
        in_specs=[pl.BlockSpec((rows, D_MODEL), lambda n: (0, 0)),
                  pl.BlockSpec((D_MODEL, tn), lambda n: (0, n)),
                  pl.BlockSpec((1, tn), lambda n: (0, n))],
        out_specs=pl.BlockSpec((rows, tn), lambda n: (0, n)),
        out_shape=jax.ShapeDtypeStruct((rows, n_out), F32),
        compiler_params=_params(("arbitrary",)),
        name="mods",
    )(c_all, w_ada, b_ada.reshape(1, n_out))


def _rope_tables(pos):
    posf = pos.astype(F32)[:, None]
    half_r = RET_DIM // 2
    inv_r = 1.0 / (RET_ROPE_THETA ** (jnp.arange(half_r, dtype=F32) / half_r))
    ang_r = posf * inv_r[None, :]
    half_a = ROPE_DIM // 2
    inv_a = 1.0 / (ROPE_THETA ** (jnp.arange(half_a, dtype=F32) / half_a))
    d = jnp.arange(LANES) % ATT_HEAD_DIM
    ang_a = posf * inv_a[None, :]
    cos_a = jnp.tile(jnp.cos(ang_a), (1, LANES // half_a))
    sin_a = jnp.tile(jnp.sin(ang_a), (1, LANES // half_a))
    ca = jnp.where(d[None, :] < ROPE_DIM, cos_a, 1.0)
    s1 = jnp.where(d[None, :] < half_a, -sin_a, 0.0)
    s2 = jnp.where((d[None, :] >= half_a) & (d[None, :] < ROPE_DIM), sin_a, 0.0)
    return jnp.stack([jnp.cos(ang_r), jnp.sin(ang_r), ca, s1, s2])


def _rot_att(x, ca, s1, s2):
    half = ROPE_DIM // 2
    return x * ca + pltpu.roll(x, LANES - half, 1) * s1 + pltpu.roll(x, half, 1) * s2


def _proj_kernel(x_ref, sc_ref, sh_ref, tab_ref, w_ref, wkv_ref, o_ref, kv_ref, h_ref):
    n = pl.program_id(2)
    tm = x_ref.shape[0]
    n_split = max(tm // PROJ_SPLIT_ROWS, 1)
    row_blocks = [slice(r * (tm // n_split), (r + 1) * (tm // n_split)) for r in range(n_split)]

    def mod_rows(ref, rows):
        return ref[...] if ref.shape[0] == 1 else ref[rows, :]

    def att_tabs(rows):
        return tab_ref[2, rows, :], tab_ref[3, rows, :], tab_ref[4, rows, :]

    @pl.when(n == 0)
    def _():
        for rows in row_blocks:
            h = x_ref[rows, :] * (1.0 + mod_rows(sc_ref, rows)) + mod_rows(sh_ref, rows)
            h_ref[rows, :] = h.astype(BF16)
            kv = jnp.dot(h_ref[rows, :], wkv_ref[...], preferred_element_type=F32)
            ca, s1, s2 = att_tabs(rows)
            for s in range(KV_W // LANES):
                xs = kv[:, s * LANES:(s + 1) * LANES]
                if s < KV_W // LANES // 2:
                    xs = _rot_att(xs, ca, s1, s2)
                kv_ref[rows, s * LANES:(s + 1) * LANES] = xs.astype(BF16)

    def matmul(rows):
        return jnp.dot(h_ref[rows, :], w_ref[...], preferred_element_type=F32)

    def rot_ret(scale):
        for rows in row_blocks:
            acc = matmul(rows)
            cr, sr = tab_ref[0, rows, :], tab_ref[1, rows, :]
            for j in range(PROJ_TN // RET_DIM):
                a = j * RET_DIM
                x1 = acc[:, a:a + LANES]
                x2 = acc[:, a + LANES:a + 2 * LANES]
                o_ref[rows, a:a + LANES] = ((x1 * cr - x2 * sr) * scale).astype(BF16)
                o_ref[rows, a + LANES:a + 2 * LANES] = ((x1 * sr + x2 * cr) * scale).astype(BF16)

    region = n // (D_MODEL // PROJ_TN)

    @pl.when(region == 0)
    def _():
        rot_ret(1.0)

    @pl.when(region == 1)
    def _():
        rot_ret(RET_DIM ** -0.5)

    @pl.when(region == 2)
    def _():
        for rows in row_blocks:
            o_ref[rows, :] = matmul(rows).astype(BF16)

    @pl.when(region == 3)
    def _():
        for rows in row_blocks:
            acc = matmul(rows)
            o_ref[rows, :] = (acc * jax.nn.sigmoid(acc)).astype(BF16)

    @pl.when(region == 4)
    def _():
        for rows in row_blocks:
            acc = matmul(rows)
            ca, s1, s2 = att_tabs(rows)
            for s in range(PROJ_TN // LANES):
                xs = _rot_att(acc[:, s * LANES:(s + 1) * LANES], ca, s1, s2)
                o_ref[rows, s * LANES:(s + 1) * LANES] = (xs * ATT_HEAD_DIM ** -0.5).astype(BF16)

    @pl.when(region >= 5)
    def _():
        for rows in row_blocks:
            o_ref[rows, :] = jax.nn.sigmoid(matmul(rows)).astype(BF16)


def _mod_spec(mod, tm):
    if mod.shape[1] == 1:
        return pl.BlockSpec((None, 1, D_MODEL), lambda b, t, *_: (b, 0, 0))
    return pl.BlockSpec((None, tm, D_MODEL), lambda b, t, *_: (b, t, 0))


def _proj(x, sc, sh, tab, w_main, w_kv, tm):
    B, T, _ = x.shape
    n_tiles = MAIN_W // PROJ_TN
    return pl.pallas_call(
        _proj_kernel,
        grid=(B, T // tm, n_tiles),
        in_specs=[pl.BlockSpec((None, tm, D_MODEL), lambda b, t, n: (b, t, 0)),
                  _mod_spec(sc, tm), _mod_spec(sh, tm),
                  pl.BlockSpec((5, tm, LANES), lambda b, t, n: (0, t, 0)),
                  pl.BlockSpec((D_MODEL, PROJ_TN), lambda b, t, n: (0, n)),
                  pl.BlockSpec((D_MODEL, KV_W), lambda b, t, n: (0, 0), pipeline_mode=pl.Buffered(1))],
        out_specs=[pl.BlockSpec((None, tm, PROJ_TN), lambda b, t, n: (b, t, n)),
                   pl.BlockSpec((None, tm, KV_W), lambda b, t, n: (b, t, 0))],
        out_shape=[jax.ShapeDtypeStruct((B, T, MAIN_W), BF16),
                   jax.ShapeDtypeStruct((B, T, KV_W), BF16)],
        scratch_shapes=[pltpu.VMEM((tm, D_MODEL), BF16)],
        compiler_params=_params(("arbitrary", "arbitrary", "arbitrary")),
        name="proj",
    )(x, sc, sh, tab, w_main, w_kv)


def _repack_kernel(w_ref, o_ref):
    o_ref[...] = w_ref[...].astype(BF16)


def _repack_w_in(w):
    kv_blk = 5 * D_MODEL // KV_W
    return pl.pallas_call(
        _repack_kernel,
        grid=(MAIN_W // KV_W,),
        in_specs=[pl.BlockSpec((D_MODEL, KV_W), lambda n: (0, jnp.where(n >= kv_blk, n + 1, n)))],
        out_specs=pl.BlockSpec((D_MODEL, KV_W), lambda n: (0, n)),
        out_shape=jax.ShapeDtypeStruct((D_MODEL, MAIN_W), BF16),
        compiler_params=_params(("arbitrary",)),
        name="repack_w_in",
    )(w)


def _kv32_kernel(x_ref, sc_ref, sh_ref, tab_ref, wkv_ref, o_ref):
    h = (x_ref[...] * (1.0 + sc_ref[...]) + sh_ref[...]).astype(BF16)
    kv = jnp.dot(h, wkv_ref[...], preferred_element_type=F32)
    ca, s1, s2 = tab_ref[2], tab_ref[3], tab_ref[4]
    for s in range(KV_W // LANES):
        xs = kv[:, s * LANES:(s + 1) * LANES]
        if s < KV_W // LANES // 2:
            xs = _rot_att(xs, ca, s1, s2)
        o_ref[:, s * LANES:(s + 1) * LANES] = xs


def _kv32(x, sc, sh, tab, w_kv):
    B, R, _ = x.shape
    return pl.pallas_call(
        _kv32_kernel,
        grid=(B, 1),
        in_specs=[pl.BlockSpec((None, R, D_MODEL), lambda b, t: (b, 0, 0)),
                  _mod_spec(sc, R), _mod_spec(sh, R),
                  pl.BlockSpec((5, R, LANES), lambda b, t: (0, 0, 0)),
                  pl.BlockSpec((D_MODEL, KV_W), lambda b, t: (0, 0))],
        out_specs=pl.BlockSpec((None, R, KV_W), lambda b, t: (b, 0, 0)),
        out_shape=jax.ShapeDtypeStruct((B, R, KV_W), F32),
        compiler_params=_params(("arbitrary", "arbitrary")),
        name="kv32",
    )(x, sc, sh, tab, w_kv)


def _ret_kernel(lg_ref, q_ref, k_ref, v_ref, gs_ref, sg_ref, gn_ref, s0_ref, o_ref, sout_ref,
                s_ref, dm_ref, qd_ref, kd_ref, *, chunk, n_chunks, heads):
    hg = pl.program_id(1)
    t = pl.program_id(2)
    nt = (((1,), (1,)), ((), ()))
    tn = (((0,), (0,)), ((), ()))
    for hl in range(heads):
        lg = lg_ref[hg * heads + hl]
        cols = slice(hl * RET_DIM, (hl + 1) * RET_DIM)

        @pl.when(t == 0)
        def _():
            s_ref[hl] = s0_ref[hl]
            i = lax.broadcasted_iota(jnp.int32, (chunk, chunk), 0)
            j = lax.broadcasted_iota(jnp.int32, (chunk, chunk), 1)
            diff = (i - j).astype(F32)
            dm_ref[hl] = jnp.where(diff >= 0, jnp.exp(lg * jnp.maximum(diff, 0.0)), 0.0)
            r = lax.broadcasted_iota(jnp.int32, (chunk, RET_DIM), 0).astype(F32)
            qd_ref[hl] = jnp.exp(lg * (r + 1.0))
            kd_ref[hl] = jnp.exp(lg * (chunk - 1.0 - r))

        g_chunk = jnp.exp(jnp.full((1, RET_DIM), lg * chunk, F32))
        for c in range(n_chunks):
            rows = pl.ds(c * chunk, chunk)
            q, k, v = q_ref[rows, cols], k_ref[rows, cols], v_ref[rows, cols]
            inner = lax.dot_general(q, k, nt, preferred_element_type=F32) * dm_ref[hl]
            o = jnp.dot(inner.astype(BF16), v, preferred_element_type=F32)
            s_prev = s_ref[hl]
            o = o + jnp.dot(q, s_prev.astype(BF16), preferred_element_type=F32) * qd_ref[hl]
            kdec = (k.astype(F32) * kd_ref[hl]).astype(BF16)
            s_ref[hl] = s_prev * g_chunk + lax.dot_general(kdec, v, tn, preferred_element_type=F32)
            mu = jnp.mean(o, axis=-1, keepdims=True)
            xc = o - mu
            var = jnp.mean(xc * xc, axis=-1, keepdims=True)
            gn = xc * lax.rsqrt(var + LN_EPS) * gn_ref[:, cols]
            out = sg_ref[rows, cols].astype(F32) * (gs_ref[rows, cols].astype(F32) * gn)
            o_ref[rows, cols] = out.astype(BF16)

    @pl.when(t == pl.num_programs(2) - 1)
    def _():
        sout_ref[...] = s_ref[...]


def _retention(proj, gn_g, s0, chunk, tt, heads):
    B, T, _ = proj.shape
    n_chunks = tt // chunk
    width = heads * RET_DIM
    log_gamma = jnp.log1p(-jnp.exp2(-5.0 - jnp.arange(RET_HEADS, dtype=F32)))

    def col(off):
        base = off // width
        return pl.BlockSpec((None, tt, width), lambda b, h, t: (b, t, base + h))

    state = pl.BlockSpec((None, heads, RET_DIM, RET_DIM), lambda b, h, t: (b, h, 0, 0))
    return pl.pallas_call(
        functools.partial(_ret_kernel, chunk=chunk, n_chunks=n_chunks, heads=heads),
        grid=(B, RET_HEADS // heads, T // tt),
        in_specs=[pl.BlockSpec(memory_space=pltpu.SMEM),
                  col(COL_QR), col(COL_KR), col(COL_VR), col(COL_GR), col(COL_SGR),
                  pl.BlockSpec((1, width), lambda b, h, t: (0, h)), state],
        out_specs=[pl.BlockSpec((None, tt, width), lambda b, h, t: (b, t, h)), state],
        out_shape=[jax.ShapeDtypeStruct((B, T, D_MODEL), BF16),
                   jax.ShapeDtypeStruct((B, RET_HEADS, RET_DIM, RET_DIM), F32)],
        scratch_shapes=[pltpu.VMEM((heads, RET_DIM, RET_DIM), F32),
                        pltpu.VMEM((heads, chunk, chunk), F32),
                        pltpu.VMEM((heads, chunk, RET_DIM), F32),
                        pltpu.VMEM((heads, chunk, RET_DIM), F32)],
        compiler_params=_params(("arbitrary", "arbitrary", "arbitrary")),
        name="retention",
    )(log_gamma, proj, proj, proj, proj, proj, gn_g.reshape(1, D_MODEL), s0)


def _att_kernel(sink_ref, q_ref, halo_ref, cur_ref, sg_ref, o_ref, *, cq, n_chunks, own_valid, mask_first):
    i = pl.program_id(1)
    kv = jnp.concatenate([halo_ref[...], cur_ref[...]], axis=0).astype(F32)
    n_keys = kv.shape[0]
    lane = lax.broadcasted_iota(jnp.int32, (1, LANES), 1)
    lo = lane < ATT_HEAD_DIM
    ones_lo = jnp.broadcast_to(jnp.where(lo, 1.0, 0.0), (n_keys, LANES)).astype(BF16)
    ones_hi = jnp.broadcast_to(jnp.where(lo, 0.0, 1.0), (n_keys, LANES)).astype(BF16)
    own_ok = (lane % ATT_HEAD_DIM) < own_valid
    nt = (((1,), (1,)), ((), ()))
    pairs = ATT_GROUP // 2
    k_off = ATT_KV_HEADS * ATT_HEAD_DIM

    for y in range(ATT_KV_HEADS // 2):
        ks = kv[:, y * LANES:(y + 1) * LANES]
        kr = pltpu.roll(ks, ATT_HEAD_DIM, 1)
        vs = kv[:, k_off + y * LANES:k_off + (y + 1) * LANES]
        vr = pltpu.roll(vs, ATT_HEAD_DIM, 1)
        for par in range(2):
            x = 2 * y + par
            k_even, k_odd = (ks, kr) if par == 0 else (kr, ks)
            v_even, v_odd = (vs, vr) if par == 0 else (vr, vs)
            klo = jnp.where(lo, k_even, 0.0).astype(BF16)
            khi = jnp.where(lo, 0.0, k_odd).astype(BF16)
            vlo = jnp.concatenate([jnp.where(lo, v_even, 0.0).astype(BF16), ones_lo], axis=1)
            vhi = jnp.concatenate([jnp.where(lo, 0.0, v_odd).astype(BF16), ones_hi], axis=1)
            sink_b = jnp.concatenate(
                [jnp.broadcast_to(jnp.where(lo, sink_ref[x * ATT_GROUP + 2 * p],
                                            sink_ref[x * ATT_GROUP + 2 * p + 1]), (cq, LANES))
                 for p in range(pairs)], axis=0)
            for c in range(n_chunks):
                r0 = c * CHUNK
                segs = [slice(r0 + j * CHUNK, r0 + (j + 1) * CHUNK) for j in range(3)]
                kt = jnp.concatenate([a[sl] for sl in segs for a in (klo, khi)], axis=0)
                vb = jnp.concatenate([a[sl] for sl in segs for a in (vlo, vhi)], axis=0)
                col0 = x * ATT_GROUP * ATT_HEAD_DIM
                qp = jnp.concatenate(
                    [q_ref[c * cq:(c + 1) * cq, col0 + p * LANES:col0 + (p + 1) * LANES]
                     for p in range(pairs)], axis=0)
                s = lax.dot_general(qp, kt, nt, preferred_element_type=F32)
                sj = [s[:, j * LANES:(j + 1) * LANES] for j in range(3)]
                if mask_first:
                    for j in range(2):
                        if c + j < 2:
                            sj[j] = jnp.where(i * n_chunks + (c + j - 2) < 0, NEG_INF, sj[j])
                if own_valid < CHUNK:
                    sj[2] = jnp.where(own_ok, sj[2], NEG_INF)
                m3 = jnp.maximum(jnp.maximum(sj[0], sj[1]), sj[2])
                m_even = jnp.max(jnp.where(lo, m3, NEG_INF), axis=1, keepdims=True)
                m_odd = jnp.max(jnp.where(lo, NEG_INF, m3), axis=1, keepdims=True)
                m_b = jnp.maximum(jnp.where(lo, m_even, m_odd), sink_b)
                p_all = jnp.concatenate([jnp.exp(a - m_b).astype(BF16) for a in sj], axis=1)
                oe = jnp.dot(p_all, vb, preferred_element_type=F32)
                den = oe[:, LANES:] + jnp.exp(sink_b - m_b)
                res = oe[:, :LANES] / den
                for p in range(pairs):
                    cols = slice(col0 + p * LANES, col0 + (p + 1) * LANES)
                    rows = slice(c * cq, (c + 1) * cq)
                    gate = sg_ref[rows, cols].astype(F32)
                    o_ref[rows, cols] = (gate * res[p * cq:(p + 1) * cq]).astype(BF16)


def _attention(proj, halo_src, cur_src, sinks, cq, n_chunks, own_valid, mask_first):
    B, T, _ = proj.shape
    tq = cq * n_chunks
    cur_rows = CHUNK * n_chunks
    q_blk, sg_blk = COL_QA // D_MODEL, COL_SGA // D_MODEL
    halo_per_cur = cur_rows // WINDOW if mask_first else 0

    def halo_map(b, i):
        return (b, jnp.maximum(i * halo_per_cur - 1, 0), 0) if mask_first else (b, 0, 0)

    return pl.pallas_call(
        functools.partial(_att_kernel, cq=cq, n_chunks=n_chunks, own_valid=own_valid, mask_first=mask_first),
        grid=(B, T // tq),
        in_specs=[pl.BlockSpec(memory_space=pltpu.SMEM),
                  pl.BlockSpec((None, tq, D_MODEL), lambda b, i: (b, i, q_blk)),
                  pl.BlockSpec((None, WINDOW, KV_W), halo_map),
                  pl.BlockSpec((None, cur_rows, KV_W), lambda b, i: (b, i, 0)),
                  pl.BlockSpec((None, tq, D_MODEL), lambda b, i: (b, i, sg_blk))],
        out_specs=pl.BlockSpec((None, tq, D_MODEL), lambda b, i: (b, i, 0)),
        out_shape=jax.ShapeDtypeStruct((B, T, D_MODEL), BF16),
        compiler_params=_params(("arbitrary", "arbitrary")),
        name="attention",
    )(sinks, proj, halo_src, cur_src, proj)


def _layernorm(z, g, b):
    mu = jnp.mean(z, axis=-1, keepdims=True)
    zc = z - mu
    var = jnp.mean(zc * zc, axis=-1, keepdims=True)
    return zc * lax.rsqrt(var + LN_EPS) * g + b


def _out_kernel(r_ref, a_ref, x_ref, gt_ref, sc_ref, sh_ref, w_ref, g_ref, b_ref, x1_ref, h2_ref):
    n_split = max(x_ref.shape[0] // OUT_SPLIT_ROWS, 1)
    rows_per = x_ref.shape[0] // n_split

    def mod_rows(ref, rows):
        return ref[...] if ref.shape[0] == 1 else ref[rows, :]

    for r in range(n_split):
        rows = slice(r * rows_per, (r + 1) * rows_per)
        merged = r_ref[rows, :] + a_ref[rows, :]
        mix = jnp.dot(merged, w_ref[...], preferred_element_type=F32)
        z = ALPHA * x_ref[rows, :] + mod_rows(gt_ref, rows) * mix
        x1 = _layernorm(z, g_ref[...], b_ref[...])
        x1_ref[rows, :] = x1
        h2_ref[rows, :] = (x1 * (1.0 + mod_rows(sc_ref, rows)) + mod_rows(sh_ref, rows)).astype(BF16)


def _out_proj(ret_m, att_m, x, gt, sc, sh, w_o, g, b, tm):
    B, T, _ = x.shape
    tok = pl.BlockSpec((None, tm, D_MODEL), lambda bb, t: (bb, t, 0))
    vec = pl.BlockSpec((1, D_MODEL), lambda bb, t: (0, 0))
    return pl.pallas_call(
        _out_kernel,
        grid=(B, T // tm),
        in_specs=[tok, tok, tok, _mod_spec(gt, tm), _mod_spec(sc, tm), _mod_spec(sh, tm),
                  pl.BlockSpec((D_MODEL, D_MODEL), lambda bb, t: (0, 0)), vec, vec],
        out_specs=[tok, tok],
        out_shape=[jax.ShapeDtypeStruct((B, T, D_MODEL), F32),
                   jax.ShapeDtypeStruct((B, T, D_MODEL), BF16)],
        compiler_params=_params(("arbitrary", "arbitrary")),
        name="out_proj",
    )(ret_m, att_m, x, gt, sc, sh, w_o, g.reshape(1, D_MODEL), b.reshape(1, D_MODEL))


def _ffn_kernel(h_ref, x1_ref, gt_ref, wg_ref, wu_ref, wd_ref, g_ref, b_ref, o_ref, acc_ref):
    f = pl.program_id(2)

    @pl.when((pl.program_id(0) == 0) & (pl.program_id(1) == 0) & (f == 0))
    def _():
        acc_ref[...] = jnp.zeros_like(acc_ref)

    h = h_ref[...]
    a = jnp.dot(h, wg_ref[...], preferred_element_type=F32)
    u = jnp.dot(h, wu_ref[...], preferred_element_type=F32)
    act = (a * jax.nn.sigmoid(a) * u).astype(BF16)
    part = jnp.dot(act, wd_ref[...], preferred_element_type=F32)
    acc_ref[...] = part + jnp.where(f == 0, 0.0, acc_ref[...])

    @pl.when(f == pl.num_programs(2) - 1)
    def _():
        z = ALPHA * x1_ref[...] + gt_ref[...] * acc_ref[...]
        o_ref[...] = _layernorm(z, g_ref[...], b_ref[...])


def _ffn(h2, x1, gt, wg, wu, wd, g, b, tm, tf):
    B, T, _ = x1.shape
    tok = pl.BlockSpec((None, tm, D_MODEL), lambda bb, t, f: (bb, t, 0))
    vec = pl.BlockSpec((1, D_MODEL), lambda bb, t, f: (0, 0))
    return pl.pallas_call(
        _ffn_kernel,
        grid=(B, T // tm, D_FF // tf),
        in_specs=[tok, tok, _mod_spec(gt, tm),
                  pl.BlockSpec((D_MODEL, tf), lambda bb, t, f: (0, f)),
                  pl.BlockSpec((D_MODEL, tf), lambda bb, t, f: (0, f)),
                  pl.BlockSpec((tf, D_MODEL), lambda bb, t, f: (f, 0)), vec, vec],
        out_specs=tok,
        out_shape=jax.ShapeDtypeStruct((B, T, D_MODEL), F32),
        scratch_shapes=[pltpu.VMEM((tm, D_MODEL), F32)],
        compiler_params=_params(("arbitrary", "arbitrary", "arbitrary")),
        name="ffn",
    )(h2, x1, gt, wg, wu, wd, g.reshape(1, D_MODEL), b.reshape(1, D_MODEL))


def _largest_tile(total, cap):
    t = min(total, cap)
    while total % t:
        t //= 2
    return t


def kernel(x_prompt, x_sample, c_prompt, c_sample, cache_attn_k, cache_attn_v, state_ret, w_ada, b_ada, w_in,
           gn_g, attn_sinks, w_o, ln1_g, ln1_b, w_ffn_gate, w_ffn_up, w_ffn_down, ln2_g, ln2_b):
    B, T, _ = x_prompt.shape
    Bs, Ls, _ = x_sample.shape
    l = 0

    n_c = B + Bs
    pad = (-n_c) % 8
    c_all = jnp.concatenate([c_prompt, c_sample, jnp.zeros((pad, D_MODEL), F32)], axis=0)
    mods = _mods(c_all, w_ada[l], b_ada[l])
    mods_p = [m[:, None, :] for m in jnp.split(mods[:B], 6, axis=-1)]
    mods_s = [jnp.repeat(m, Ls, axis=0)[None] for m in jnp.split(mods[B:n_c], 6, axis=-1)]

    w = w_in[l]
    o_ka = 5 * D_MODEL
    o_gate = o_ka + KV_W
    w_main = _repack_w_in(w)
    w_kv = w[:, o_ka:o_gate].astype(BF16)
    wo, wg, wu, wd = (a[l].astype(BF16) for a in (w_o, w_ffn_gate, w_ffn_up, w_ffn_down))

    def layer(x, mods6, pos, s0, att_fn, tm_proj, tm_out, tm_ffn, ret_chunk, ret_tt):
        sh_a, sc_a, gt_a, sh_f, sc_f, gt_f = mods6
        tab = _rope_tables(pos)
        proj, kvb = _proj(x, sc_a, sh_a, tab, w_main, w_kv, tm_proj)
        ret_m, s_new = _retention(proj, gn_g[l], s0, ret_chunk, ret_tt, 1)
        att_m = att_fn(proj, kvb)
        x1, h2 = _out_proj(ret_m, att_m, x, gt_a, sc_f, sh_f, wo, ln1_g[l], ln1_b[l], tm_out)
        y = _ffn(h2, x1, gt_f, wg, wu, wd, ln2_g[l], ln2_b[l], tm_ffn, 512)
        return y, s_new, tab

    pos_p = jnp.arange(T)
    n_chunks_p = _largest_tile(T // CHUNK, 8)
    y_p, s_p, tab_p = layer(
        x_prompt, mods_p, pos_p, jnp.zeros((B, RET_HEADS, RET_DIM, RET_DIM), F32),
        lambda proj, kvb: _attention(proj, kvb, kvb, attn_sinks[l], CHUNK, n_chunks_p, CHUNK, True),
        _largest_tile(T, 1024), _largest_tile(T, 512), _largest_tile(T, 512),
        _largest_tile(T, 256), _largest_tile(T, 2048))
    kv_p = _kv32(x_prompt[:, T - WINDOW:], mods_p[1], mods_p[0], tab_p[:, T - WINDOW:], w_kv)
    k_p = kv_p[..., :KV_W // 2].reshape(B, WINDOW, ATT_KV_HEADS, ATT_HEAD_DIM)
    v_p = kv_p[..., KV_W // 2:].reshape(B, WINDOW, ATT_KV_HEADS, ATT_HEAD_DIM)

    R = Bs * Ls
    xs = x_sample.reshape(1, R, D_MODEL)
    pos_s = jnp.tile(PAST_LEN + jnp.arange(Ls), Bs)
    cache = jnp.concatenate([cache_attn_k[l].reshape(Bs, WINDOW, KV_W // 2),
                             cache_attn_v[l].reshape(Bs, WINDOW, KV_W // 2)], axis=-1).astype(BF16)

    def att_sample(proj, kvb):
        new = jnp.pad(kvb.reshape(Bs, Ls, KV_W), ((0, 0), (0, CHUNK - Ls), (0, 0)))
        o = _attention(proj.reshape(Bs, Ls, MAIN_W), cache, new, attn_sinks[l], Ls, 1, Ls, False)
        return o.reshape(1, R, D_MODEL)

    def ret_sample(proj, s0):
        return _retention(proj.reshape(Bs, Ls, MAIN_W), gn_g[l], s0, Ls, Ls, RET_HEADS)

    sh_a, sc_a, gt_a, sh_f, sc_f, gt_f = mods_s
    tab_s = _rope_tables(pos_s)
    proj_s, kvb_s = _proj(xs, sc_a, sh_a, tab_s, w_main, w_kv, R)
    ret_s, s_s = ret_sample(proj_s, state_ret[l])
    att_s = att_sample(proj_s, kvb_s)
    x1_s, h2_s = _out_proj(ret_s.reshape(1, R, D_MODEL), att_s, xs, gt_a, sc_f, sh_f, wo, ln1_g[l], ln1_b[l], R)
    y_s = _ffn(h2_s, x1_s, gt_f, wg, wu, wd, ln2_g[l], ln2_b[l], R, 512).reshape(Bs, Ls, D_MODEL)
    kv_s = _kv32(xs, sc_a, sh_a, tab_s, w_kv).reshape(Bs, Ls, KV_W)
    k_s = kv_s[..., :KV_W // 2].reshape(Bs, Ls, ATT_KV_HEADS, ATT_HEAD_DIM)
    v_s = kv_s[..., KV_W // 2:].reshape(Bs, Ls, ATT_KV_HEADS, ATT_HEAD_DIM)

    return (y_p, y_s, k_p[None], v_p[None], s_p[None], k_s[None], v_s[None], s_s[None])
```

```python
import functools

import jax
import jax.numpy as jnp
from jax import lax
from jax.experimental import pallas as pl
from jax.experimental.pallas import tpu as pltpu

F32 = jnp.float32
BF16 = jnp.bfloat16

D_MODEL = 2048
CHUNK = 64
PAST_LEN = 1024
RET_HEADS = 8
RET_DIM = 256
RET_ROPE_THETA = 10000.0
ATT_HEADS = 32
ATT_KV_HEADS = 4
ATT_HEAD_DIM = 64
ATT_GROUP = ATT_HEADS // ATT_KV_HEADS
WINDOW = 128
ROPE_DIM = ATT_HEAD_DIM // 4
ROPE_THETA = 500000.0
D_FF = 5632
DEPTH = 1
ALPHA = (2.0 * DEPTH) ** 0.25
LN_EPS = 1e-5
NEG_INF = -1e30
LOG2E = 1.4426950408889634

LANES = 128
KV_W = 2 * ATT_KV_HEADS * ATT_HEAD_DIM
MAIN_W = 7 * D_MODEL
PROJ_TN = 1024
PROJ_SPLIT_ROWS = 256
FFN_TF = 512
OUT_SPLIT_ROWS = 256
VMEM_LIMIT = 56 * 1024 * 1024

COL_QR, COL_KR, COL_VR, COL_GR, COL_QA, COL_SGR, COL_SGA = (i * D_MODEL for i in range(7))


def _params(sem):
    return pltpu.CompilerParams(dimension_semantics=sem, vmem_limit_bytes=VMEM_LIMIT)


def _mods_kernel(c_ref, w_ref, b_ref, o_ref):
    c = c_ref[...]
    a = (c * jax.nn.sigmoid(c)).astype(BF16)
    o_ref[...] = jnp.dot(a, w_ref[...].astype(BF16), preferred_element_type=F32) + b_ref[...]


def _mods(c_all, w_ada, b_ada):
    rows = c_all.shape[0]
    n_out = w_ada.shape[1]
    tn = 1024
    return pl.pallas_call(
        _mods_kernel,
        grid=(n_out // tn,),
        in_specs=[pl.BlockSpec((rows, D_MODEL), lambda n: (0, 0)),
                  pl.BlockSpec((D_MODEL, tn), lambda n: (0, n)),
                  pl.BlockSpec((1, tn), lambda n: (0, n))],
        out_specs=pl.BlockSpec((rows, tn), lambda n: (0, n)),
        out_shape=jax.ShapeDtypeStruct((rows, n_out), F32),
        compiler_params=_params(("arbitrary",)),
        name="mods",
    )(c_all, w_ada, b_ada.reshape(1, n_out))


def _rope_tables(pos):
    posf = pos.astype(F32)[:, None]
    half_r = RET_DIM // 2
    inv_r = 1.0 / (RET_ROPE_THETA ** (jnp.arange(half_r, dtype=F32) / half_r))
    ang_r = posf * inv_r[None, :]
    half_a = ROPE_DIM // 2
    inv_a = 1.0 / (ROPE_THETA ** (jnp.arange(half_a, dtype=F32) / half_a))
    d = jnp.arange(LANES) % ATT_HEAD_DIM
    ang_a = posf * inv_a[None, :]
    cos_a = jnp.tile(jnp.cos(ang_a), (1, LANES // half_a))
    sin_a = jnp.tile(jnp.sin(ang_a), (1, LANES // half_a))
    ca = jnp.where(d[None, :] < ROPE_DIM, cos_a, 1.0)
    s1 = jnp.where(d[None, :] < half_a, -sin_a, 0.0)
    s2 = jnp.where((d[None, :] >= half_a) & (d[None, :] < ROPE_DIM), sin_a, 0.0)
    return jnp.stack([jnp.cos(ang_r), jnp.sin(ang_r), ca, s1, s2])


def _rot_att(x, ca, s1, s2):
    half = ROPE_DIM // 2
    return x * ca + pltpu.roll(x, LANES - half, 1) * s1 + pltpu.roll(x, half, 1) * s2


def _proj_kernel(x_ref, sc_ref, sh_ref, tab_ref, w_ref, wkv_ref, o_ref, kv_ref, h_ref):
    n = pl.program_id(2)
    tm = x_ref.shape[0]
    n_split = max(tm // PROJ_SPLIT_ROWS, 1)
    row_blocks = [slice(r * (tm // n_split), (r + 1) * (tm // n_split)) for r in range(n_split)]

    def mod_rows(ref, rows):
        return ref[...] if ref.shape[0] == 1 else ref[rows, :]

    def att_tabs(rows):
        return tab_ref[2, rows, :], tab_ref[3, rows, :], tab_ref[4, rows, :]

    @pl.when(n == 0)
    def _():
        for rows in row_blocks:
            h = x_ref[rows, :] * (1.0 + mod_rows(sc_ref, rows)) + mod_rows(sh_ref, rows)
            h_ref[rows, :] = h.astype(BF16)
            kv = jnp.dot(h_ref[rows, :], wkv_ref[...], preferred_element_type=F32)
            ca, s1, s2 = att_tabs(rows)
            for s in range(KV_W // LANES):
                xs = kv[:, s * LANES:(s + 1) * LANES]
                if s < KV_W // LANES // 2:
                    xs = _rot_att(xs, ca, s1, s2)
                kv_ref[rows, s * LANES:(s + 1) * LANES] = xs.astype(BF16)

    def matmul(rows):
        return jnp.dot(h_ref[rows, :], w_ref[...], preferred_element_type=F32)

    def rot_ret(scale):
        for rows in row_blocks:
            acc = matmul(rows)
            cr, sr = tab_ref[0, rows, :], tab_ref[1, rows, :]
            for j in range(PROJ_TN // RET_DIM):
                a = j * RET_DIM
                x1 = acc[:, a:a + LANES]
                x2 = acc[:, a + LANES:a + 2 * LANES]
                o_ref[rows, a:a + LANES] = ((x1 * cr - x2 * sr) * scale).astype(BF16)
                o_ref[rows, a + LANES:a + 2 * LANES] = ((x1 * sr + x2 * cr) * scale).astype(BF16)

    region = n // (D_MODEL // PROJ_TN)

    @pl.when(region == 0)
    def _():
        rot_ret(1.0)

    @pl.when(region == 1)
    def _():
        rot_ret(RET_DIM ** -0.5)

    @pl.when(region == 2)
    def _():
        for rows in row_blocks:
            o_ref[rows, :] = matmul(rows).astype(BF16)

    @pl.when(region == 3)
    def _():
        for rows in row_blocks:
            acc = matmul(rows)
            o_ref[rows, :] = (acc * jax.nn.sigmoid(acc)).astype(BF16)

    @pl.when(region == 4)
    def _():
        for rows in row_blocks:
            acc = matmul(rows)
            ca, s1, s2 = att_tabs(rows)
            for s in range(PROJ_TN // LANES):
                xs = _rot_att(acc[:, s * LANES:(s + 1) * LANES], ca, s1, s2)
                o_ref[rows, s * LANES:(s + 1) * LANES] = (xs * (ATT_HEAD_DIM ** -0.5 * LOG2E)).astype(BF16)

    @pl.when(region >= 5)
    def _():
        for rows in row_blocks:
            o_ref[rows, :] = jax.nn.sigmoid(matmul(rows)).astype(BF16)


def _mod_spec(mod, tm):
    if mod.shape[1] == 1:
        return pl.BlockSpec((None, 1, D_MODEL), lambda b, t, *_: (b, 0, 0))
    return pl.BlockSpec((None, tm, D_MODEL), lambda b, t, *_: (b, t, 0))


def _proj(x, sc, sh, tab, w_main, w_kv, tm):
    B, T, _ = x.shape
    n_tiles = MAIN_W // PROJ_TN
    return pl.pallas_call(
        _proj_kernel,
        grid=(B, T // tm, n_tiles),
        in_specs=[pl.BlockSpec((None, tm, D_MODEL), lambda b, t, n: (b, t, 0)),
                  _mod_spec(sc, tm), _mod_spec(sh, tm),
                  pl.BlockSpec((5, tm, LANES), lambda b, t, n: (0, t, 0)),
                  pl.BlockSpec((D_MODEL, PROJ_TN), lambda b, t, n: (0, n)),
                  pl.BlockSpec((D_MODEL, KV_W), lambda b, t, n: (0, 0), pipeline_mode=pl.Buffered(1))],
        out_specs=[pl.BlockSpec((None, tm, PROJ_TN), lambda b, t, n: (b, t, n)),
                   pl.BlockSpec((None, tm, KV_W), lambda b, t, n: (b, t, 0))],
        out_shape=[jax.ShapeDtypeStruct((B, T, MAIN_W), BF16),
                   jax.ShapeDtypeStruct((B, T, KV_W), BF16)],
        scratch_shapes=[pltpu.VMEM((tm, D_MODEL), BF16)],
        compiler_params=_params(("arbitrary", "arbitrary", "arbitrary")),
        name="proj",
    )(x, sc, sh, tab, w_main, w_kv)


def _repack_kernel(w_ref, o_ref):
    o_ref[...] = w_ref[...].astype(BF16)


def _repack_w_in(w):
    kv_blk = 5 * D_MODEL // KV_W
    return pl.pallas_call(
        _repack_kernel,
        grid=(MAIN_W // KV_W,),
        in_specs=[pl.BlockSpec((D_MODEL, KV_W), lambda n: (0, jnp.where(n >= kv_blk, n + 1, n)))],
        out_specs=pl.BlockSpec((D_MODEL, KV_W), lambda n: (0, n)),
        out_shape=jax.ShapeDtypeStruct((D_MODEL, MAIN_W), BF16),
        compiler_params=_params(("arbitrary",)),
        name="repack_w_in",
    )(w)


def _kv32_kernel(x_ref, sc_ref, sh_ref, tab_ref, wkv_ref, o_ref):
    h = (x_ref[...] * (1.0 + sc_ref[...]) + sh_ref[...]).astype(BF16)
    kv = jnp.dot(h, wkv_ref[...], preferred_element_type=F32)
    ca, s1, s2 = tab_ref[2], tab_ref[3], tab_ref[4]
    for s in range(KV_W // LANES):
        xs = kv[:, s * LANES:(s + 1) * LANES]
        if s < KV_W // LANES // 2:
            xs = _rot_att(xs, ca, s1, s2)
        o_ref[:, s * LANES:(s + 1) * LANES] = xs


def _kv32(x, sc, sh, tab, w_kv):
    B, R, _ = x.shape
    return pl.pallas_call(
        _kv32_kernel,
        grid=(B, 1),
        in_specs=[pl.BlockSpec((None, R, D_MODEL), lambda b, t: (b, 0, 0)),
                  _mod_spec(sc, R), _mod_spec(sh, R),
                  pl.BlockSpec((5, R, LANES), lambda b, t: (0, 0, 0)),
                  pl.BlockSpec((D_MODEL, KV_W), lambda b, t: (0, 0))],
        out_specs=pl.BlockSpec((None, R, KV_W), lambda b, t: (b, 0, 0)),
        out_shape=jax.ShapeDtypeStruct((B, R, KV_W), F32),
        compiler_params=_params(("arbitrary", "arbitrary")),
        name="kv32",
    )(x, sc, sh, tab, w_kv)


def _ret_kernel(lg_ref, q_ref, k_ref, v_ref, gs_ref, sg_ref, gn_ref, s0_ref, o_ref, sout_ref,
                s_ref, dm_ref, qd_ref, kd_ref, *, chunk, n_chunks, heads):
    hg = pl.program_id(1)
    t = pl.program_id(2)
    nt = (((1,), (1,)), ((), ()))
    tn = (((0,), (0,)), ((), ()))
    for hl in range(heads):
        lg = lg_ref[hg * heads + hl]
        cols = slice(hl * RET_DIM, (hl + 1) * RET_DIM)

        @pl.when(t == 0)
        def _():
            s_ref[hl] = s0_ref[hl]
            i = lax.broadcasted_iota(jnp.int32, (chunk, chunk), 0)
            j = lax.broadcasted_iota(jnp.int32, (chunk, chunk), 1)
            diff = (i - j).astype(F32)
            dm_ref[hl] = jnp.where(diff >= 0, jnp.exp(lg * jnp.maximum(diff, 0.0)), 0.0)
            r = lax.broadcasted_iota(jnp.int32, (chunk, RET_DIM), 0).astype(F32)
            qd_ref[hl] = jnp.exp(lg * (r + 1.0))
            kd_ref[hl] = jnp.exp(lg * (chunk - 1.0 - r))

        g_chunk = jnp.exp(jnp.full((1, RET_DIM), lg * chunk, F32))
        for c in range(n_chunks):
            rows = pl.ds(c * chunk, chunk)
            q, k, v = q_ref[rows, cols], k_ref[rows, cols], v_ref[rows, cols]
            inner = lax.dot_general(q, k, nt, preferred_element_type=F32) * dm_ref[hl]
            o = jnp.dot(inner.astype(BF16), v, preferred_element_type=F32)
            s_prev = s_ref[hl]
            o = o + jnp.dot(q, s_prev.astype(BF16), preferred_element_type=F32) * qd_ref[hl]
            kdec = (k.astype(F32) * kd_ref[hl]).astype(BF16)
            s_ref[hl] = s_prev * g_chunk + lax.dot_general(kdec, v, tn, preferred_element_type=F32)
            mu = jnp.mean(o, axis=-1, keepdims=True)
            xc = o - mu
            var = jnp.mean(xc * xc, axis=-1, keepdims=True)
            gn = xc * lax.rsqrt(var + LN_EPS) * gn_ref[:, cols]
            out = sg_ref[rows, cols].astype(F32) * (gs_ref[rows, cols].astype(F32) * gn)
            o_ref[rows, cols] = out.astype(BF16)

    @pl.when(t == pl.num_programs(2) - 1)
    def _():
        sout_ref[...] = s_ref[...]


def _retention(proj, gn_g, s0, chunk, tt, heads):
    B, T, _ = proj.shape
    n_chunks = tt // chunk
    width = heads * RET_DIM
    log_gamma = jnp.log1p(-jnp.exp2(-5.0 - jnp.arange(RET_HEADS, dtype=F32)))

    def col(off):
        base = off // width
        return pl.BlockSpec((None, tt, width), lambda b, h, t: (b, t, base + h))

    state = pl.BlockSpec((None, heads, RET_DIM, RET_DIM), lambda b, h, t: (b, h, 0, 0))
    return pl.pallas_call(
        functools.partial(_ret_kernel, chunk=chunk, n_chunks=n_chunks, heads=heads),
        grid=(B, RET_HEADS // heads, T // tt),
        in_specs=[pl.BlockSpec(memory_space=pltpu.SMEM),
                  col(COL_QR), col(COL_KR), col(COL_VR), col(COL_GR), col(COL_SGR),
                  pl.BlockSpec((1, width), lambda b, h, t: (0, h)), state],
        out_specs=[pl.BlockSpec((None, tt, width), lambda b, h, t: (b, t, h)), state],
        out_shape=[jax.ShapeDtypeStruct((B, T, D_MODEL), BF16),
                   jax.ShapeDtypeStruct((B, RET_HEADS, RET_DIM, RET_DIM), F32)],
        scratch_shapes=[pltpu.VMEM((heads, RET_DIM, RET_DIM), F32),
                        pltpu.VMEM((heads, chunk, chunk), F32),
                        pltpu.VMEM((heads, chunk, RET_DIM), F32),
                        pltpu.VMEM((heads, chunk, RET_DIM), F32)],
        compiler_params=_params(("arbitrary", "arbitrary", "arbitrary")),
        name="retention",
    )(log_gamma, proj, proj, proj, proj, proj, gn_g.reshape(1, D_MODEL), s0)


def _att_kernel(sink_ref, q_ref, halo_ref, cur_ref, sg_ref, o_ref, *, cq, n_chunks, own_valid, mask_first):
    i = pl.program_id(1)
    kv = jnp.concatenate([halo_ref[...], cur_ref[...]], axis=0).astype(F32)
    n_keys = kv.shape[0]
    n_key_chunks = n_keys // CHUNK
    lane = lax.broadcasted_iota(jnp.int32, (1, LANES), 1)
    lo = lane < ATT_HEAD_DIM
    ones_lo = jnp.broadcast_to(jnp.where(lo, 1.0, 0.0), (n_keys, LANES)).astype(BF16)
    ones_hi = jnp.broadcast_to(jnp.where(lo, 0.0, 1.0), (n_keys, LANES)).astype(BF16)
    own_ok = (lane % ATT_HEAD_DIM) < own_valid
    pairs = ATT_GROUP // 2
    k_off = ATT_KV_HEADS * ATT_HEAD_DIM

    for y in range(ATT_KV_HEADS // 2):
        ks = kv[:, y * LANES:(y + 1) * LANES]
        kr = pltpu.roll(ks, ATT_HEAD_DIM, 1)
        vs = kv[:, k_off + y * LANES:k_off + (y + 1) * LANES]
        vr = pltpu.roll(vs, ATT_HEAD_DIM, 1)
        for par in range(2):
            x = 2 * y + par
            k_even, k_odd = (ks, kr) if par == 0 else (kr, ks)
            v_even, v_odd = (vs, vr) if par == 0 else (vr, vs)
            klo = jnp.where(lo, k_even, 0.0)
            khi = jnp.where(lo, 0.0, k_odd)
            kt_all = jnp.concatenate(
                [a[j * CHUNK:(j + 1) * CHUNK] for j in range(n_key_chunks) for a in (klo, khi)], axis=0).T.astype(BF16)
            vlo = jnp.concatenate([jnp.where(lo, v_even, 0.0).astype(BF16), ones_lo], axis=1)
            vhi = jnp.concatenate([jnp.where(lo, 0.0, v_odd).astype(BF16), ones_hi], axis=1)
            sink_b = jnp.concatenate(
                [jnp.broadcast_to(jnp.where(lo, sink_ref[x * ATT_GROUP + 2 * p],
                                            sink_ref[x * ATT_GROUP + 2 * p + 1]), (cq, LANES))
                 for p in range(pairs)], axis=0) * LOG2E
            for c in range(n_chunks):
                r0 = c * CHUNK
                segs = [slice(r0 + j * CHUNK, r0 + (j + 1) * CHUNK) for j in range(3)]
                vb = jnp.concatenate([a[sl] for sl in segs for a in (vlo, vhi)], axis=0)
                col0 = x * ATT_GROUP * ATT_HEAD_DIM
                qp = jnp.concatenate(
                    [q_ref[c * cq:(c + 1) * cq, col0 + p * LANES:col0 + (p + 1) * LANES]
                     for p in range(pairs)], axis=0)
                s = jnp.dot(qp, kt_all[:, c * LANES:(c + 3) * LANES], preferred_element_type=F32)
                sj = [s[:, j * LANES:(j + 1) * LANES] for j in range(3)]
                if mask_first:
                    for j in range(2):
                        if c + j < 2:
                            sj[j] = jnp.where(i * n_chunks + (c + j - 2) < 0, NEG_INF, sj[j])
                if own_valid < CHUNK:
                    sj[2] = jnp.where(own_ok, sj[2], NEG_INF)
                m3 = jnp.maximum(jnp.maximum(sj[0], sj[1]), sj[2])
                m_even = jnp.max(jnp.where(lo, m3, NEG_INF), axis=1, keepdims=True)
                m_odd = jnp.max(jnp.where(lo, NEG_INF, m3), axis=1, keepdims=True)
                m_b = jnp.maximum(jnp.where(lo, m_even, m_odd), sink_b)
                p_all = jnp.concatenate([jnp.exp2(a - m_b).astype(BF16) for a in sj], axis=1)
                oe = jnp.dot(p_all, vb, preferred_element_type=F32)
                den = oe[:, LANES:] + jnp.exp2(sink_b - m_b)
                res = oe[:, :LANES] / den
                for p in range(pairs):
                    cols = slice(col0 + p * LANES, col0 + (p + 1) * LANES)
                    rows = slice(c * cq, (c + 1) * cq)
                    gate = sg_ref[rows, cols].astype(F32)
                    o_ref[rows, cols] = (gate * res[p * cq:(p + 1) * cq]).astype(BF16)


def _attention(proj, halo_src, cur_src, sinks, cq, n_chunks, own_valid, mask_first):
    B, T, _ = proj.shape
    tq = cq * n_chunks
    cur_rows = CHUNK * n_chunks
    q_blk, sg_blk = COL_QA // D_MODEL, COL_SGA // D_MODEL
    halo_per_cur = cur_rows // WINDOW if mask_first else 0

    def halo_map(b, i):
        return (b, jnp.maximum(i * halo_per_cur - 1, 0), 0) if mask_first else (b, 0, 0)

    return pl.pallas_call(
        functools.partial(_att_kernel, cq=cq, n_chunks=n_chunks, own_valid=own_valid, mask_first=mask_first),
        grid=(B, T // tq),
        in_specs=[pl.BlockSpec(memory_space=pltpu.SMEM),
                  pl.BlockSpec((None, tq, D_MODEL), lambda b, i: (b, i, q_blk)),
                  pl.BlockSpec((None, WINDOW, KV_W), halo_map),
                  pl.BlockSpec((None, cur_rows, KV_W), lambda b, i: (b, i, 0)),
                  pl.BlockSpec((None, tq, D_MODEL), lambda b, i: (b, i, sg_blk))],
        out_specs=pl.BlockSpec((None, tq, D_MODEL), lambda b, i: (b, i, 0)),
        out_shape=jax.ShapeDtypeStruct((B, T, D_MODEL), BF16),
        compiler_params=_params(("arbitrary", "arbitrary")),
        name="attention",
    )(sinks, proj, halo_src, cur_src, proj)


def _layernorm(z, g, b):
    mu = jnp.mean(z, axis=-1, keepdims=True)
    zc = z - mu
    var = jnp.mean(zc * zc, axis=-1, keepdims=True)
    return zc * lax.rsqrt(var + LN_EPS) * g + b


def _out_kernel(r_ref, a_ref, x_ref, gt_ref, sc_ref, sh_ref, w_ref, g_ref, b_ref, x1_ref, h2_ref):
    n_split = max(x_ref.shape[0] // OUT_SPLIT_ROWS, 1)
    rows_per = x_ref.shape[0] // n_split

    def mod_rows(ref, rows):
        return ref[...] if ref.shape[0] == 1 else ref[rows, :]

    for r in range(n_split):
        rows = slice(r * rows_per, (r + 1) * rows_per)
        merged = r_ref[rows, :] + a_ref[rows, :]
        mix = jnp.dot(merged, w_ref[...], preferred_element_type=F32)
        z = ALPHA * x_ref[rows, :] + mod_rows(gt_ref, rows) * mix
        x1 = _layernorm(z, g_ref[...], b_ref[...])
        x1_ref[rows, :] = x1
        h2_ref[rows, :] = (x1 * (1.0 + mod_rows(sc_ref, rows)) + mod_rows(sh_ref, rows)).astype(BF16)


def _out_proj(ret_m, att_m, x, gt, sc, sh, w_o, g, b, tm):
    B, T, _ = x.shape
    tok = pl.BlockSpec((None, tm, D_MODEL), lambda bb, t: (bb, t, 0))
    vec = pl.BlockSpec((1, D_MODEL), lambda bb, t: (0, 0))
    return pl.pallas_call(
        _out_kernel,
        grid=(B, T // tm),
        in_specs=[tok, tok, tok, _mod_spec(gt, tm), _mod_spec(sc, tm), _mod_spec(sh, tm),
                  pl.BlockSpec((D_MODEL, D_MODEL), lambda bb, t: (0, 0)), vec, vec],
        out_specs=[tok, tok],
        out_shape=[jax.ShapeDtypeStruct((B, T, D_MODEL), F32),
                   jax.ShapeDtypeStruct((B, T, D_MODEL), BF16)],
        compiler_params=_params(("arbitrary", "arbitrary")),
        name="out_proj",
    )(ret_m, att_m, x, gt, sc, sh, w_o, g.reshape(1, D_MODEL), b.reshape(1, D_MODEL))


def _ffn_kernel(h_ref, x1_ref, gt_ref, wg_ref, wu_ref, wd_ref, g_ref, b_ref, o_ref, acc_ref):
    f = pl.program_id(2)

    @pl.when((pl.program_id(0) == 0) & (pl.program_id(1) == 0) & (f == 0))
    def _():
        acc_ref[...] = jnp.zeros_like(acc_ref)

    h = h_ref[...]
    a = jnp.dot(h, wg_ref[...], preferred_element_type=F32)
    u = jnp.dot(h, wu_ref[...], preferred_element_type=F32)
    act = (a * jax.nn.sigmoid(a) * u).astype(BF16)
    part = jnp.dot(act, wd_ref[...], preferred_element_type=F32)
    acc_ref[...] = part + jnp.where(f == 0, 0.0, acc_ref[...])

    @pl.when(f == pl.num_programs(2) - 1)
    def _():
        z = ALPHA * x1_ref[...] + gt_ref[...] * acc_ref[...]
        o_ref[...] = _layernorm(z, g_ref[...], b_ref[...])


def _ffn(h2, x1, gt, wg, wu, wd, g, b, tm):
    B, T, _ = x1.shape
    tf = wg.shape[2]
    tok = pl.BlockSpec((None, tm, D_MODEL), lambda bb, t, f: (bb, t, 0))
    vec = pl.BlockSpec((1, D_MODEL), lambda bb, t, f: (0, 0))
    return pl.pallas_call(
        _ffn_kernel,
        grid=(B, T // tm, D_FF // tf),
        in_specs=[tok, tok, _mod_spec(gt, tm),
                  pl.BlockSpec((None, D_MODEL, tf), lambda bb, t, f: (f, 0, 0)),
                  pl.BlockSpec((None, D_MODEL, tf), lambda bb, t, f: (f, 0, 0)),
                  pl.BlockSpec((tf, D_MODEL), lambda bb, t, f: (f, 0)), vec, vec],
        out_specs=tok,
        out_shape=jax.ShapeDtypeStruct((B, T, D_MODEL), F32),
        scratch_shapes=[pltpu.VMEM((tm, D_MODEL), F32)],
        compiler_params=_params(("arbitrary", "arbitrary", "arbitrary")),
        name="ffn",
    )(h2, x1, gt, wg, wu, wd, g.reshape(1, D_MODEL), b.reshape(1, D_MODEL))


def _largest_tile(total, cap):
    t = min(total, cap)
    while total % t:
        t //= 2
    return t


def kernel(x_prompt, x_sample, c_prompt, c_sample, cache_attn_k, cache_attn_v, state_ret, w_ada, b_ada, w_in,
           gn_g, attn_sinks, w_o, ln1_g, ln1_b, w_ffn_gate, w_ffn_up, w_ffn_down, ln2_g, ln2_b):
    B, T, _ = x_prompt.shape
    Bs, Ls, _ = x_sample.shape
    l = 0

    n_c = B + Bs
    pad = (-n_c) % 8
    c_all = jnp.concatenate([c_prompt, c_sample, jnp.zeros((pad, D_MODEL), F32)], axis=0)
    mods = _mods(c_all, w_ada[l], b_ada[l])
    mods_p = [m[:, None, :] for m in jnp.split(mods[:B], 6, axis=-1)]
    mods_s = [jnp.repeat(m, Ls, axis=0)[None] for m in jnp.split(mods[B:n_c], 6, axis=-1)]

    w = w_in[l]
    o_ka = 5 * D_MODEL
    o_gate = o_ka + KV_W
    w_main = _repack_w_in(w)
    w_kv = w[:, o_ka:o_gate].astype(BF16)
    wo, wd = w_o[l].astype(BF16), w_ffn_down[l].astype(BF16)

    def col_tiles(a):
        return a.reshape(D_MODEL, D_FF // FFN_TF, FFN_TF).transpose(1, 0, 2).astype(BF16)

    wg, wu = col_tiles(w_ffn_gate[l]), col_tiles(w_ffn_up[l])

    def layer(x, mods6, pos, s0, att_fn, tm_proj, tm_out, tm_ffn, ret_chunk, ret_tt):
        sh_a, sc_a, gt_a, sh_f, sc_f, gt_f = mods6
        tab = _rope_tables(pos)
        proj, kvb = _proj(x, sc_a, sh_a, tab, w_main, w_kv, tm_proj)
        ret_m, s_new = _retention(proj, gn_g[l], s0, ret_chunk, ret_tt, 1)
        att_m = att_fn(proj, kvb)
        x1, h2 = _out_proj(ret_m, att_m, x, gt_a, sc_f, sh_f, wo, ln1_g[l], ln1_b[l], tm_out)
        y = _ffn(h2, x1, gt_f, wg, wu, wd, ln2_g[l], ln2_b[l], tm_ffn)
        return y, s_new, tab

    pos_p = jnp.arange(T)
    n_chunks_p = _largest_tile(T // CHUNK, 8)
    y_p, s_p, tab_p = layer(
        x_prompt, mods_p, pos_p, jnp.zeros((B, RET_HEADS, RET_DIM, RET_DIM), F32),
        lambda proj, kvb: _attention(proj, kvb, kvb, attn_sinks[l], CHUNK, n_chunks_p, CHUNK, True),
        _largest_tile(T, 1024), _largest_tile(T, 512), _largest_tile(T, 512),
        _largest_tile(T, 256), _largest_tile(T, 2048))
    kv_p = _kv32(x_prompt[:, T - WINDOW:], mods_p[1], mods_p[0], tab_p[:, T - WINDOW:], w_kv)
    k_p = kv_p[..., :KV_W // 2].reshape(B, WINDOW, ATT_KV_HEADS, ATT_HEAD_DIM)
    v_p = kv_p[..., KV_W // 2:].reshape(B, WINDOW, ATT_KV_HEADS, ATT_HEAD_DIM)

    R = Bs * Ls
    xs = x_sample.reshape(1, R, D_MODEL)
    pos_s = jnp.tile(PAST_LEN + jnp.arange(Ls), Bs)
    cache = jnp.concatenate([cache_attn_k[l].reshape(Bs, WINDOW, KV_W // 2),
                             cache_attn_v[l].reshape(Bs, WINDOW, KV_W // 2)], axis=-1).astype(BF16)

    def att_sample(proj, kvb):
        new = jnp.pad(kvb.reshape(Bs, Ls, KV_W), ((0, 0), (0, CHUNK - Ls), (0, 0)))
        o = _attention(proj.reshape(Bs, Ls, MAIN_W), cache, new, attn_sinks[l], Ls, 1, Ls, False)
        return o.reshape(1, R, D_MODEL)

    def ret_sample(proj, s0):
        return _retention(proj.reshape(Bs, Ls, MAIN_W), gn_g[l], s0, Ls, Ls, RET_HEADS)

    sh_a, sc_a, gt_a, sh_f, sc_f, gt_f = mods_s
    tab_s = _rope_tables(pos_s)
    proj_s, kvb_s = _proj(xs, sc_a, sh_a, tab_s, w_main, w_kv, R)
    ret_s, s_s = ret_sample(proj_s, state_ret[l])
    att_s = att_sample(proj_s, kvb_s)
    x1_s, h2_s = _out_proj(ret_s.reshape(1, R, D_MODEL), att_s, xs, gt_a, sc_f, sh_f, wo, ln1_g[l], ln1_b[l], R)
    y_s = _ffn(h2_s, x1_s, gt_f, wg, wu, wd, ln2_g[l], ln2_b[l], R).reshape(Bs, Ls, D_MODEL)
    kv_s = _kv32(xs, sc_a, sh_a, tab_s, w_kv).reshape(Bs, Ls, KV_W)
    k_s = kv_s[..., :KV_W // 2].reshape(Bs, Ls, ATT_KV_HEADS, ATT_HEAD_DIM)
    v_s = kv_s[..., KV_W // 2:].reshape(Bs, Ls, ATT_KV_HEADS, ATT_HEAD_DIM)

    return (y_p, y_s, k_p[None], v_p[None], s_p[None], k_s[None], v_s[None], s_s[None])
```

```python
import functools

import jax
import jax.numpy as jnp
from jax import lax
from jax.experimental import pallas as pl
from jax.experimental.pallas import tpu as pltpu

F32 = jnp.float32
BF16 = jnp.bfloat16

D_MODEL = 2048
CHUNK = 64
PAST_LEN = 1024
RET_HEADS = 8
RET_DIM = 256
RET_ROPE_THETA = 10000.0
ATT_HEADS = 32
ATT_KV_HEADS = 4
ATT_HEAD_DIM = 64
ATT_GROUP = ATT_HEADS // ATT_KV_HEADS
WINDOW = 128
ROPE_DIM = ATT_HEAD_DIM // 4
ROPE_THETA = 500000.0
D_FF = 5632
DEPTH = 1
ALPHA = (2.0 * DEPTH) ** 0.25
LN_EPS = 1e-5
NEG_INF = -1e30

LANES = 128
KV_W = 2 * ATT_KV_HEADS * ATT_HEAD_DIM
MAIN_W = 7 * D_MODEL
PROJ_TN = 1024
PROJ_SPLIT_ROWS = 256
FFN_SPLIT_ROWS = 256
OUT_SPLIT_ROWS = 256
VMEM_LIMIT = 60 * 1024 * 1024

COL_QR, COL_KR, COL_VR, COL_GR, COL_QA, COL_SGR, COL_SGA = (i * D_MODEL for i in range(7))


def _params(sem):
    return pltpu.CompilerParams(dimension_semantics=sem, vmem_limit_bytes=VMEM_LIMIT)


def _mods_kernel(c_ref, w_ref, b_ref, o_ref):
    c = c_ref[...]
    a = (c * jax.nn.sigmoid(c)).astype(BF16)
    o_ref[...] = jnp.dot(a, w_ref[...].astype(BF16), preferred_element_type=F32) + b_ref[...]


def _mods(c_all, w_ada, b_ada):
    rows = c_all.shape[0]
    n_out = w_ada.shape[1]
    tn = 1024
    return pl.pallas_call(
        _mods_kernel,
        grid=(n_out // tn,),
        in_specs=[pl.BlockSpec((rows, D_MODEL), lambda n: (0, 0)),
                  pl.BlockSpec((D_MODEL, tn), lambda n: (0, n)),
                  pl.BlockSpec((1, tn), lambda n: (0, n))],
        out_specs=pl.BlockSpec((rows, tn), lambda n: (0, n)),
        out_shape=jax.ShapeDtypeStruct((rows, n_out), F32),
        compiler_params=_params(("arbitrary",)),
        name="mods",
    )(c_all, w_ada, b_ada.reshape(1, n_out))


def _rope_tables(pos):
    posf = pos.astype(F32)[:, None]
    half_r = RET_DIM // 2
    inv_r = 1.0 / (RET_ROPE_THETA ** (jnp.arange(half_r, dtype=F32) / half_r))
    ang_r = posf * inv_r[None, :]
    half_a = ROPE_DIM // 2
    inv_a = 1.0 / (ROPE_THETA ** (jnp.arange(half_a, dtype=F32) / half_a))
    d = jnp.arange(LANES) % ATT_HEAD_DIM
    ang_a = posf * inv_a[None, :]
    cos_a = jnp.tile(jnp.cos(ang_a), (1, LANES // half_a))
    sin_a = jnp.tile(jnp.sin(ang_a), (1, LANES // half_a))
    ca = jnp.where(d[None, :] < ROPE_DIM, cos_a, 1.0)
    s1 = jnp.where(d[None, :] < half_a, -sin_a, 0.0)
    s2 = jnp.where((d[None, :] >= half_a) & (d[None, :] < ROPE_DIM), sin_a, 0.0)
    return jnp.stack([jnp.cos(ang_r), jnp.sin(ang_r), ca, s1, s2])


def _rot_att(x, ca, s1, s2):
    half = ROPE_DIM // 2
    return x * ca + pltpu.roll(x, LANES - half, 1) * s1 + pltpu.roll(x, half, 1) * s2


def _proj_kernel(x_ref, sc_ref, sh_ref, tab_ref, w_ref, wkv_ref, o_ref, kv_ref, h_ref):
    n = pl.program_id(2)
    tm = x_ref.shape[0]
    n_split = max(tm // PROJ_SPLIT_ROWS, 1)
    row_blocks = [slice(r * (tm // n_split), (r + 1) * (tm // n_split)) for r in range(n_split)]

    def mod_rows(ref, rows):
        return ref[...] if ref.shape[0] == 1 else ref[rows, :]

    def att_tabs(rows):
        return tab_ref[2, rows, :], tab_ref[3, rows, :], tab_ref[4, rows, :]

    @pl.when(n == 0)
    def _():
        for rows in row_blocks:
            h = x_ref[rows, :] * (1.0 + mod_rows(sc_ref, rows)) + mod_rows(sh_ref, rows)
            h_ref[rows, :] = h.astype(BF16)
            kv = jnp.dot(h_ref[rows, :], wkv_ref[...], preferred_element_type=F32)
            ca, s1, s2 = att_tabs(rows)
            for s in range(KV_W // LANES):
                xs = kv[:, s * LANES:(s + 1) * LANES]
                if s < KV_W // LANES // 2:
                    xs = _rot_att(xs, ca, s1, s2)
                kv_ref[rows, s * LANES:(s + 1) * LANES] = xs.astype(BF16)

    def matmul(rows):
        return jnp.dot(h_ref[rows, :], w_ref[...], preferred_element_type=F32)

    def rot_ret(scale):
        for rows in row_blocks:
            acc = matmul(rows)
            cr, sr = tab_ref[0, rows, :], tab_ref[1, rows, :]
            for j in range(PROJ_TN // RET_DIM):
                a = j * RET_DIM
                x1 = acc[:, a:a + LANES]
                x2 = acc[:, a + LANES:a + 2 * LANES]
                o_ref[rows, a:a + LANES] = ((x1 * cr - x2 * sr) * scale).astype(BF16)
                o_ref[rows, a + LANES:a + 2 * LANES] = ((x1 * sr + x2 * cr) * scale).astype(BF16)

    region = n // (D_MODEL // PROJ_TN)

    @pl.when(region == 0)
    def _():
        rot_ret(1.0)

    @pl.when(region == 1)
    def _():
        rot_ret(RET_DIM ** -0.5)

    @pl.when(region == 2)
    def _():
        for rows in row_blocks:
            o_ref[rows, :] = matmul(rows).astype(BF16)

    @pl.when(region == 3)
    def _():
        for rows in row_blocks:
            acc = matmul(rows)
            o_ref[rows, :] = (acc * jax.nn.sigmoid(acc)).astype(BF16)

    @pl.when(region == 4)
    def _():
        for rows in row_blocks:
            acc = matmul(rows)
            ca, s1, s2 = att_tabs(rows)
            for s in range(PROJ_TN // LANES):
                xs = _rot_att(acc[:, s * LANES:(s + 1) * LANES], ca, s1, s2)
                o_ref[rows, s * LANES:(s + 1) * LANES] = (xs * ATT_HEAD_DIM ** -0.5).astype(BF16)

    @pl.when(region >= 5)
    def _():
        for rows in row_blocks:
            o_ref[rows, :] = jax.nn.sigmoid(matmul(rows)).astype(BF16)


def _mod_spec(mod, tm):
    if mod.shape[1] == 1:
        return pl.BlockSpec((None, 1, D_MODEL), lambda b, t, *_: (b, 0, 0))
    return pl.BlockSpec((None, tm, D_MODEL), lambda b, t, *_: (b, t, 0))


def _proj(x, sc, sh, tab, w_main, w_kv, tm):
    B, T, _ = x.shape
    n_tiles = MAIN_W // PROJ_TN
    return pl.pallas_call(
        _proj_kernel,
        grid=(B, T // tm, n_tiles),
        in_specs=[pl.BlockSpec((None, tm, D_MODEL), lambda b, t, n: (b, t, 0)),
                  _mod_spec(sc, tm), _mod_spec(sh, tm),
                  pl.BlockSpec((5, tm, LANES), lambda b, t, n: (0, t, 0)),
                  pl.BlockSpec((D_MODEL, PROJ_TN), lambda b, t, n: (0, n)),
                  pl.BlockSpec((D_MODEL, KV_W), lambda b, t, n: (0, 0), pipeline_mode=pl.Buffered(1))],
        out_specs=[pl.BlockSpec((None, tm, PROJ_TN), lambda b, t, n: (b, t, n)),
                   pl.BlockSpec((None, tm, KV_W), lambda b, t, n: (b, t, 0))],
        out_shape=[jax.ShapeDtypeStruct((B, T, MAIN_W), BF16),
                   jax.ShapeDtypeStruct((B, T, KV_W), BF16)],
        scratch_shapes=[pltpu.VMEM((tm, D_MODEL), BF16)],
        compiler_params=_params(("arbitrary", "arbitrary", "arbitrary")),
        name="proj",
    )(x, sc, sh, tab, w_main, w_kv)


def _repack_kernel(w_ref, o_ref):
    o_ref[...] = w_ref[...].astype(BF16)


def _repack_w_in(w):
    kv_blk = 5 * D_MODEL // KV_W
    return pl.pallas_call(
        _repack_kernel,
        grid=(MAIN_W // KV_W,),
        in_specs=[pl.BlockSpec((D_MODEL, KV_W), lambda n: (0, jnp.where(n >= kv_blk, n + 1, n)))],
        out_specs=pl.BlockSpec((D_MODEL, KV_W), lambda n: (0, n)),
        out_shape=jax.ShapeDtypeStruct((D_MODEL, MAIN_W), BF16),
        compiler_params=_params(("arbitrary",)),
        name="repack_w_in",
    )(w)


def _kv32_kernel(x_ref, sc_ref, sh_ref, tab_ref, wkv_ref, o_ref):
    h = (x_ref[...] * (1.0 + sc_ref[...]) + sh_ref[...]).astype(BF16)
    kv = jnp.dot(h, wkv_ref[...], preferred_element_type=F32)
    ca, s1, s2 = tab_ref[2], tab_ref[3], tab_ref[4]
    for s in range(KV_W // LANES):
        xs = kv[:, s * LANES:(s + 1) * LANES]
        if s < KV_W // LANES // 2:
            xs = _rot_att(xs, ca, s1, s2)
        o_ref[:, s * LANES:(s + 1) * LANES] = xs


def _kv32(x, sc, sh, tab, w_kv):
    B, R, _ = x.shape
    return pl.pallas_call(
        _kv32_kernel,
        grid=(B, 1),
        in_specs=[pl.BlockSpec((None, R, D_MODEL), lambda b, t: (b, 0, 0)),
                  _mod_spec(sc, R), _mod_spec(sh, R),
                  pl.BlockSpec((5, R, LANES), lambda b, t: (0, 0, 0)),
                  pl.BlockSpec((D_MODEL, KV_W), lambda b, t: (0, 0))],
        out_specs=pl.BlockSpec((None, R, KV_W), lambda b, t: (b, 0, 0)),
        out_shape=jax.ShapeDtypeStruct((B, R, KV_W), F32),
        compiler_params=_params(("arbitrary", "arbitrary")),
        name="kv32",
    )(x, sc, sh, tab, w_kv)


def _ret_kernel(lg_ref, q_ref, k_ref, v_ref, gs_ref, sg_ref, gn_ref, s0_ref, o_ref, sout_ref,
                s_ref, dm_ref, qd_ref, kd_ref, *, chunk, n_chunks, heads):
    hg = pl.program_id(1)
    t = pl.program_id(2)
    nt = (((1,), (1,)), ((), ()))
    tn = (((0,), (0,)), ((), ()))
    for hl in range(heads):
        lg = lg_ref[hg * heads + hl]
        cols = slice(hl * RET_DIM, (hl + 1) * RET_DIM)

        @pl.when(t == 0)
        def _():
            s_ref[hl] = s0_ref[hl]
            i = lax.broadcasted_iota(jnp.int32, (chunk, chunk), 0)
            j = lax.broadcasted_iota(jnp.int32, (chunk, chunk), 1)
            diff = (i - j).astype(F32)
            dm_ref[hl] = jnp.where(diff >= 0, jnp.exp(lg * jnp.maximum(diff, 0.0)), 0.0)
            r = lax.broadcasted_iota(jnp.int32, (chunk, RET_DIM), 0).astype(F32)
            qd_ref[hl] = jnp.exp(lg * (r + 1.0))
            kd_ref[hl] = jnp.exp(lg * (chunk - 1.0 - r))

        g_chunk = jnp.exp(jnp.full((1, RET_DIM), lg * chunk, F32))
        for c in range(n_chunks):
            rows = pl.ds(c * chunk, chunk)
            q, k, v = q_ref[rows, cols], k_ref[rows, cols], v_ref[rows, cols]
            inner = lax.dot_general(q, k, nt, preferred_element_type=F32) * dm_ref[hl]
            o = jnp.dot(inner.astype(BF16), v, preferred_element_type=F32)
            s_prev = s_ref[hl]
            o = o + jnp.dot(q, s_prev.astype(BF16), preferred_element_type=F32) * qd_ref[hl]
            kdec = (k.astype(F32) * kd_ref[hl]).astype(BF16)
            s_ref[hl] = s_prev * g_chunk + lax.dot_general(kdec, v, tn, preferred_element_type=F32)
            mu = jnp.mean(o, axis=-1, keepdims=True)
            xc = o - mu
            var = jnp.mean(xc * xc, axis=-1, keepdims=True)
            gn = xc * lax.rsqrt(var + LN_EPS) * gn_ref[:, cols]
            out = sg_ref[rows, cols].astype(F32) * (gs_ref[rows, cols].astype(F32) * gn)
            o_ref[rows, cols] = out.astype(BF16)

    @pl.when(t == pl.num_programs(2) - 1)
    def _():
        sout_ref[...] = s_ref[...]


def _retention(proj, gn_g, s0, chunk, tt, heads):
    B, T, _ = proj.shape
    n_chunks = tt // chunk
    width = heads * RET_DIM
    log_gamma = jnp.log1p(-jnp.exp2(-5.0 - jnp.arange(RET_HEADS, dtype=F32)))

    def col(off):
        base = off // width
        return pl.BlockSpec((None, tt, width), lambda b, h, t: (b, t, base + h))

    state = pl.BlockSpec((None, heads, RET_DIM, RET_DIM), lambda b, h, t: (b, h, 0, 0))
    return pl.pallas_call(
        functools.partial(_ret_kernel, chunk=chunk, n_chunks=n_chunks, heads=heads),
        grid=(B, RET_HEADS // heads, T // tt),
        in_specs=[pl.BlockSpec(memory_space=pltpu.SMEM),
                  col(COL_QR), col(COL_KR), col(COL_VR), col(COL_GR), col(COL_SGR),
                  pl.BlockSpec((1, width), lambda b, h, t: (0, h)), state],
        out_specs=[pl.BlockSpec((None, tt, width), lambda b, h, t: (b, t, h)), state],
        out_shape=[jax.ShapeDtypeStruct((B, T, D_MODEL), BF16),
                   jax.ShapeDtypeStruct((B, RET_HEADS, RET_DIM, RET_DIM), F32)],
        scratch_shapes=[pltpu.VMEM((heads, RET_DIM, RET_DIM), F32),
                        pltpu.VMEM((heads, chunk, chunk), F32),
                        pltpu.VMEM((heads, chunk, RET_DIM), F32),
                        pltpu.VMEM((heads, chunk, RET_DIM), F32)],
        compiler_params=_params(("arbitrary", "arbitrary", "arbitrary")),
        name="retention",
    )(log_gamma, proj, proj, proj, proj, proj, gn_g.reshape(1, D_MODEL), s0)


def _att_kernel(sink_ref, q_ref, halo_ref, cur_ref, sg_ref, o_ref, *, cq, n_chunks, own_valid, mask_first):
    i = pl.program_id(1)
    kv = jnp.concatenate([halo_ref[...], cur_ref[...]], axis=0).astype(F32)
    n_keys = kv.shape[0]
    n_kc = n_keys // CHUNK
    n_tiles = (n_kc + 1) // 2
    lane = lax.broadcasted_iota(jnp.int32, (1, LANES), 1)
    lo = lane < ATT_HEAD_DIM
    ones_lo = jnp.broadcast_to(jnp.where(lo, 1.0, 0.0), (n_keys, LANES)).astype(BF16)
    ones_hi = jnp.broadcast_to(jnp.where(lo, 0.0, 1.0), (n_keys, LANES)).astype(BF16)
    own_ok = (lane % ATT_HEAD_DIM) < own_valid
    nt = (((1,), (1,)), ((), ()))
    pairs = ATT_GROUP // 2
    rows_q = pairs * cq
    k_off = ATT_KV_HEADS * ATT_HEAD_DIM
    users = [[c for c in range(n_chunks) if c <= 2 * u + 1 and c + 2 >= 2 * u] for u in range(n_tiles)]

    def tile_rows(a_lo, a_hi, u):
        parts = []
        for j in (2 * u, 2 * u + 1):
            for a in (a_lo, a_hi):
                parts.append(a[j * CHUNK:(j + 1) * CHUNK] if j < n_kc else jnp.zeros((CHUNK, a.shape[1]), a.dtype))
        return jnp.concatenate(parts, axis=0)

    for y in range(ATT_KV_HEADS // 2):
        ks = kv[:, y * LANES:(y + 1) * LANES]
        kr = pltpu.roll(ks, ATT_HEAD_DIM, 1)
        vs = kv[:, k_off + y * LANES:k_off + (y + 1) * LANES]
        vr = pltpu.roll(vs, ATT_HEAD_DIM, 1)
        for par in range(2):
            x = 2 * y + par
            k_even, k_odd = (ks, kr) if par == 0 else (kr, ks)
            v_even, v_odd = (vs, vr) if par == 0 else (vr, vs)
            klo = jnp.where(lo, k_even, 0.0).astype(BF16)
            khi = jnp.where(lo, 0.0, k_odd).astype(BF16)
            vlo = jnp.concatenate([jnp.where(lo, v_even, 0.0).astype(BF16), ones_lo], axis=1)
            vhi = jnp.concatenate([jnp.where(lo, 0.0, v_odd).astype(BF16), ones_hi], axis=1)
            sink_b = jnp.concatenate(
                [jnp.broadcast_to(jnp.where(lo, sink_ref[x * ATT_GROUP + 2 * p],
                                            sink_ref[x * ATT_GROUP + 2 * p + 1]), (cq, LANES))
                 for p in range(pairs)], axis=0)
            col0 = x * ATT_GROUP * ATT_HEAD_DIM
            qp = [jnp.concatenate([q_ref[c * cq:(c + 1) * cq, col0 + p * LANES:col0 + (p + 1) * LANES]
                                   for p in range(pairs)], axis=0) for c in range(n_chunks)]
            s_tiles = [lax.dot_general(jnp.concatenate([qp[c] for c in users[u]], axis=0), tile_rows(klo, khi, u),
                                       nt, preferred_element_type=F32) for u in range(n_tiles)]
            p_blocks, m_all = {}, {}
            for c in range(n_chunks):
                sj = []
                for jj in range(3):
                    u, half = (c + jj) // 2, (c + jj) % 2
                    r = users[u].index(c) * rows_q
                    sj.append(s_tiles[u][r:r + rows_q, half * LANES:(half + 1) * LANES])
                if mask_first:
                    for jj in range(2):
                        if c + jj < 2:
                            sj[jj] = jnp.where(i * n_chunks + (c + jj - 2) < 0, NEG_INF, sj[jj])
                if own_valid < CHUNK:
                    sj[2] = jnp.where(own_ok, sj[2], NEG_INF)
                m3 = jnp.maximum(jnp.maximum(sj[0], sj[1]), sj[2])
                m_even = jnp.max(jnp.where(lo, m3, NEG_INF), axis=1, keepdims=True)
                m_odd = jnp.max(jnp.where(lo, NEG_INF, m3), axis=1, keepdims=True)
                m_b = jnp.maximum(jnp.where(lo, m_even, m_odd), sink_b)
                m_all[c] = m_b
                for jj in range(3):
                    p_blocks[(c, c + jj)] = jnp.exp(sj[jj] - m_b).astype(BF16)
            zero_p = jnp.zeros((rows_q, LANES), BF16)
            o_tiles = []
            for u in range(n_tiles):
                p_u = jnp.concatenate(
                    [jnp.concatenate([p_blocks.get((c, j), zero_p) for j in (2 * u, 2 * u + 1)], axis=1)
                     for c in users[u]], axis=0)
                o_tiles.append(jnp.dot(p_u, tile_rows(vlo, vhi, u), preferred_element_type=F32))
            for c in range(n_chunks):
                oe = None
                for u in sorted({c // 2, (c + 2) // 2}):
                    r = users[u].index(c) * rows_q
                    part = o_tiles[u][r:r + rows_q]
                    oe = part if oe is None else oe + part
                den = oe[:, LANES:] + jnp.exp(sink_b - m_all[c])
                res = oe[:, :LANES] / den
                for p in range(pairs):
                    cols = slice(col0 + p * LANES, col0 + (p + 1) * LANES)
                    rows = slice(c * cq, (c + 1) * cq)
                    gate = sg_ref[rows, cols].astype(F32)
                    o_ref[rows, cols] = (gate * res[p * cq:(p + 1) * cq]).astype(BF16)


def _attention(proj, halo_src, cur_src, sinks, cq, n_chunks, own_valid, mask_first):
    B, T, _ = proj.shape
    tq = cq * n_chunks
    cur_rows = CHUNK * n_chunks
    q_blk, sg_blk = COL_QA // D_MODEL, COL_SGA // D_MODEL
    halo_per_cur = cur_rows // WINDOW if mask_first else 0

    def halo_map(b, i):
        return (b, jnp.maximum(i * halo_per_cur - 1, 0), 0) if mask_first else (b, 0, 0)

    return pl.pallas_call(
        functools.partial(_att_kernel, cq=cq, n_chunks=n_chunks, own_valid=own_valid, mask_first=mask_first),
        grid=(B, T // tq),
        in_specs=[pl.BlockSpec(memory_space=pltpu.SMEM),
                  pl.BlockSpec((None, tq, D_MODEL), lambda b, i: (b, i, q_blk)),
                  pl.BlockSpec((None, WINDOW, KV_W), halo_map),
                  pl.BlockSpec((None, cur_rows, KV_W), lambda b, i: (b, i, 0)),
                  pl.BlockSpec((None, tq, D_MODEL), lambda b, i: (b, i, sg_blk))],
        out_specs=pl.BlockSpec((None, tq, D_MODEL), lambda b, i: (b, i, 0)),
        out_shape=jax.ShapeDtypeStruct((B, T, D_MODEL), BF16),
        compiler_params=_params(("arbitrary", "arbitrary")),
        name="attention",
    )(sinks, proj, halo_src, cur_src, proj)


def _layernorm(z, g, b):
    mu = jnp.mean(z, axis=-1, keepdims=True)
    zc = z - mu
    var = jnp.mean(zc * zc, axis=-1, keepdims=True)
    return zc * lax.rsqrt(var + LN_EPS) * g + b


def _out_kernel(r_ref, a_ref, x_ref, gt_ref, sc_ref, sh_ref, w_ref, g_ref, b_ref, x1_ref, h2_ref):
    n_split = max(x_ref.shape[0] // OUT_SPLIT_ROWS, 1)
    rows_per = x_ref.shape[0] // n_split

    def mod_rows(ref, rows):
        return ref[...] if ref.shape[0] == 1 else ref[rows, :]

    for r in range(n_split):
        rows = slice(r * rows_per, (r + 1) * rows_per)
        merged = r_ref[rows, :] + a_ref[rows, :]
        mix = jnp.dot(merged, w_ref[...], preferred_element_type=F32)
        z = ALPHA * x_ref[rows, :] + mod_rows(gt_ref, rows) * mix
        x1 = _layernorm(z, g_ref[...], b_ref[...])
        x1_ref[rows, :] = x1
        h2_ref[rows, :] = (x1 * (1.0 + mod_rows(sc_ref, rows)) + mod_rows(sh_ref, rows)).astype(BF16)


def _out_proj(ret_m, att_m, x, gt, sc, sh, w_o, g, b, tm):
    B, T, _ = x.shape
    tok = pl.BlockSpec((None, tm, D_MODEL), lambda bb, t: (bb, t, 0))
    vec = pl.BlockSpec((1, D_MODEL), lambda bb, t: (0, 0))
    return pl.pallas_call(
        _out_kernel,
        grid=(B, T // tm),
        in_specs=[tok, tok, tok, _mod_spec(gt, tm), _mod_spec(sc, tm), _mod_spec(sh, tm),
                  pl.BlockSpec((D_MODEL, D_MODEL), lambda bb, t: (0, 0)), vec, vec],
        out_specs=[tok, tok],
        out_shape=[jax.ShapeDtypeStruct((B, T, D_MODEL), F32),
                   jax.ShapeDtypeStruct((B, T, D_MODEL), BF16)],
        compiler_params=_params(("arbitrary", "arbitrary")),
        name="out_proj",
    )(ret_m, att_m, x, gt, sc, sh, w_o, g.reshape(1, D_MODEL), b.reshape(1, D_MODEL))


def _ffn_kernel(h_ref, x1_ref, gt_ref, wg_ref, wu_ref, wd_ref, g_ref, b_ref, o_ref):
    f = pl.program_id(2)
    tm = h_ref.shape[0]
    n_split = max(tm // FFN_SPLIT_ROWS, 1)
    row_blocks = [slice(r * (tm // n_split), (r + 1) * (tm // n_split)) for r in range(n_split)]

    @pl.when(f == 0)
    def _():
        o_ref[...] = jnp.zeros_like(o_ref)

    for rows in row_blocks:
        h = h_ref[rows, :]
        a = jnp.dot(h, wg_ref[...], preferred_element_type=F32)
        u = jnp.dot(h, wu_ref[...], preferred_element_type=F32)
        act = (a * jax.nn.sigmoid(a) * u).astype(BF16)
        o_ref[rows, :] += jnp.dot(act, wd_ref[...], preferred_element_type=F32)

    @pl.when(f == pl.num_programs(2) - 1)
    def _():
        for rows in row_blocks:
            gt = gt_ref[...] if gt_ref.shape[0] == 1 else gt_ref[rows, :]
            z = ALPHA * x1_ref[rows, :] + gt * o_ref[rows, :]
            o_ref[rows, :] = _layernorm(z, g_ref[...], b_ref[...])


def _ffn(h2, x1, gt, wg, wu, wd, g, b, tm, tf):
    B, T, _ = x1.shape
    tok = pl.BlockSpec((None, tm, D_MODEL), lambda bb, t, f: (bb, t, 0))
    vec = pl.BlockSpec((1, D_MODEL), lambda bb, t, f: (0, 0))
    return pl.pallas_call(
        _ffn_kernel,
        grid=(B, T // tm, D_FF // tf),
        in_specs=[tok, tok, _mod_spec(gt, tm),
                  pl.BlockSpec((D_MODEL, tf), lambda bb, t, f: (0, f)),
                  pl.BlockSpec((D_MODEL, tf), lambda bb, t, f: (0, f)),
                  pl.BlockSpec((tf, D_MODEL), lambda bb, t, f: (f, 0)), vec, vec],
        out_specs=tok,
        out_shape=jax.ShapeDtypeStruct((B, T, D_MODEL), F32),
        compiler_params=_params(("arbitrary", "arbitrary", "arbitrary")),
        name="ffn",
    )(h2, x1, gt, wg, wu, wd, g.reshape(1, D_MODEL), b.reshape(1, D_MODEL))


def _largest_tile(total, cap):
    t = min(total, cap)
    while total % t:
        t //= 2
    return t


def kernel(x_prompt, x_sample, c_prompt, c_sample, cache_attn_k, cache_attn_v, state_ret, w_ada, b_ada, w_in,
           gn_g, attn_sinks, w_o, ln1_g, ln1_b, w_ffn_gate, w_ffn_up, w_ffn_down, ln2_g, ln2_b):
    B, T, _ = x_prompt.shape
    Bs, Ls, _ = x_sample.shape
    l = 0

    n_c = B + Bs
    pad = (-n_c) % 8
    c_all = jnp.concatenate([c_prompt, c_sample, jnp.zeros((pad, D_MODEL), F32)], axis=0)
    mods = _mods(c_all, w_ada[l], b_ada[l])
    mods_p = [m[:, None, :] for m in jnp.split(mods[:B], 6, axis=-1)]
    mods_s = [jnp.repeat(m, Ls, axis=0)[None] for m in jnp.split(mods[B:n_c], 6, axis=-1)]

    w = w_in[l]
    o_ka = 5 * D_MODEL
    o_gate = o_ka + KV_W
    w_main = _repack_w_in(w)
    w_kv = w[:, o_ka:o_gate].astype(BF16)
    wo, wg, wu, wd = (a[l].astype(BF16) for a in (w_o, w_ffn_gate, w_ffn_up, w_ffn_down))

    def layer(x, mods6, pos, s0, att_fn, tm_proj, tm_out, tm_ffn, ret_chunk, ret_tt):
        sh_a, sc_a, gt_a, sh_f, sc_f, gt_f = mods6
        tab = _rope_tables(pos)
        proj, kvb = _proj(x, sc_a, sh_a, tab, w_main, w_kv, tm_proj)
        ret_m, s_new = _retention(proj, gn_g[l], s0, ret_chunk, ret_tt, 1)
        att_m = att_fn(proj, kvb)
        x1, h2 = _out_proj(ret_m, att_m, x, gt_a, sc_f, sh_f, wo, ln1_g[l], ln1_b[l], tm_out)
        y = _ffn(h2, x1, gt_f, wg, wu, wd, ln2_g[l], ln2_b[l], tm_ffn, 512)
        return y, s_new, tab

    pos_p = jnp.arange(T)
    n_chunks_p = _largest_tile(T // CHUNK, 8)
    y_p, s_p, tab_p = layer(
        x_prompt, mods_p, pos_p, jnp.zeros((B, RET_HEADS, RET_DIM, RET_DIM), F32),
        lambda proj, kvb: _attention(proj, kvb, kvb, attn_sinks[l], CHUNK, n_chunks_p, CHUNK, True),
        _largest_tile(T, 1024), _largest_tile(T, 512), _largest_tile(T, 1024),
        _largest_tile(T, 256), _largest_tile(T, 2048))
    kv_p = _kv32(x_prompt[:, T - WINDOW:], mods_p[1], mods_p[0], tab_p[:, T - WINDOW:], w_kv)
    k_p = kv_p[..., :KV_W // 2].reshape(B, WINDOW, ATT_KV_HEADS, ATT_HEAD_DIM)
    v_p = kv_p[..., KV_W // 2:].reshape(B, WINDOW, ATT_KV_HEADS, ATT_HEAD_DIM)

    R = Bs * Ls
    xs = x_sample.reshape(1, R, D_MODEL)
    pos_s = jnp.tile(PAST_LEN + jnp.arange(Ls), Bs)
    cache = jnp.concatenate([cache_attn_k[l].reshape(Bs, WINDOW, KV_W // 2),
                             cache_attn_v[l].reshape(Bs, WINDOW, KV_W // 2)], axis=-1).astype(BF16)

    def att_sample(proj, kvb):
        new = jnp.pad(kvb.reshape(Bs, Ls, KV_W), ((0, 0), (0, CHUNK - Ls), (0, 0)))
        o = _attention(proj.reshape(Bs, Ls, MAIN_W), cache, new, attn_sinks[l], Ls, 1, Ls, False)
        return o.reshape(1, R, D_MODEL)

    def ret_sample(proj, s0):
        return _retention(proj.reshape(Bs, Ls, MAIN_W), gn_g[l], s0, Ls, Ls, RET_HEADS)

    sh_a, sc_a, gt_a, sh_f, sc_f, gt_f = mods_s
    tab_s = _rope_tables(pos_s)
    proj_s, kvb_s = _proj(xs, sc_a, sh_a, tab_s, w_main, w_kv, R)
    ret_s, s_s = ret_sample(proj_s, state_ret[l])
    att_s = att_sample(proj_s, kvb_s)
    x1_s, h2_s = _out_proj(ret_s.reshape(1, R, D_MODEL), att_s, xs, gt_a, sc_f, sh_f, wo, ln1_g[l], ln1_b[l], R)
    y_s = _ffn(h2_s, x1_s, gt_f, wg, wu, wd, ln2_g[l], ln2_b[l], R, 512).reshape(Bs, Ls, D_MODEL)
    kv_s = _kv32(xs, sc_a, sh_a, tab_s, w_kv).reshape(Bs, Ls, KV_W)
    k_s = kv_s[..., :KV_W // 2].reshape(Bs, Ls, ATT_KV_HEADS, ATT_HEAD_DIM)
    v_s = kv_s[..., KV_W // 2:].reshape(Bs, Ls, ATT_KV_HEADS, ATT_HEAD_DIM)

    return (y_p, y_s, k_p[None], v_p[None], s_p[None], k_s[None], v_s[None], s_s[None])
```

```python
import functools

import jax
import jax.numpy as jnp
from jax import lax
from jax.experimental import pallas as pl
from jax.experimental.pallas import tpu as pltpu

F32 = jnp.float32
BF16 = jnp.bfloat16

D_MODEL = 2048
CHUNK = 64
PAST_LEN = 1024
RET_HEADS = 8
RET_DIM = 256
RET_ROPE_THETA = 10000.0
ATT_HEADS = 32
ATT_KV_HEADS = 4
ATT_HEAD_DIM = 64
ATT_GROUP = ATT_HEADS // ATT_KV_HEADS
WINDOW = 128
ROPE_DIM = ATT_HEAD_DIM // 4
ROPE_THETA = 500000.0
D_FF = 5632
DEPTH = 1
ALPHA = (2.0 * DEPTH) ** 0.25
LN_EPS = 1e-5
NEG_INF = -1e30

LANES = 128
KV_W = 2 * ATT_KV_HEADS * ATT_HEAD_DIM
MAIN_W = 7 * D_MODEL
PROJ_TN = 1024
PROJ_SPLIT_ROWS = 256
FFN_SPLIT_ROWS = 512
OUT_SPLIT_ROWS = 256
VMEM_LIMIT = 60 * 1024 * 1024

COL_QR, COL_KR, COL_VR, COL_GR, COL_QA, COL_SGR, COL_SGA = (i * D_MODEL for i in range(7))


def _params(sem):
    return pltpu.CompilerParams(dimension_semantics=sem, vmem_limit_bytes=VMEM_LIMIT)


def _mods_kernel(c_ref, w_ref, b_ref, o_ref):
    c = c_ref[...]
    a = (c * jax.nn.sigmoid(c)).astype(BF16)
    o_ref[...] = jnp.dot(a, w_ref[...].astype(BF16), preferred_element_type=F32) + b_ref[...]


def _mods(c_all, w_ada, b_ada):
    rows = c_all.shape[0]
    n_out = w_ada.shape[1]
    tn = 1024
    return pl.pallas_call(
        _mods_kernel,
        grid=(n_out // tn,),
        in_specs=[pl.BlockSpec((rows, D_MODEL), lambda n: (0, 0)),
                  pl.BlockSpec((D_MODEL, tn), lambda n: (0, n)),
                  pl.BlockSpec((1, tn), lambda n: (0, n))],
        out_specs=pl.BlockSpec((rows, tn), lambda n: (0, n)),
        out_shape=jax.ShapeDtypeStruct((rows, n_out), F32),
        compiler_params=_params(("arbitrary",)),
        name="mods",
    )(c_all, w_ada, b_ada.reshape(1, n_out))


def _rope_tables(pos):
    posf = pos.astype(F32)[:, None]
    half_r = RET_DIM // 2
    inv_r = 1.0 / (RET_ROPE_THETA ** (jnp.arange(half_r, dtype=F32) / half_r))
    ang_r = posf * inv_r[None, :]
    half_a = ROPE_DIM // 2
    inv_a = 1.0 / (ROPE_THETA ** (jnp.arange(half_a, dtype=F32) / half_a))
    d = jnp.arange(LANES) % ATT_HEAD_DIM
    ang_a = posf * inv_a[None, :]
    cos_a = jnp.tile(jnp.cos(ang_a), (1, LANES // half_a))
    sin_a = jnp.tile(jnp.sin(ang_a), (1, LANES // half_a))
    ca = jnp.where(d[None, :] < ROPE_DIM, cos_a, 1.0)
    s1 = jnp.where(d[None, :] < half_a, -sin_a, 0.0)
    s2 = jnp.where((d[None, :] >= half_a) & (d[None, :] < ROPE_DIM), sin_a, 0.0)
    return jnp.stack([jnp.cos(ang_r), jnp.sin(ang_r), ca, s1, s2])


def _rot_att(x, ca, s1, s2):
    half = ROPE_DIM // 2
    return x * ca + pltpu.roll(x, LANES - half, 1) * s1 + pltpu.roll(x, half, 1) * s2


def _proj_kernel(x_ref, sc_ref, sh_ref, tab_ref, w_ref, wkv_ref, o_ref, kv_ref, h_ref):
    n = pl.program_id(2)
    tm = x_ref.shape[0]
    n_split = max(tm // PROJ_SPLIT_ROWS, 1)
    row_blocks = [slice(r * (tm // n_split), (r + 1) * (tm // n_split)) for r in range(n_split)]

    def mod_rows(ref, rows):
        return ref[...] if ref.shape[0] == 1 else ref[rows, :]

    def att_tabs(rows):
        return tab_ref[2, rows, :], tab_ref[3, rows, :], tab_ref[4, rows, :]

    def modulate_and_kv(rows):
        h = x_ref[rows, :] * (1.0 + mod_rows(sc_ref, rows)) + mod_rows(sh_ref, rows)
        h_ref[rows, :] = h.astype(BF16)
        kv = jnp.dot(h_ref[rows, :], wkv_ref[...], preferred_element_type=F32)
        ca, s1, s2 = att_tabs(rows)
        for s in range(KV_W // LANES):
            xs = kv[:, s * LANES:(s + 1) * LANES]
            if s < KV_W // LANES // 2:
                xs = _rot_att(xs, ca, s1, s2)
            kv_ref[rows, s * LANES:(s + 1) * LANES] = xs.astype(BF16)

    def matmul(rows):
        return jnp.dot(h_ref[rows, :], w_ref[...], preferred_element_type=F32)

    def rot_ret_rows(rows, scale):
        acc = matmul(rows)
        cr, sr = tab_ref[0, rows, :], tab_ref[1, rows, :]
        for j in range(PROJ_TN // RET_DIM):
            a = j * RET_DIM
            x1 = acc[:, a:a + LANES]
            x2 = acc[:, a + LANES:a + 2 * LANES]
            o_ref[rows, a:a + LANES] = ((x1 * cr - x2 * sr) * scale).astype(BF16)
            o_ref[rows, a + LANES:a + 2 * LANES] = ((x1 * sr + x2 * cr) * scale).astype(BF16)

    def rot_ret(scale):
        for rows in row_blocks:
            rot_ret_rows(rows, scale)

    region = n // (D_MODEL // PROJ_TN)

    @pl.when(n == 0)
    def _():
        for rows in row_blocks:
            modulate_and_kv(rows)
            rot_ret_rows(rows, 1.0)

    @pl.when((region == 0) & (n > 0))
    def _():
        rot_ret(1.0)

    @pl.when(region == 1)
    def _():
        rot_ret(RET_DIM ** -0.5)

    @pl.when(region == 2)
    def _():
        for rows in row_blocks:
            o_ref[rows, :] = matmul(rows).astype(BF16)

    @pl.when(region == 3)
    def _():
        for rows in row_blocks:
            acc = matmul(rows)
            o_ref[rows, :] = (acc * jax.nn.sigmoid(acc)).astype(BF16)

    @pl.when(region == 4)
    def _():
        for rows in row_blocks:
            acc = matmul(rows)
            ca, s1, s2 = att_tabs(rows)
            for s in range(PROJ_TN // LANES):
                xs = _rot_att(acc[:, s * LANES:(s + 1) * LANES], ca, s1, s2)
                o_ref[rows, s * LANES:(s + 1) * LANES] = (xs * ATT_HEAD_DIM ** -0.5).astype(BF16)

    @pl.when(region >= 5)
    def _():
        for rows in row_blocks:
            o_ref[rows, :] = jax.nn.sigmoid(matmul(rows)).astype(BF16)


def _mod_spec(mod, tm):
    if mod.shape[1] == 1:
        return pl.BlockSpec((None, 1, D_MODEL), lambda b, t, *_: (b, 0, 0))
    return pl.BlockSpec((None, tm, D_MODEL), lambda b, t, *_: (b, t, 0))


def _proj(x, sc, sh, tab, w_main, w_kv, tm):
    B, T, _ = x.shape
    n_tiles = MAIN_W // PROJ_TN
    return pl.pallas_call(
        _proj_kernel,
        grid=(B, T // tm, n_tiles),
        in_specs=[pl.BlockSpec((None, tm, D_MODEL), lambda b, t, n: (b, t, 0)),
                  _mod_spec(sc, tm), _mod_spec(sh, tm),
                  pl.BlockSpec((5, tm, LANES), lambda b, t, n: (0, t, 0)),
                  pl.BlockSpec((D_MODEL, PROJ_TN), lambda b, t, n: (0, n)),
                  pl.BlockSpec((D_MODEL, KV_W), lambda b, t, n: (0, 0), pipeline_mode=pl.Buffered(1))],
        out_specs=[pl.BlockSpec((None, tm, PROJ_TN), lambda b, t, n: (b, t, n)),
                   pl.BlockSpec((None, tm, KV_W), lambda b, t, n: (b, t, 0))],
        out_shape=[jax.ShapeDtypeStruct((B, T, MAIN_W), BF16),
                   jax.ShapeDtypeStruct((B, T, KV_W), BF16)],
        scratch_shapes=[pltpu.VMEM((tm, D_MODEL), BF16)],
        compiler_params=_params(("arbitrary", "arbitrary", "arbitrary")),
        name="proj",
    )(x, sc, sh, tab, w_main, w_kv)


def _repack_kernel(w_ref, o_ref):
    o_ref[...] = w_ref[...].astype(BF16)


def _repack_w_in(w):
    kv_blk = 5 * D_MODEL // KV_W
    return pl.pallas_call(
        _repack_kernel,
        grid=(MAIN_W // KV_W,),
        in_specs=[pl.BlockSpec((D_MODEL, KV_W), lambda n: (0, jnp.where(n >= kv_blk, n + 1, n)))],
        out_specs=pl.BlockSpec((D_MODEL, KV_W), lambda n: (0, n)),
        out_shape=jax.ShapeDtypeStruct((D_MODEL, MAIN_W), BF16),
        compiler_params=_params(("arbitrary",)),
        name="repack_w_in",
    )(w)


def _kv32_kernel(x_ref, sc_ref, sh_ref, tab_ref, wkv_ref, o_ref):
    h = (x_ref[...] * (1.0 + sc_ref[...]) + sh_ref[...]).astype(BF16)
    kv = jnp.dot(h, wkv_ref[...], preferred_element_type=F32)
    ca, s1, s2 = tab_ref[2], tab_ref[3], tab_ref[4]
    for s in range(KV_W // LANES):
        xs = kv[:, s * LANES:(s + 1) * LANES]
        if s < KV_W // LANES // 2:
            xs = _rot_att(xs, ca, s1, s2)
        o_ref[:, s * LANES:(s + 1) * LANES] = xs


def _kv32(x, sc, sh, tab, w_kv):
    B, R, _ = x.shape
    return pl.pallas_call(
        _kv32_kernel,
        grid=(B, 1),
        in_specs=[pl.BlockSpec((None, R, D_MODEL), lambda b, t: (b, 0, 0)),
                  _mod_spec(sc, R), _mod_spec(sh, R),
                  pl.BlockSpec((5, R, LANES), lambda b, t: (0, 0, 0)),
                  pl.BlockSpec((D_MODEL, KV_W), lambda b, t: (0, 0))],
        out_specs=pl.BlockSpec((None, R, KV_W), lambda b, t: (b, 0, 0)),
        out_shape=jax.ShapeDtypeStruct((B, R, KV_W), F32),
        compiler_params=_params(("arbitrary", "arbitrary")),
        name="kv32",
    )(x, sc, sh, tab, w_kv)


def _ret_kernel(lg_ref, q_ref, k_ref, v_ref, gs_ref, sg_ref, gn_ref, s0_ref, o_ref, sout_ref,
                s_ref, dm_ref, qd_ref, kd_ref, *, chunk, n_chunks, heads):
    hg = pl.program_id(1)
    t = pl.program_id(2)
    nt = (((1,), (1,)), ((), ()))
    tn = (((0,), (0,)), ((), ()))
    for hl in range(heads):
        lg = lg_ref[hg * heads + hl]
        cols = slice(hl * RET_DIM, (hl + 1) * RET_DIM)

        @pl.when(t == 0)
        def _():
            s_ref[hl] = s0_ref[hl]
            i = lax.broadcasted_iota(jnp.int32, (chunk, chunk), 0)
            j = lax.broadcasted_iota(jnp.int32, (chunk, chunk), 1)
            diff = (i - j).astype(F32)
            dm_ref[hl] = jnp.where(diff >= 0, jnp.exp(lg * jnp.maximum(diff, 0.0)), 0.0)
            r = lax.broadcasted_iota(jnp.int32, (chunk, RET_DIM), 0).astype(F32)
            qd_ref[hl] = jnp.exp(lg * (r + 1.0))
            kd_ref[hl] = jnp.exp(lg * (chunk - 1.0 - r))

        g_chunk = jnp.exp(jnp.full((1, RET_DIM), lg * chunk, F32))
        for c in range(n_chunks):
            rows = pl.ds(c * chunk, chunk)
            q, k, v = q_ref[rows, cols], k_ref[rows, cols], v_ref[rows, cols]
            inner = lax.dot_general(q, k, nt, preferred_element_type=F32) * dm_ref[hl]
            o = jnp.dot(inner.astype(BF16), v, preferred_element_type=F32)
            s_prev = s_ref[hl]
            o = o + jnp.dot(q, s_prev.astype(BF16), preferred_element_type=F32) * qd_ref[hl]
            kdec = (k.astype(F32) * kd_ref[hl]).astype(BF16)
            s_ref[hl] = s_prev * g_chunk + lax.dot_general(kdec, v, tn, preferred_element_type=F32)
            mu = jnp.mean(o, axis=-1, keepdims=True)
            xc = o - mu
            var = jnp.mean(xc * xc, axis=-1, keepdims=True)
            gn = xc * lax.rsqrt(var + LN_EPS) * gn_ref[:, cols]
            out = sg_ref[rows, cols].astype(F32) * (gs_ref[rows, cols].astype(F32) * gn)
            o_ref[rows, cols] = out.astype(BF16)

    @pl.when(t == pl.num_programs(2) - 1)
    def _():
        sout_ref[...] = s_ref[...]


def _retention(proj, gn_g, s0, chunk, tt, heads):
    B, T, _ = proj.shape
    n_chunks = tt // chunk
    width = heads * RET_DIM
    log_gamma = jnp.log1p(-jnp.exp2(-5.0 - jnp.arange(RET_HEADS, dtype=F32)))

    def col(off):
        base = off // width
        return pl.BlockSpec((None, tt, width), lambda b, h, t: (b, t, base + h))

    state = pl.BlockSpec((None, heads, RET_DIM, RET_DIM), lambda b, h, t: (b, h, 0, 0))
    return pl.pallas_call(
        functools.partial(_ret_kernel, chunk=chunk, n_chunks=n_chunks, heads=heads),
        grid=(B, RET_HEADS // heads, T // tt),
        in_specs=[pl.BlockSpec(memory_space=pltpu.SMEM),
                  col(COL_QR), col(COL_KR), col(COL_VR), col(COL_GR), col(COL_SGR),
                  pl.BlockSpec((1, width), lambda b, h, t: (0, h)), state],
        out_specs=[pl.BlockSpec((None, tt, width), lambda b, h, t: (b, t, h)), state],
        out_shape=[jax.ShapeDtypeStruct((B, T, D_MODEL), BF16),
                   jax.ShapeDtypeStruct((B, RET_HEADS, RET_DIM, RET_DIM), F32)],
        scratch_shapes=[pltpu.VMEM((heads, RET_DIM, RET_DIM), F32),
                        pltpu.VMEM((heads, chunk, chunk), F32),
                        pltpu.VMEM((heads, chunk, RET_DIM), F32),
                        pltpu.VMEM((heads, chunk, RET_DIM), F32)],
        compiler_params=_params(("arbitrary", "arbitrary", "arbitrary")),
        name="retention",
    )(log_gamma, proj, proj, proj, proj, proj, gn_g.reshape(1, D_MODEL), s0)


def _att_kernel(sink_ref, q_ref, halo_ref, cur_ref, sg_ref, o_ref, *, cq, n_chunks, own_valid, mask_first):
    i = pl.program_id(1)
    kv = jnp.concatenate([halo_ref[...], cur_ref[...]], axis=0).astype(F32)
    n_keys = kv.shape[0]
    n_kc = n_keys // CHUNK
    n_tiles = (n_kc + 1) // 2
    lane = lax.broadcasted_iota(jnp.int32, (1, LANES), 1)
    lo = lane < ATT_HEAD_DIM
    ones_lo = jnp.broadcast_to(jnp.where(lo, 1.0, 0.0), (n_keys, LANES)).astype(BF16)
    ones_hi = jnp.broadcast_to(jnp.where(lo, 0.0, 1.0), (n_keys, LANES)).astype(BF16)
    own_ok = (lane % ATT_HEAD_DIM) < own_valid
    nt = (((1,), (1,)), ((), ()))
    pairs = ATT_GROUP // 2
    rows_q = pairs * cq
    k_off = ATT_KV_HEADS * ATT_HEAD_DIM
    users = [[c for c in range(n_chunks) if c <= 2 * u + 1 and c + 2 >= 2 * u] for u in range(n_tiles)]

    def tile_rows(a_lo, a_hi, u):
        parts = []
        for j in (2 * u, 2 * u + 1):
            for a in (a_lo, a_hi):
                parts.append(a[j * CHUNK:(j + 1) * CHUNK] if j < n_kc else jnp.zeros((CHUNK, a.shape[1]), a.dtype))
        return jnp.concatenate(parts, axis=0)

    for y in range(ATT_KV_HEADS // 2):
        ks = kv[:, y * LANES:(y + 1) * LANES]
        kr = pltpu.roll(ks, ATT_HEAD_DIM, 1)
        vs = kv[:, k_off + y * LANES:k_off + (y + 1) * LANES]
        vr = pltpu.roll(vs, ATT_HEAD_DIM, 1)
        for par in range(2):
            x = 2 * y + par
            k_even, k_odd = (ks, kr) if par == 0 else (kr, ks)
            v_even, v_odd = (vs, vr) if par == 0 else (vr, vs)
            klo = jnp.where(lo, k_even, 0.0).astype(BF16)
            khi = jnp.where(lo, 0.0, k_odd).astype(BF16)
            vlo = jnp.concatenate([jnp.where(lo, v_even, 0.0).astype(BF16), ones_lo], axis=1)
            vhi = jnp.concatenate([jnp.where(lo, 0.0, v_odd).astype(BF16), ones_hi], axis=1)
            sink_b = jnp.concatenate(
                [jnp.broadcast_to(jnp.where(lo, sink_ref[x * ATT_GROUP + 2 * p],
                                            sink_ref[x * ATT_GROUP + 2 * p + 1]), (cq, LANES))
                 for p in range(pairs)], axis=0)
            col0 = x * ATT_GROUP * ATT_HEAD_DIM
            qp = [jnp.concatenate([q_ref[c * cq:(c + 1) * cq, col0 + p * LANES:col0 + (p + 1) * LANES]
                                   for p in range(pairs)], axis=0) for c in range(n_chunks)]
            s_tiles = [lax.dot_general(jnp.concatenate([qp[c] for c in users[u]], axis=0), tile_rows(klo, khi, u),
                                       nt, preferred_element_type=F32) for u in range(n_tiles)]
            p_blocks, m_all = {}, {}
            for c in range(n_chunks):
                sj = []
                for jj in range(3):
                    u, half = (c + jj) // 2, (c + jj) % 2
                    r = users[u].index(c) * rows_q
                    sj.append(s_tiles[u][r:r + rows_q, half * LANES:(half + 1) * LANES])
                if mask_first:
                    for jj in range(2):
                        if c + jj < 2:
                            sj[jj] = jnp.where(i * n_chunks + (c + jj - 2) < 0, NEG_INF, sj[jj])
                if own_valid < CHUNK:
                    sj[2] = jnp.where(own_ok, sj[2], NEG_INF)
                m3 = jnp.maximum(jnp.maximum(sj[0], sj[1]), sj[2])
                m_even = jnp.max(jnp.where(lo, m3, NEG_INF), axis=1, keepdims=True)
                m_odd = jnp.max(jnp.where(lo, NEG_INF, m3), axis=1, keepdims=True)
                m_b = jnp.maximum(jnp.where(lo, m_even, m_odd), sink_b)
                m_all[c] = m_b
                for jj in range(3):
                    p_blocks[(c, c + jj)] = jnp.exp(sj[jj] - m_b).astype(BF16)
            zero_p = jnp.zeros((rows_q, LANES), BF16)
            o_tiles = []
            for u in range(n_tiles):
                p_u = jnp.concatenate(
                    [jnp.concatenate([p_blocks.get((c, j), zero_p) for j in (2 * u, 2 * u + 1)], axis=1)
                     for c in users[u]], axis=0)
                o_tiles.append(jnp.dot(p_u, tile_rows(vlo, vhi, u), preferred_element_type=F32))
            for c in range(n_chunks):
                oe = None
                for u in sorted({c // 2, (c + 2) // 2}):
                    r = users[u].index(c) * rows_q
                    part = o_tiles[u][r:r + rows_q]
                    oe = part if oe is None else oe + part
                den = oe[:, LANES:] + jnp.exp(sink_b - m_all[c])
                res = oe[:, :LANES] / den
                for p in range(pairs):
                    cols = slice(col0 + p * LANES, col0 + (p + 1) * LANES)
                    rows = slice(c * cq, (c + 1) * cq)
                    gate = sg_ref[rows, cols].astype(F32)
                    o_ref[rows, cols] = (gate * res[p * cq:(p + 1) * cq]).astype(BF16)


def _attention(proj, halo_src, cur_src, sinks, cq, n_chunks, own_valid, mask_first):
    B, T, _ = proj.shape
    tq = cq * n_chunks
    cur_rows = CHUNK * n_chunks
    q_blk, sg_blk = COL_QA // D_MODEL, COL_SGA // D_MODEL
    halo_per_cur = cur_rows // WINDOW if mask_first else 0

    def halo_map(b, i):
        return (b, jnp.maximum(i * halo_per_cur - 1, 0), 0) if mask_first else (b, 0, 0)

    return pl.pallas_call(
        functools.partial(_att_kernel, cq=cq, n_chunks=n_chunks, own_valid=own_valid, mask_first=mask_first),
        grid=(B, T // tq),
        in_specs=[pl.BlockSpec(memory_space=pltpu.SMEM),
                  pl.BlockSpec((None, tq, D_MODEL), lambda b, i: (b, i, q_blk)),
                  pl.BlockSpec((None, WINDOW, KV_W), halo_map),
                  pl.BlockSpec((None, cur_rows, KV_W), lambda b, i: (b, i, 0)),
                  pl.BlockSpec((None, tq, D_MODEL), lambda b, i: (b, i, sg_blk))],
        out_specs=pl.BlockSpec((None, tq, D_MODEL), lambda b, i: (b, i, 0)),
        out_shape=jax.ShapeDtypeStruct((B, T, D_MODEL), BF16),
        compiler_params=_params(("arbitrary", "arbitrary")),
        name="attention",
    )(sinks, proj, halo_src, cur_src, proj)


def _layernorm(z, g, b):
    mu = jnp.mean(z, axis=-1, keepdims=True)
    zc = z - mu
    var = jnp.mean(zc * zc, axis=-1, keepdims=True)
    return zc * lax.rsqrt(var + LN_EPS) * g + b


def _out_kernel(r_ref, a_ref, x_ref, gt_ref, sc_ref, sh_ref, w_ref, g_ref, b_ref, x1_ref, h2_ref):
    n_split = max(x_ref.shape[0] // OUT_SPLIT_ROWS, 1)
    rows_per = x_ref.shape[0] // n_split

    def mod_rows(ref, rows):
        return ref[...] if ref.shape[0] == 1 else ref[rows, :]

    for r in range(n_split):
        rows = slice(r * rows_per, (r + 1) * rows_per)
        merged = r_ref[rows, :] + a_ref[rows, :]
        mix = jnp.dot(merged, w_ref[...], preferred_element_type=F32)
        z = ALPHA * x_ref[rows, :] + mod_rows(gt_ref, rows) * mix
        x1 = _layernorm(z, g_ref[...], b_ref[...])
        x1_ref[rows, :] = x1
        h2_ref[rows, :] = (x1 * (1.0 + mod_rows(sc_ref, rows)) + mod_rows(sh_ref, rows)).astype(BF16)


def _out_proj(ret_m, att_m, x, gt, sc, sh, w_o, g, b, tm):
    B, T, _ = x.shape
    tok = pl.BlockSpec((None, tm, D_MODEL), lambda bb, t: (bb, t, 0))
    vec = pl.BlockSpec((1, D_MODEL), lambda bb, t: (0, 0))
    return pl.pallas_call(
        _out_kernel,
        grid=(B, T // tm),
        in_specs=[tok, tok, tok, _mod_spec(gt, tm), _mod_spec(sc, tm), _mod_spec(sh, tm),
                  pl.BlockSpec((D_MODEL, D_MODEL), lambda bb, t: (0, 0)), vec, vec],
        out_specs=[tok, tok],
        out_shape=[jax.ShapeDtypeStruct((B, T, D_MODEL), F32),
                   jax.ShapeDtypeStruct((B, T, D_MODEL), BF16)],
        compiler_params=_params(("arbitrary", "arbitrary")),
        name="out_proj",
    )(ret_m, att_m, x, gt, sc, sh, w_o, g.reshape(1, D_MODEL), b.reshape(1, D_MODEL))


def _ffn_kernel(h_ref, x1_ref, gt_ref, wg_ref, wu_ref, wd_ref, g_ref, b_ref, o_ref):
    f = pl.program_id(2)
    tm = h_ref.shape[0]
    n_split = max(tm // FFN_SPLIT_ROWS, 1)
    row_blocks = [slice(r * (tm // n_split), (r + 1) * (tm // n_split)) for r in range(n_split)]

    @pl.when(f == 0)
    def _():
        o_ref[...] = jnp.zeros_like(o_ref)

    for rows in row_blocks:
        h = h_ref[rows, :]
        a = jnp.dot(h, wg_ref[...], preferred_element_type=F32)
        u = jnp.dot(h, wu_ref[...], preferred_element_type=F32)
        act = (a * jax.nn.sigmoid(a) * u).astype(BF16)
        o_ref[rows, :] += jnp.dot(act, wd_ref[...], preferred_element_type=F32)

    @pl.when(f == pl.num_programs(2) - 1)
    def _():
        for rows in row_blocks:
            gt = gt_ref[...] if gt_ref.shape[0] == 1 else gt_ref[rows, :]
            z = ALPHA * x1_ref[rows, :] + gt * o_ref[rows, :]
            o_ref[rows, :] = _layernorm(z, g_ref[...], b_ref[...])


def _ffn(h2, x1, gt, wg, wu, wd, g, b, tm, tf):
    B, T, _ = x1.shape
    tok = pl.BlockSpec((None, tm, D_MODEL), lambda bb, t, f: (bb, t, 0))
    vec = pl.BlockSpec((1, D_MODEL), lambda bb, t, f: (0, 0))
    return pl.pallas_call(
        _ffn_kernel,
        grid=(B, T // tm, D_FF // tf),
        in_specs=[tok, tok, _mod_spec(gt, tm),
                  pl.BlockSpec((D_MODEL, tf), lambda bb, t, f: (0, f)),
                  pl.BlockSpec((D_MODEL, tf), lambda bb, t, f: (0, f)),
                  pl.BlockSpec((tf, D_MODEL), lambda bb, t, f: (f, 0)), vec, vec],
        out_specs=tok,
        out_shape=jax.ShapeDtypeStruct((B, T, D_MODEL), F32),
        compiler_params=_params(("arbitrary", "arbitrary", "arbitrary")),
        name="ffn",
    )(h2, x1, gt, wg, wu, wd, g.reshape(1, D_MODEL), b.reshape(1, D_MODEL))


def _largest_tile(total, cap):
    t = min(total, cap)
    while total % t:
        t //= 2
    return t


def kernel(x_prompt, x_sample, c_prompt, c_sample, cache_attn_k, cache_attn_v, state_ret, w_ada, b_ada, w_in,
           gn_g, attn_sinks, w_o, ln1_g, ln1_b, w_ffn_gate, w_ffn_up, w_ffn_down, ln2_g, ln2_b):
    B, T, _ = x_prompt.shape
    Bs, Ls, _ = x_sample.shape
    l = 0

    n_c = B + Bs
    pad = (-n_c) % 8
    c_all = jnp.concatenate([c_prompt, c_sample, jnp.zeros((pad, D_MODEL), F32)], axis=0)
    mods = _mods(c_all, w_ada[l], b_ada[l])
    mods_p = [m[:, None, :] for m in jnp.split(mods[:B], 6, axis=-1)]
    mods_s = [jnp.repeat(m, Ls, axis=0)[None] for m in jnp.split(mods[B:n_c], 6, axis=-1)]

    w = w_in[l]
    o_ka = 5 * D_MODEL
    o_gate = o_ka + KV_W
    w_main = _repack_w_in(w)
    w_kv = w[:, o_ka:o_gate].astype(BF16)
    wo, wg, wu, wd = (a[l].astype(BF16) for a in (w_o, w_ffn_gate, w_ffn_up, w_ffn_down))

    def layer(x, mods6, pos, s0, att_fn, tm_proj, tm_out, tm_ffn, ret_chunk, ret_tt):
        sh_a, sc_a, gt_a, sh_f, sc_f, gt_f = mods6
        tab = _rope_tables(pos)
        proj, kvb = _proj(x, sc_a, sh_a, tab, w_main, w_kv, tm_proj)
        ret_m, s_new = _retention(proj, gn_g[l], s0, ret_chunk, ret_tt, 1)
        att_m = att_fn(proj, kvb)
        x1, h2 = _out_proj(ret_m, att_m, x, gt_a, sc_f, sh_f, wo, ln1_g[l], ln1_b[l], tm_out)
        y = _ffn(h2, x1, gt_f, wg, wu, wd, ln2_g[l], ln2_b[l], tm_ffn, 512)
        return y, s_new, tab

    pos_p = jnp.arange(T)
    n_chunks_p = _largest_tile(T // CHUNK, 8)
    y_p, s_p, tab_p = layer(
        x_prompt, mods_p, pos_p, jnp.zeros((B, RET_HEADS, RET_DIM, RET_DIM), F32),
        lambda proj, kvb: _attention(proj, kvb, kvb, attn_sinks[l], CHUNK, n_chunks_p, CHUNK, True),
        _largest_tile(T, 1024), _largest_tile(T, 512), _largest_tile(T, 1024),
        _largest_tile(T, 256), _largest_tile(T, 2048))
    kv_p = _kv32(x_prompt[:, T - WINDOW:], mods_p[1], mods_p[0], tab_p[:, T - WINDOW:], w_kv)
    k_p = kv_p[..., :KV_W // 2].reshape(B, WINDOW, ATT_KV_HEADS, ATT_HEAD_DIM)
    v_p = kv_p[..., KV_W // 2:].reshape(B, WINDOW, ATT_KV_HEADS, ATT_HEAD_DIM)

    R = Bs * Ls
    xs = x_sample.reshape(1, R, D_MODEL)
    pos_s = jnp.tile(PAST_LEN + jnp.arange(Ls), Bs)
    cache = jnp.concatenate([cache_attn_k[l].reshape(Bs, WINDOW, KV_W // 2),
                             cache_attn_v[l].reshape(Bs, WINDOW, KV_W // 2)], axis=-1).astype(BF16)

    def att_sample(proj, kvb):
        new = jnp.pad(kvb.reshape(Bs, Ls, KV_W), ((0, 0), (0, CHUNK - Ls), (0, 0)))
        o = _attention(proj.reshape(Bs, Ls, MAIN_W), cache, new, attn_sinks[l], Ls, 1, Ls, False)
        return o.reshape(1, R, D_MODEL)

    def ret_sample(proj, s0):
        return _retention(proj.reshape(Bs, Ls, MAIN_W), gn_g[l], s0, Ls, Ls, RET_HEADS)

    sh_a, sc_a, gt_a, sh_f, sc_f, gt_f = mods_s
    tab_s = _rope_tables(pos_s)
    proj_s, kvb_s = _proj(xs, sc_a, sh_a, tab_s, w_main, w_kv, R)
    ret_s, s_s = ret_sample(proj_s, state_ret[l])
    att_s = att_sample(proj_s, kvb_s)
    x1_s, h2_s = _out_proj(ret_s.reshape(1, R, D_MODEL), att_s, xs, gt_a, sc_f, sh_f, wo, ln1_g[l], ln1_b[l], R)
    y_s = _ffn(h2_s, x1_s, gt_f, wg, wu, wd, ln2_g[l], ln2_b[l], R, 512).reshape(Bs, Ls, D_MODEL)
    kv_s = _kv32(xs, sc_a, sh_a, tab_s, w_kv).reshape(Bs, Ls, KV_W)
    k_s = kv_s[..., :KV_W // 2].reshape(Bs, Ls, ATT_KV_HEADS, ATT_HEAD_DIM)
    v_s = kv_s[..., KV_W // 2:].reshape(Bs, Ls, ATT_KV_HEADS, ATT_HEAD_DIM)

    return (y_p, y_s, k_p[None], v_p[None], s_p[None], k_s[None], v_s[None], s_s[None])
```

```python
import functools

import jax
import jax.numpy as jnp
from jax import lax
from jax.experimental import pallas as pl
from jax.experimental.pallas import tpu as pltpu

F32 = jnp.float32
BF16 = jnp.bfloat16

D_MODEL = 2048
CHUNK = 64
PAST_LEN = 1024
RET_HEADS = 8
RET_DIM = 256
RET_ROPE_THETA = 10000.0
ATT_HEADS = 32
ATT_KV_HEADS = 4
ATT_HEAD_DIM = 64
ATT_GROUP = ATT_HEADS // ATT_KV_HEADS
WINDOW = 128
ROPE_DIM = ATT_HEAD_DIM // 4
ROPE_THETA = 500000.0
D_FF = 5632
DEPTH = 1
ALPHA = (2.0 * DEPTH) ** 0.25
LN_EPS = 1e-5
NEG_INF = -1e30

LANES = 128
KV_W = 2 * ATT_KV_HEADS * ATT_HEAD_DIM
MAIN_W = 7 * D_MODEL
PROJ_TN = 1024
PROJ_SPLIT_ROWS = 256
FFN_TF = 512
FFN_SPLIT_ROWS = 512
OUT_SPLIT_ROWS = 256
VMEM_LIMIT = 60 * 1024 * 1024

COL_QR, COL_KR, COL_VR, COL_GR, COL_QA, COL_SGR, COL_SGA = (i * D_MODEL for i in range(7))


def _params(sem):
    return pltpu.CompilerParams(dimension_semantics=sem, vmem_limit_bytes=VMEM_LIMIT)


def _mods_kernel(c_ref, w_ref, b_ref, o_ref):
    c = c_ref[...]
    a = (c * jax.nn.sigmoid(c)).astype(BF16)
    o_ref[...] = jnp.dot(a, w_ref[...].astype(BF16), preferred_element_type=F32) + b_ref[...]


def _mods(c_all, w_ada, b_ada):
    rows = c_all.shape[0]
    n_out = w_ada.shape[1]
    tn = 1024
    return pl.pallas_call(
        _mods_kernel,
        grid=(n_out // tn,),
        in_specs=[pl.BlockSpec((rows, D_MODEL), lambda n: (0, 0)),
                  pl.BlockSpec((D_MODEL, tn), lambda n: (0, n)),
                  pl.BlockSpec((1, tn), lambda n: (0, n))],
        out_specs=pl.BlockSpec((rows, tn), lambda n: (0, n)),
        out_shape=jax.ShapeDtypeStruct((rows, n_out), F32),
        compiler_params=_params(("arbitrary",)),
        name="mods",
    )(c_all, w_ada, b_ada.reshape(1, n_out))


def _rope_tables(pos):
    posf = pos.astype(F32)[:, None]
    half_r = RET_DIM // 2
    inv_r = 1.0 / (RET_ROPE_THETA ** (jnp.arange(half_r, dtype=F32) / half_r))
    ang_r = posf * inv_r[None, :]
    half_a = ROPE_DIM // 2
    inv_a = 1.0 / (ROPE_THETA ** (jnp.arange(half_a, dtype=F32) / half_a))
    d = jnp.arange(LANES) % ATT_HEAD_DIM
    ang_a = posf * inv_a[None, :]
    cos_a = jnp.tile(jnp.cos(ang_a), (1, LANES // half_a))
    sin_a = jnp.tile(jnp.sin(ang_a), (1, LANES // half_a))
    ca = jnp.where(d[None, :] < ROPE_DIM, cos_a, 1.0)
    s1 = jnp.where(d[None, :] < half_a, -sin_a, 0.0)
    s2 = jnp.where((d[None, :] >= half_a) & (d[None, :] < ROPE_DIM), sin_a, 0.0)
    return jnp.stack([jnp.cos(ang_r), jnp.sin(ang_r), ca, s1, s2])


def _rot_att(x, ca, s1, s2):
    half = ROPE_DIM // 2
    return x * ca + pltpu.roll(x, LANES - half, 1) * s1 + pltpu.roll(x, half, 1) * s2


def _proj_kernel(x_ref, sc_ref, sh_ref, tab_ref, w_ref, wkv_ref, o_ref, kv_ref, h_ref):
    n = pl.program_id(2)
    tm = x_ref.shape[0]
    n_split = max(tm // PROJ_SPLIT_ROWS, 1)
    row_blocks = [slice(r * (tm // n_split), (r + 1) * (tm // n_split)) for r in range(n_split)]

    def mod_rows(ref, rows):
        return ref[...] if ref.shape[0] == 1 else ref[rows, :]

    def att_tabs(rows):
        return tab_ref[2, rows, :], tab_ref[3, rows, :], tab_ref[4, rows, :]

    def modulate_and_kv(rows):
        h = x_ref[rows, :] * (1.0 + mod_rows(sc_ref, rows)) + mod_rows(sh_ref, rows)
        h_ref[rows, :] = h.astype(BF16)
        kv = jnp.dot(h_ref[rows, :], wkv_ref[...], preferred_element_type=F32)
        ca, s1, s2 = att_tabs(rows)
        for s in range(KV_W // LANES):
            xs = kv[:, s * LANES:(s + 1) * LANES]
            if s < KV_W // LANES // 2:
                xs = _rot_att(xs, ca, s1, s2)
            kv_ref[rows, s * LANES:(s + 1) * LANES] = xs.astype(BF16)

    def matmul(rows):
        return jnp.dot(h_ref[rows, :], w_ref[...], preferred_element_type=F32)

    def rot_ret_rows(rows, scale):
        acc = matmul(rows)
        cr, sr = tab_ref[0, rows, :], tab_ref[1, rows, :]
        for j in range(PROJ_TN // RET_DIM):
            a = j * RET_DIM
            x1 = acc[:, a:a + LANES]
            x2 = acc[:, a + LANES:a + 2 * LANES]
            o_ref[rows, a:a + LANES] = ((x1 * cr - x2 * sr) * scale).astype(BF16)
            o_ref[rows, a + LANES:a + 2 * LANES] = ((x1 * sr + x2 * cr) * scale).astype(BF16)

    def rot_ret(scale):
        for rows in row_blocks:
            rot_ret_rows(rows, scale)

    region = n // (D_MODEL // PROJ_TN)

    @pl.when(n == 0)
    def _():
        for rows in row_blocks:
            modulate_and_kv(rows)
            rot_ret_rows(rows, 1.0)

    @pl.when((region == 0) & (n > 0))
    def _():
        rot_ret(1.0)

    @pl.when(region == 1)
    def _():
        rot_ret(RET_DIM ** -0.5)

    @pl.when(region == 2)
    def _():
        for rows in row_blocks:
            o_ref[rows, :] = matmul(rows).astype(BF16)

    @pl.when(region == 3)
    def _():
        for rows in row_blocks:
            acc = matmul(rows)
            o_ref[rows, :] = (acc * jax.nn.sigmoid(acc)).astype(BF16)

    @pl.when(region == 4)
    def _():
        for rows in row_blocks:
            acc = matmul(rows)
            ca, s1, s2 = att_tabs(rows)
            for s in range(PROJ_TN // LANES):
                xs = _rot_att(acc[:, s * LANES:(s + 1) * LANES], ca, s1, s2)
                o_ref[rows, s * LANES:(s + 1) * LANES] = (xs * ATT_HEAD_DIM ** -0.5).astype(BF16)

    @pl.when(region >= 5)
    def _():
        for rows in row_blocks:
            o_ref[rows, :] = jax.nn.sigmoid(matmul(rows)).astype(BF16)


def _mod_spec(mod, tm):
    if mod.shape[1] == 1:
        return pl.BlockSpec((None, 1, D_MODEL), lambda b, t, *_: (b, 0, 0))
    return pl.BlockSpec((None, tm, D_MODEL), lambda b, t, *_: (b, t, 0))


def _proj(x, sc, sh, tab, w_main, w_kv, tm):
    B, T, _ = x.shape
    n_tiles = MAIN_W // PROJ_TN
    return pl.pallas_call(
        _proj_kernel,
        grid=(B, T // tm, n_tiles),
        in_specs=[pl.BlockSpec((None, tm, D_MODEL), lambda b, t, n: (b, t, 0)),
                  _mod_spec(sc, tm), _mod_spec(sh, tm),
                  pl.BlockSpec((5, tm, LANES), lambda b, t, n: (0, t, 0)),
                  pl.BlockSpec((D_MODEL, PROJ_TN), lambda b, t, n: (0, n)),
                  pl.BlockSpec((D_MODEL, KV_W), lambda b, t, n: (0, 0), pipeline_mode=pl.Buffered(1))],
        out_specs=[pl.BlockSpec((None, tm, PROJ_TN), lambda b, t, n: (b, t, n)),
                   pl.BlockSpec((None, tm, KV_W), lambda b, t, n: (b, t, 0))],
        out_shape=[jax.ShapeDtypeStruct((B, T, MAIN_W), BF16),
                   jax.ShapeDtypeStruct((B, T, KV_W), BF16)],
        scratch_shapes=[pltpu.VMEM((tm, D_MODEL), BF16)],
        compiler_params=_params(("arbitrary", "arbitrary", "arbitrary")),
        name="proj",
    )(x, sc, sh, tab, w_main, w_kv)


def _repack_kernel(w_ref, o_ref):
    o_ref[...] = w_ref[...].astype(BF16)


def _repack_w_in(w):
    kv_blk = 5 * D_MODEL // KV_W
    return pl.pallas_call(
        _repack_kernel,
        grid=(MAIN_W // KV_W,),
        in_specs=[pl.BlockSpec((D_MODEL, KV_W), lambda n: (0, jnp.where(n >= kv_blk, n + 1, n)))],
        out_specs=pl.BlockSpec((D_MODEL, KV_W), lambda n: (0, n)),
        out_shape=jax.ShapeDtypeStruct((D_MODEL, MAIN_W), BF16),
        compiler_params=_params(("arbitrary",)),
        name="repack_w_in",
    )(w)


def _kv32_kernel(x_ref, sc_ref, sh_ref, tab_ref, wkv_ref, o_ref):
    h = (x_ref[...] * (1.0 + sc_ref[...]) + sh_ref[...]).astype(BF16)
    kv = jnp.dot(h, wkv_ref[...], preferred_element_type=F32)
    ca, s1, s2 = tab_ref[2], tab_ref[3], tab_ref[4]
    for s in range(KV_W // LANES):
        xs = kv[:, s * LANES:(s + 1) * LANES]
        if s < KV_W // LANES // 2:
            xs = _rot_att(xs, ca, s1, s2)
        o_ref[:, s * LANES:(s + 1) * LANES] = xs


def _kv32(x, sc, sh, tab, w_kv):
    B, R, _ = x.shape
    return pl.pallas_call(
        _kv32_kernel,
        grid=(B, 1),
        in_specs=[pl.BlockSpec((None, R, D_MODEL), lambda b, t: (b, 0, 0)),
                  _mod_spec(sc, R), _mod_spec(sh, R),
                  pl.BlockSpec((5, R, LANES), lambda b, t: (0, 0, 0)),
                  pl.BlockSpec((D_MODEL, KV_W), lambda b, t: (0, 0))],
        out_specs=pl.BlockSpec((None, R, KV_W), lambda b, t: (b, 0, 0)),
        out_shape=jax.ShapeDtypeStruct((B, R, KV_W), F32),
        compiler_params=_params(("arbitrary", "arbitrary")),
        name="kv32",
    )(x, sc, sh, tab, w_kv)


def _ret_kernel(lg_ref, q_ref, k_ref, v_ref, gs_ref, sg_ref, gn_ref, s0_ref, o_ref, sout_ref,
                s_ref, dm_ref, qd_ref, kd_ref, *, chunk, n_chunks, heads):
    hg = pl.program_id(1)
    t = pl.program_id(2)
    nt = (((1,), (1,)), ((), ()))
    tn = (((0,), (0,)), ((), ()))
    for hl in range(heads):
        lg = lg_ref[hg * heads + hl]
        cols = slice(hl * RET_DIM, (hl + 1) * RET_DIM)

        @pl.when(t == 0)
        def _():
            s_ref[hl] = s0_ref[hl]
            i = lax.broadcasted_iota(jnp.int32, (chunk, chunk), 0)
            j = lax.broadcasted_iota(jnp.int32, (chunk, chunk), 1)
            diff = (i - j).astype(F32)
            dm_ref[hl] = jnp.where(diff >= 0, jnp.exp(lg * jnp.maximum(diff, 0.0)), 0.0)
            r = lax.broadcasted_iota(jnp.int32, (chunk, RET_DIM), 0).astype(F32)
            qd_ref[hl] = jnp.exp(lg * (r + 1.0))
            kd_ref[hl] = jnp.exp(lg * (chunk - 1.0 - r))

        g_chunk = jnp.exp(jnp.full((1, RET_DIM), lg * chunk, F32))
        for c in range(n_chunks):
            rows = pl.ds(c * chunk, chunk)
            q, k, v = q_ref[rows, cols], k_ref[rows, cols], v_ref[rows, cols]
            inner = (lax.dot_general(q, k, nt, preferred_element_type=F32) * dm_ref[hl]).astype(BF16)
            s_prev = s_ref[hl]
            if chunk % LANES == 0:
                q_dec = (q.astype(F32) * qd_ref[hl]).astype(BF16)
                o = jnp.dot(jnp.concatenate([inner, q_dec], axis=1),
                            jnp.concatenate([v, s_prev.astype(BF16)], axis=0), preferred_element_type=F32)
            else:
                o = jnp.dot(inner, v, preferred_element_type=F32)
                o = o + jnp.dot(q, s_prev.astype(BF16), preferred_element_type=F32) * qd_ref[hl]
            kdec = (k.astype(F32) * kd_ref[hl]).astype(BF16)
            s_ref[hl] = s_prev * g_chunk + lax.dot_general(kdec, v, tn, preferred_element_type=F32)
            mu = jnp.mean(o, axis=-1, keepdims=True)
            xc = o - mu
            var = jnp.mean(xc * xc, axis=-1, keepdims=True)
            gn = xc * lax.rsqrt(var + LN_EPS) * gn_ref[:, cols]
            out = sg_ref[rows, cols].astype(F32) * (gs_ref[rows, cols].astype(F32) * gn)
            o_ref[rows, cols] = out.astype(BF16)

    @pl.when(t == pl.num_programs(2) - 1)
    def _():
        sout_ref[...] = s_ref[...]


def _retention(proj, gn_g, s0, chunk, tt, heads):
    B, T, _ = proj.shape
    n_chunks = tt // chunk
    width = heads * RET_DIM
    log_gamma = jnp.log1p(-jnp.exp2(-5.0 - jnp.arange(RET_HEADS, dtype=F32)))

    def col(off):
        base = off // width
        return pl.BlockSpec((None, tt, width), lambda b, h, t: (b, t, base + h))

    state = pl.BlockSpec((None, heads, RET_DIM, RET_DIM), lambda b, h, t: (b, h, 0, 0))
    return pl.pallas_call(
        functools.partial(_ret_kernel, chunk=chunk, n_chunks=n_chunks, heads=heads),
        grid=(B, RET_HEADS // heads, T // tt),
        in_specs=[pl.BlockSpec(memory_space=pltpu.SMEM),
                  col(COL_QR), col(COL_KR), col(COL_VR), col(COL_GR), col(COL_SGR),
                  pl.BlockSpec((1, width), lambda b, h, t: (0, h)), state],
        out_specs=[pl.BlockSpec((None, tt, width), lambda b, h, t: (b, t, h)), state],
        out_shape=[jax.ShapeDtypeStruct((B, T, D_MODEL), BF16),
                   jax.ShapeDtypeStruct((B, RET_HEADS, RET_DIM, RET_DIM), F32)],
        scratch_shapes=[pltpu.VMEM((heads, RET_DIM, RET_DIM), F32),
                        pltpu.VMEM((heads, chunk, chunk), F32),
                        pltpu.VMEM((heads, chunk, RET_DIM), F32),
                        pltpu.VMEM((heads, chunk, RET_DIM), F32)],
        compiler_params=_params(("arbitrary", "arbitrary", "arbitrary")),
        name="retention",
    )(log_gamma, proj, proj, proj, proj, proj, gn_g.reshape(1, D_MODEL), s0)


def _att_kernel(sink_ref, q_ref, halo_ref, cur_ref, sg_ref, o_ref, *, cq, n_chunks, own_valid, mask_first):
    i = pl.program_id(1)
    kv = jnp.concatenate([halo_ref[...], cur_ref[...]], axis=0).astype(F32)
    n_keys = kv.shape[0]
    n_kc = n_keys // CHUNK
    n_tiles = (n_kc + 1) // 2
    lane = lax.broadcasted_iota(jnp.int32, (1, LANES), 1)
    lo = lane < ATT_HEAD_DIM
    ones_lo = jnp.broadcast_to(jnp.where(lo, 1.0, 0.0), (n_keys, LANES)).astype(BF16)
    ones_hi = jnp.broadcast_to(jnp.where(lo, 0.0, 1.0), (n_keys, LANES)).astype(BF16)
    own_ok = (lane % ATT_HEAD_DIM) < own_valid
    nt = (((1,), (1,)), ((), ()))
    pairs = ATT_GROUP // 2
    rows_q = pairs * cq
    k_off = ATT_KV_HEADS * ATT_HEAD_DIM
    users = [[c for c in range(n_chunks) if c <= 2 * u + 1 and c + 2 >= 2 * u] for u in range(n_tiles)]

    def tile_rows(a_lo, a_hi, u):
        parts = []
        for j in (2 * u, 2 * u + 1):
            for a in (a_lo, a_hi):
                parts.append(a[j * CHUNK:(j + 1) * CHUNK] if j < n_kc else jnp.zeros((CHUNK, a.shape[1]), a.dtype))
        return jnp.concatenate(parts, axis=0)

    for y in range(ATT_KV_HEADS // 2):
        ks = kv[:, y * LANES:(y + 1) * LANES]
        kr = pltpu.roll(ks, ATT_HEAD_DIM, 1)
        vs = kv[:, k_off + y * LANES:k_off + (y + 1) * LANES]
        vr = pltpu.roll(vs, ATT_HEAD_DIM, 1)
        for par in range(2):
            x = 2 * y + par
            k_even, k_odd = (ks, kr) if par == 0 else (kr, ks)
            v_even, v_odd = (vs, vr) if par == 0 else (vr, vs)
            klo = jnp.where(lo, k_even, 0.0).astype(BF16)
            khi = jnp.where(lo, 0.0, k_odd).astype(BF16)
            vlo = jnp.concatenate([jnp.where(lo, v_even, 0.0).astype(BF16), ones_lo], axis=1)
            vhi = jnp.concatenate([jnp.where(lo, 0.0, v_odd).astype(BF16), ones_hi], axis=1)
            sink_b = jnp.concatenate(
                [jnp.broadcast_to(jnp.where(lo, sink_ref[x * ATT_GROUP + 2 * p],
                                            sink_ref[x * ATT_GROUP + 2 * p + 1]), (cq, LANES))
                 for p in range(pairs)], axis=0)
            col0 = x * ATT_GROUP * ATT_HEAD_DIM
            qp = [jnp.concatenate([q_ref[c * cq:(c + 1) * cq, col0 + p * LANES:col0 + (p + 1) * LANES]
                                   for p in range(pairs)], axis=0) for c in range(n_chunks)]
            s_tiles = [lax.dot_general(jnp.concatenate([qp[c] for c in users[u]], axis=0), tile_rows(klo, khi, u),
                                       nt, preferred_element_type=F32) for u in range(n_tiles)]
            p_blocks, m_all = {}, {}
            for c in range(n_chunks):
                sj = []
                for jj in range(3):
                    u, half = (c + jj) // 2, (c + jj) % 2
                    r = users[u].index(c) * rows_q
                    sj.append(s_tiles[u][r:r + rows_q, half * LANES:(half + 1) * LANES])
                if mask_first:
                    for jj in range(2):
                        if c + jj < 2:
                            sj[jj] = jnp.where(i * n_chunks + (c + jj - 2) < 0, NEG_INF, sj[jj])
                if own_valid < CHUNK:
                    sj[2] = jnp.where(own_ok, sj[2], NEG_INF)
                m3 = jnp.maximum(jnp.maximum(sj[0], sj[1]), sj[2])
                m_even = jnp.max(jnp.where(lo, m3, NEG_INF), axis=1, keepdims=True)
                m_odd = jnp.max(jnp.where(lo, NEG_INF, m3), axis=1, keepdims=True)
                m_b = jnp.maximum(jnp.where(lo, m_even, m_odd), sink_b)
                m_all[c] = m_b
                for jj in range(3):
                    p_blocks[(c, c + jj)] = jnp.exp(sj[jj] - m_b).astype(BF16)
            zero_p = jnp.zeros((rows_q, LANES), BF16)
            o_tiles = []
            for u in range(n_tiles):
                p_u = jnp.concatenate(
                    [jnp.concatenate([p_blocks.get((c, j), zero_p) for j in (2 * u, 2 * u + 1)], axis=1)
                     for c in users[u]], axis=0)
                o_tiles.append(jnp.dot(p_u, tile_rows(vlo, vhi, u), preferred_element_type=F32))
            for c in range(n_chunks):
                oe = None
                for u in sorted({c // 2, (c + 2) // 2}):
                    r = users[u].index(c) * rows_q
                    part = o_tiles[u][r:r + rows_q]
                    oe = part if oe is None else oe + part
                den = oe[:, LANES:] + jnp.exp(sink_b - m_all[c])
                res = oe[:, :LANES] / den
                for p in range(pairs):
                    cols = slice(col0 + p * LANES, col0 + (p + 1) * LANES)
                    rows = slice(c * cq, (c + 1) * cq)
                    gate = sg_ref[rows, cols].astype(F32)
                    o_ref[rows, cols] = (gate * res[p * cq:(p + 1) * cq]).astype(BF16)


def _attention(proj, halo_src, cur_src, sinks, cq, n_chunks, own_valid, mask_first):
    B, T, _ = proj.shape
    tq = cq * n_chunks
    cur_rows = CHUNK * n_chunks
    q_blk, sg_blk = COL_QA // D_MODEL, COL_SGA // D_MODEL
    halo_per_cur = cur_rows // WINDOW if mask_first else 0

    def halo_map(b, i):
        return (b, jnp.maximum(i * halo_per_cur - 1, 0), 0) if mask_first else (b, 0, 0)

    return pl.pallas_call(
        functools.partial(_att_kernel, cq=cq, n_chunks=n_chunks, own_valid=own_valid, mask_first=mask_first),
        grid=(B, T // tq),
        in_specs=[pl.BlockSpec(memory_space=pltpu.SMEM),
                  pl.BlockSpec((None, tq, D_MODEL), lambda b, i: (b, i, q_blk)),
                  pl.BlockSpec((None, WINDOW, KV_W), halo_map),
                  pl.BlockSpec((None, cur_rows, KV_W), lambda b, i: (b, i, 0)),
                  pl.BlockSpec((None, tq, D_MODEL), lambda b, i: (b, i, sg_blk))],
        out_specs=pl.BlockSpec((None, tq, D_MODEL), lambda b, i: (b, i, 0)),
        out_shape=jax.ShapeDtypeStruct((B, T, D_MODEL), BF16),
        compiler_params=_params(("arbitrary", "arbitrary")),
        name="attention",
    )(sinks, proj, halo_src, cur_src, proj)


def _layernorm(z, g, b):
    mu = jnp.mean(z, axis=-1, keepdims=True)
    zc = z - mu
    var = jnp.mean(zc * zc, axis=-1, keepdims=True)
    return zc * lax.rsqrt(var + LN_EPS) * g + b


def _out_kernel(r_ref, a_ref, x_ref, gt_ref, sc_ref, sh_ref, w_ref, g_ref, b_ref, x1_ref, h2_ref):
    n_split = max(x_ref.shape[0] // OUT_SPLIT_ROWS, 1)
    rows_per = x_ref.shape[0] // n_split

    def mod_rows(ref, rows):
        return ref[...] if ref.shape[0] == 1 else ref[rows, :]

    for r in range(n_split):
        rows = slice(r * rows_per, (r + 1) * rows_per)
        merged = r_ref[rows, :] + a_ref[rows, :]
        mix = jnp.dot(merged, w_ref[...], preferred_element_type=F32)
        z = ALPHA * x_ref[rows, :] + mod_rows(gt_ref, rows) * mix
        x1 = _layernorm(z, g_ref[...], b_ref[...])
        x1_ref[rows, :] = x1
        h2_ref[rows, :] = (x1 * (1.0 + mod_rows(sc_ref, rows)) + mod_rows(sh_ref, rows)).astype(BF16)


def _out_proj(ret_m, att_m, x, gt, sc, sh, w_o, g, b, tm):
    B, T, _ = x.shape
    tok = pl.BlockSpec((None, tm, D_MODEL), lambda bb, t: (bb, t, 0))
    vec = pl.BlockSpec((1, D_MODEL), lambda bb, t: (0, 0))
    return pl.pallas_call(
        _out_kernel,
        grid=(B, T // tm),
        in_specs=[tok, tok, tok, _mod_spec(gt, tm), _mod_spec(sc, tm), _mod_spec(sh, tm),
                  pl.BlockSpec((D_MODEL, D_MODEL), lambda bb, t: (0, 0)), vec, vec],
        out_specs=[tok, tok],
        out_shape=[jax.ShapeDtypeStruct((B, T, D_MODEL), F32),
                   jax.ShapeDtypeStruct((B, T, D_MODEL), BF16)],
        compiler_params=_params(("arbitrary", "arbitrary")),
        name="out_proj",
    )(ret_m, att_m, x, gt, sc, sh, w_o, g.reshape(1, D_MODEL), b.reshape(1, D_MODEL))


def _ffn_kernel(h_ref, x1_ref, gt_ref, wgu_ref, wd_ref, g_ref, b_ref, o_ref):
    f = pl.program_id(2)
    tm = h_ref.shape[0]
    tf = wd_ref.shape[0]
    n_split = max(tm // FFN_SPLIT_ROWS, 1)
    row_blocks = [slice(r * (tm // n_split), (r + 1) * (tm // n_split)) for r in range(n_split)]

    @pl.when(f == 0)
    def _():
        o_ref[...] = jnp.zeros_like(o_ref)

    for rows in row_blocks:
        au = jnp.dot(h_ref[rows, :], wgu_ref[...], preferred_element_type=F32)
        a, u = au[:, :tf], au[:, tf:]
        act = (a * jax.nn.sigmoid(a) * u).astype(BF16)
        o_ref[rows, :] += jnp.dot(act, wd_ref[...], preferred_element_type=F32)

    @pl.when(f == pl.num_programs(2) - 1)
    def _():
        for rows in row_blocks:
            gt = gt_ref[...] if gt_ref.shape[0] == 1 else gt_ref[rows, :]
            z = ALPHA * x1_ref[rows, :] + gt * o_ref[rows, :]
            o_ref[rows, :] = _layernorm(z, g_ref[...], b_ref[...])


def _repack_gate_up_kernel(wg_ref, wu_ref, o_ref):
    j = pl.program_id(1)

    @pl.when(j == 0)
    def _():
        o_ref[...] = wg_ref[...].astype(BF16)

    @pl.when(j == 1)
    def _():
        o_ref[...] = wu_ref[...].astype(BF16)


def _repack_gate_up(wg, wu, tf):
    src = pl.BlockSpec((D_MODEL, tf), lambda f, j: (0, f))
    return pl.pallas_call(
        _repack_gate_up_kernel,
        grid=(D_FF // tf, 2),
        in_specs=[src, src],
        out_specs=pl.BlockSpec((None, D_MODEL, tf), lambda f, j: (f, 0, j)),
        out_shape=jax.ShapeDtypeStruct((D_FF // tf, D_MODEL, 2 * tf), BF16),
        compiler_params=_params(("arbitrary", "arbitrary")),
        name="repack_gate_up",
    )(wg, wu)


def _ffn(h2, x1, gt, wgu, wd, g, b, tm):
    B, T, _ = x1.shape
    tf = wgu.shape[2] // 2
    tok = pl.BlockSpec((None, tm, D_MODEL), lambda bb, t, f: (bb, t, 0))
    vec = pl.BlockSpec((1, D_MODEL), lambda bb, t, f: (0, 0))
    return pl.pallas_call(
        _ffn_kernel,
        grid=(B, T // tm, D_FF // tf),
        in_specs=[tok, tok, _mod_spec(gt, tm),
                  pl.BlockSpec((None, D_MODEL, 2 * tf), lambda bb, t, f: (f, 0, 0)),
                  pl.BlockSpec((tf, D_MODEL), lambda bb, t, f: (f, 0)), vec, vec],
        out_specs=tok,
        out_shape=jax.ShapeDtypeStruct((B, T, D_MODEL), F32),
        compiler_params=_params(("arbitrary", "arbitrary", "arbitrary")),
        name="ffn",
    )(h2, x1, gt, wgu, wd, g.reshape(1, D_MODEL), b.reshape(1, D_MODEL))


def _largest_tile(total, cap):
    t = min(total, cap)
    while total % t:
        t //= 2
    return t


def kernel(x_prompt, x_sample, c_prompt, c_sample, cache_attn_k, cache_attn_v, state_ret, w_ada, b_ada, w_in,
           gn_g, attn_sinks, w_o, ln1_g, ln1_b, w_ffn_gate, w_ffn_up, w_ffn_down, ln2_g, ln2_b):
    B, T, _ = x_prompt.shape
    Bs, Ls, _ = x_sample.shape
    l = 0

    n_c = B + Bs
    pad = (-n_c) % 8
    c_all = jnp.concatenate([c_prompt, c_sample, jnp.zeros((pad, D_MODEL), F32)], axis=0)
    mods = _mods(c_all, w_ada[l], b_ada[l])
    mods_p = [m[:, None, :] for m in jnp.split(mods[:B], 6, axis=-1)]
    mods_s = [jnp.repeat(m, Ls, axis=0)[None] for m in jnp.split(mods[B:n_c], 6, axis=-1)]

    w = w_in[l]
    o_ka = 5 * D_MODEL
    o_gate = o_ka + KV_W
    w_main = _repack_w_in(w)
    w_kv = w[:, o_ka:o_gate].astype(BF16)
    wo, wd = w_o[l].astype(BF16), w_ffn_down[l].astype(BF16)
    wgu = _repack_gate_up(w_ffn_gate[l], w_ffn_up[l], FFN_TF)

    def layer(x, mods6, pos, s0, att_fn, tm_proj, tm_out, tm_ffn, ret_chunk, ret_tt):
        sh_a, sc_a, gt_a, sh_f, sc_f, gt_f = mods6
        tab = _rope_tables(pos)
        proj, kvb = _proj(x, sc_a, sh_a, tab, w_main, w_kv, tm_proj)
        ret_m, s_new = _retention(proj, gn_g[l], s0, ret_chunk, ret_tt, 1)
        att_m = att_fn(proj, kvb)
        x1, h2 = _out_proj(ret_m, att_m, x, gt_a, sc_f, sh_f, wo, ln1_g[l], ln1_b[l], tm_out)
        y = _ffn(h2, x1, gt_f, wgu, wd, ln2_g[l], ln2_b[l], tm_ffn)
        return y, s_new, tab

    pos_p = jnp.arange(T)
    n_chunks_p = _largest_tile(T // CHUNK, 8)
    y_p, s_p, tab_p = layer(
        x_prompt, mods_p, pos_p, jnp.zeros((B, RET_HEADS, RET_DIM, RET_DIM), F32),
        lambda proj, kvb: _attention(proj, kvb, kvb, attn_sinks[l], CHUNK, n_chunks_p, CHUNK, True),
        _largest_tile(T, 1024), _largest_tile(T, 512), _largest_tile(T, 1024),
        _largest_tile(T, 256), _largest_tile(T, 2048))
    kv_p = _kv32(x_prompt[:, T - WINDOW:], mods_p[1], mods_p[0], tab_p[:, T - WINDOW:], w_kv)
    k_p = kv_p[..., :KV_W // 2].reshape(B, WINDOW, ATT_KV_HEADS, ATT_HEAD_DIM)
    v_p = kv_p[..., KV_W // 2:].reshape(B, WINDOW, ATT_KV_HEADS, ATT_HEAD_DIM)

    R = Bs * Ls
    xs = x_sample.reshape(1, R, D_MODEL)
    pos_s = jnp.tile(PAST_LEN + jnp.arange(Ls), Bs)
    cache = jnp.concatenate([cache_attn_k[l].reshape(Bs, WINDOW, KV_W // 2),
                             cache_attn_v[l].reshape(Bs, WINDOW, KV_W // 2)], axis=-1).astype(BF16)

    def att_sample(proj, kvb):
        new = jnp.pad(kvb.reshape(Bs, Ls, KV_W), ((0, 0), (0, CHUNK - Ls), (0, 0)))
        o = _attention(proj.reshape(Bs, Ls, MAIN_W), cache, new, attn_sinks[l], Ls, 1, Ls, False)
        return o.reshape(1, R, D_MODEL)

    def ret_sample(proj, s0):
        return _retention(proj.reshape(Bs, Ls, MAIN_W), gn_g[l], s0, Ls, Ls, RET_HEADS)

    sh_a, sc_a, gt_a, sh_f, sc_f, gt_f = mods_s
    tab_s = _rope_tables(pos_s)
    proj_s, kvb_s = _proj(xs, sc_a, sh_a, tab_s, w_main, w_kv, R)
    ret_s, s_s = ret_sample(proj_s, state_ret[l])
    att_s = att_sample(proj_s, kvb_s)
    x1_s, h2_s = _out_proj(ret_s.reshape(1, R, D_MODEL), att_s, xs, gt_a, sc_f, sh_f, wo, ln1_g[l], ln1_b[l], R)
    y_s = _ffn(h2_s, x1_s, gt_f, wgu, wd, ln2_g[l], ln2_b[l], R).reshape(Bs, Ls, D_MODEL)
    kv_s = _kv32(xs, sc_a, sh_a, tab_s, w_kv).reshape(Bs, Ls, KV_W)
    k_s = kv_s[..., :KV_W // 2].reshape(Bs, Ls, ATT_KV_HEADS, ATT_HEAD_DIM)
    v_s = kv_s[..., KV_W // 2:].reshape(Bs, Ls, ATT_KV_HEADS, ATT_HEAD_DIM)

    return (y_p, y_s, k_p[None], v_p[None], s_p[None], k_s[None], v_s[None], s_s[None])
```

```python
import functools

import jax
import jax.numpy as jnp
from jax import lax
from jax.experimental import pallas as pl
from jax.experimental.pallas import tpu as pltpu

F32 = jnp.float32
BF16 = jnp.bfloat16

D_MODEL = 2048
CHUNK = 64
PAST_LEN = 1024
RET_HEADS = 8
RET_DIM = 256
RET_ROPE_THETA = 10000.0
ATT_HEADS = 32
ATT_KV_HEADS = 4
ATT_HEAD_DIM = 64
ATT_GROUP = ATT_HEADS // ATT_KV_HEADS
WINDOW = 128
ROPE_DIM = ATT_HEAD_DIM // 4
ROPE_THETA = 500000.0
D_FF = 5632
DEPTH = 1
ALPHA = (2.0 * DEPTH) ** 0.25
LN_EPS = 1e-5
NEG_INF = -1e30

LANES = 128
KV_W = 2 * ATT_KV_HEADS * ATT_HEAD_DIM
MAIN_W = 7 * D_MODEL
PROJ_TN = 2048
PROJ_SPLIT_ROWS = 256
FFN_TF = 512
FFN_SPLIT_ROWS = 512
FFN_LAST_SPLIT_ROWS = 256
OUT_SPLIT_ROWS = 256
VMEM_LIMIT = 60 * 1024 * 1024

COL_QR, COL_KR, COL_VR, COL_GR, COL_QA, COL_SGR, COL_SGA = (i * D_MODEL for i in range(7))


def _params(sem):
    return pltpu.CompilerParams(dimension_semantics=sem, vmem_limit_bytes=VMEM_LIMIT)


def _mods_kernel(c_ref, w_ref, b_ref, o_ref):
    c = c_ref[...]
    a = (c * jax.nn.sigmoid(c)).astype(BF16)
    o_ref[...] = jnp.dot(a, w_ref[...].astype(BF16), preferred_element_type=F32) + b_ref[...]


def _mods(c_all, w_ada, b_ada):
    rows = c_all.shape[0]
    n_out = w_ada.shape[1]
    tn = 1024
    return pl.pallas_call(
        _mods_kernel,
        grid=(n_out // tn,),
        in_specs=[pl.BlockSpec((rows, D_MODEL), lambda n: (0, 0)),
                  pl.BlockSpec((D_MODEL, tn), lambda n: (0, n)),
                  pl.BlockSpec((1, tn), lambda n: (0, n))],
        out_specs=pl.BlockSpec((rows, tn), lambda n: (0, n)),
        out_shape=jax.ShapeDtypeStruct((rows, n_out), F32),
        compiler_params=_params(("arbitrary",)),
        name="mods",
    )(c_all, w_ada, b_ada.reshape(1, n_out))


def _rope_tables(pos):
    posf = pos.astype(F32)[:, None]
    half_r = RET_DIM // 2
    inv_r = 1.0 / (RET_ROPE_THETA ** (jnp.arange(half_r, dtype=F32) / half_r))
    ang_r = posf * inv_r[None, :]
    half_a = ROPE_DIM // 2
    inv_a = 1.0 / (ROPE_THETA ** (jnp.arange(half_a, dtype=F32) / half_a))
    d = jnp.arange(LANES) % ATT_HEAD_DIM
    ang_a = posf * inv_a[None, :]
    cos_a = jnp.tile(jnp.cos(ang_a), (1, LANES // half_a))
    sin_a = jnp.tile(jnp.sin(ang_a), (1, LANES // half_a))
    ca = jnp.where(d[None, :] < ROPE_DIM, cos_a, 1.0)
    s1 = jnp.where(d[None, :] < half_a, -sin_a, 0.0)
    s2 = jnp.where((d[None, :] >= half_a) & (d[None, :] < ROPE_DIM), sin_a, 0.0)
    return jnp.stack([jnp.cos(ang_r), jnp.sin(ang_r), ca, s1, s2])


def _rot_att(x, ca, s1, s2):
    half = ROPE_DIM // 2
    return x * ca + pltpu.roll(x, LANES - half, 1) * s1 + pltpu.roll(x, half, 1) * s2


def _proj_kernel(x_ref, sc_ref, sh_ref, tab_ref, w_ref, wkv_ref, o_ref, kv_ref, h_ref):
    n = pl.program_id(2)
    tm = x_ref.shape[0]
    n_split = max(tm // PROJ_SPLIT_ROWS, 1)
    row_blocks = [slice(r * (tm // n_split), (r + 1) * (tm // n_split)) for r in range(n_split)]

    def mod_rows(ref, rows):
        return ref[...] if ref.shape[0] == 1 else ref[rows, :]

    def att_tabs(rows):
        return tab_ref[2, rows, :], tab_ref[3, rows, :], tab_ref[4, rows, :]

    def modulate_and_kv(rows):
        h = x_ref[rows, :] * (1.0 + mod_rows(sc_ref, rows)) + mod_rows(sh_ref, rows)
        h_ref[rows, :] = h.astype(BF16)
        kv = jnp.dot(h_ref[rows, :], wkv_ref[...], preferred_element_type=F32)
        ca, s1, s2 = att_tabs(rows)
        for s in range(KV_W // LANES):
            xs = kv[:, s * LANES:(s + 1) * LANES]
            if s < KV_W // LANES // 2:
                xs = _rot_att(xs, ca, s1, s2)
            kv_ref[rows, s * LANES:(s + 1) * LANES] = xs.astype(BF16)

    def matmul(rows):
        return jnp.dot(h_ref[rows, :], w_ref[...], preferred_element_type=F32)

    def rot_ret_rows(rows, scale):
        acc = matmul(rows)
        cr, sr = tab_ref[0, rows, :], tab_ref[1, rows, :]
        for j in range(PROJ_TN // RET_DIM):
            a = j * RET_DIM
            x1 = acc[:, a:a + LANES]
            x2 = acc[:, a + LANES:a + 2 * LANES]
            o_ref[rows, a:a + LANES] = ((x1 * cr - x2 * sr) * scale).astype(BF16)
            o_ref[rows, a + LANES:a + 2 * LANES] = ((x1 * sr + x2 * cr) * scale).astype(BF16)

    def rot_ret(scale):
        for rows in row_blocks:
            rot_ret_rows(rows, scale)

    region = n // (D_MODEL // PROJ_TN)

    @pl.when(n == 0)
    def _():
        for rows in row_blocks:
            modulate_and_kv(rows)
            rot_ret_rows(rows, 1.0)

    @pl.when((region == 0) & (n > 0))
    def _():
        rot_ret(1.0)

    @pl.when(region == 1)
    def _():
        rot_ret(RET_DIM ** -0.5)

    @pl.when(region == 2)
    def _():
        for rows in row_blocks:
            o_ref[rows, :] = matmul(rows).astype(BF16)

    @pl.when(region == 3)
    def _():
        for rows in row_blocks:
            acc = matmul(rows)
            o_ref[rows, :] = (acc * jax.nn.sigmoid(acc)).astype(BF16)

    @pl.when(region == 4)
    def _():
        for rows in row_blocks:
            acc = matmul(rows)
            ca, s1, s2 = att_tabs(rows)
            for s in range(PROJ_TN // LANES):
                xs = _rot_att(acc[:, s * LANES:(s + 1) * LANES], ca, s1, s2)
                o_ref[rows, s * LANES:(s + 1) * LANES] = (xs * ATT_HEAD_DIM ** -0.5).astype(BF16)

    @pl.when(region >= 5)
    def _():
        for rows in row_blocks:
            o_ref[rows, :] = jax.nn.sigmoid(matmul(rows)).astype(BF16)


def _mod_spec(mod, tm):
    if mod.shape[1] == 1:
        return pl.BlockSpec((None, 1, D_MODEL), lambda b, t, *_: (b, 0, 0))
    return pl.BlockSpec((None, tm, D_MODEL), lambda b, t, *_: (b, t, 0))


def _proj(x, sc, sh, tab, w_main, w_kv, tm):
    B, T, _ = x.shape
    n_tiles = MAIN_W // PROJ_TN
    return pl.pallas_call(
        _proj_kernel,
        grid=(B, T // tm, n_tiles),
        in_specs=[pl.BlockSpec((None, tm, D_MODEL), lambda b, t, n: (b, t, 0)),
                  _mod_spec(sc, tm), _mod_spec(sh, tm),
                  pl.BlockSpec((5, tm, LANES), lambda b, t, n: (0, t, 0)),
                  pl.BlockSpec((D_MODEL, PROJ_TN), lambda b, t, n: (0, n)),
                  pl.BlockSpec((D_MODEL, KV_W), lambda b, t, n: (0, 0), pipeline_mode=pl.Buffered(1))],
        out_specs=[pl.BlockSpec((None, tm, PROJ_TN), lambda b, t, n: (b, t, n)),
                   pl.BlockSpec((None, tm, KV_W), lambda b, t, n: (b, t, 0))],
        out_shape=[jax.ShapeDtypeStruct((B, T, MAIN_W), BF16),
                   jax.ShapeDtypeStruct((B, T, KV_W), BF16)],
        scratch_shapes=[pltpu.VMEM((tm, D_MODEL), BF16)],
        compiler_params=_params(("arbitrary", "arbitrary", "arbitrary")),
        name="proj",
    )(x, sc, sh, tab, w_main, w_kv)


def _repack_kernel(w_ref, o_ref):
    o_ref[...] = w_ref[...].astype(BF16)


def _repack_w_in(w):
    kv_blk = 5 * D_MODEL // KV_W
    return pl.pallas_call(
        _repack_kernel,
        grid=(MAIN_W // KV_W,),
        in_specs=[pl.BlockSpec((D_MODEL, KV_W), lambda n: (0, jnp.where(n >= kv_blk, n + 1, n)))],
        out_specs=pl.BlockSpec((D_MODEL, KV_W), lambda n: (0, n)),
        out_shape=jax.ShapeDtypeStruct((D_MODEL, MAIN_W), BF16),
        compiler_params=_params(("arbitrary",)),
        name="repack_w_in",
    )(w)


def _kv32_kernel(x_ref, sc_ref, sh_ref, tab_ref, wkv_ref, o_ref):
    h = (x_ref[...] * (1.0 + sc_ref[...]) + sh_ref[...]).astype(BF16)
    kv = jnp.dot(h, wkv_ref[...], preferred_element_type=F32)
    ca, s1, s2 = tab_ref[2], tab_ref[3], tab_ref[4]
    for s in range(KV_W // LANES):
        xs = kv[:, s * LANES:(s + 1) * LANES]
        if s < KV_W // LANES // 2:
            xs = _rot_att(xs, ca, s1, s2)
        o_ref[:, s * LANES:(s + 1) * LANES] = xs


def _kv32(x, sc, sh, tab, w_kv):
    B, R, _ = x.shape
    return pl.pallas_call(
        _kv32_kernel,
        grid=(B, 1),
        in_specs=[pl.BlockSpec((None, R, D_MODEL), lambda b, t: (b, 0, 0)),
                  _mod_spec(sc, R), _mod_spec(sh, R),
                  pl.BlockSpec((5, R, LANES), lambda b, t: (0, 0, 0)),
                  pl.BlockSpec((D_MODEL, KV_W), lambda b, t: (0, 0))],
        out_specs=pl.BlockSpec((None, R, KV_W), lambda b, t: (b, 0, 0)),
        out_shape=jax.ShapeDtypeStruct((B, R, KV_W), F32),
        compiler_params=_params(("arbitrary", "arbitrary")),
        name="kv32",
    )(x, sc, sh, tab, w_kv)


def _ret_kernel(lg_ref, q_ref, k_ref, v_ref, gs_ref, sg_ref, gn_ref, s0_ref, o_ref, sout_ref,
                s_ref, dm_ref, qd_ref, kd_ref, *, chunk, n_chunks, heads):
    hg = pl.program_id(1)
    t = pl.program_id(2)
    nt = (((1,), (1,)), ((), ()))
    tn = (((0,), (0,)), ((), ()))
    for hl in range(heads):
        lg = lg_ref[hg * heads + hl]
        cols = slice(hl * RET_DIM, (hl + 1) * RET_DIM)

        @pl.when(t == 0)
        def _():
            s_ref[hl] = s0_ref[hl]
            i = lax.broadcasted_iota(jnp.int32, (chunk, chunk), 0)
            j = lax.broadcasted_iota(jnp.int32, (chunk, chunk), 1)
            diff = (i - j).astype(F32)
            dm_ref[hl] = jnp.where(diff >= 0, jnp.exp(lg * jnp.maximum(diff, 0.0)), 0.0)
            r = lax.broadcasted_iota(jnp.int32, (chunk, RET_DIM), 0).astype(F32)
            qd_ref[hl] = jnp.exp(lg * (r + 1.0))
            kd_ref[hl] = jnp.exp(lg * (chunk - 1.0 - r))

        g_chunk = jnp.exp(jnp.full((1, RET_DIM), lg * chunk, F32))
        for c in range(n_chunks):
            rows = pl.ds(c * chunk, chunk)
            q, k, v = q_ref[rows, cols], k_ref[rows, cols], v_ref[rows, cols]
            inner = (lax.dot_general(q, k, nt, preferred_element_type=F32) * dm_ref[hl]).astype(BF16)
            s_prev = s_ref[hl]
            if chunk % LANES == 0:
                q_dec = (q.astype(F32) * qd_ref[hl]).astype(BF16)
                o = jnp.dot(jnp.concatenate([inner, q_dec], axis=1),
                            jnp.concatenate([v, s_prev.astype(BF16)], axis=0), preferred_element_type=F32)
            else:
                o = jnp.dot(inner, v, preferred_element_type=F32)
                o = o + jnp.dot(q, s_prev.astype(BF16), preferred_element_type=F32) * qd_ref[hl]
            kdec = (k.astype(F32) * kd_ref[hl]).astype(BF16)
            s_ref[hl] = s_prev * g_chunk + lax.dot_general(kdec, v, tn, preferred_element_type=F32)
            mu = jnp.mean(o, axis=-1, keepdims=True)
            xc = o - mu
            var = jnp.mean(xc * xc, axis=-1, keepdims=True)
            gn = xc * lax.rsqrt(var + LN_EPS) * gn_ref[:, cols]
            out = sg_ref[rows, cols].astype(F32) * (gs_ref[rows, cols].astype(F32) * gn)
            o_ref[rows, cols] = out.astype(BF16)

    @pl.when(t == pl.num_programs(2) - 1)
    def _():
        sout_ref[...] = s_ref[...]


def _retention(proj, gn_g, s0, chunk, tt, heads):
    B, T, _ = proj.shape
    n_chunks = tt // chunk
    width = heads * RET_DIM
    log_gamma = jnp.log1p(-jnp.exp2(-5.0 - jnp.arange(RET_HEADS, dtype=F32)))

    def col(off):
        base = off // width
        return pl.BlockSpec((None, tt, width), lambda b, h, t: (b, t, base + h))

    state = pl.BlockSpec((None, heads, RET_DIM, RET_DIM), lambda b, h, t: (b, h, 0, 0))
    return pl.pallas_call(
        functools.partial(_ret_kernel, chunk=chunk, n_chunks=n_chunks, heads=heads),
        grid=(B, RET_HEADS // heads, T // tt),
        in_specs=[pl.BlockSpec(memory_space=pltpu.SMEM),
                  col(COL_QR), col(COL_KR), col(COL_VR), col(COL_GR), col(COL_SGR),
                  pl.BlockSpec((1, width), lambda b, h, t: (0, h)), state],
        out_specs=[pl.BlockSpec((None, tt, width), lambda b, h, t: (b, t, h)), state],
        out_shape=[jax.ShapeDtypeStruct((B, T, D_MODEL), BF16),
                   jax.ShapeDtypeStruct((B, RET_HEADS, RET_DIM, RET_DIM), F32)],
        scratch_shapes=[pltpu.VMEM((heads, RET_DIM, RET_DIM), F32),
                        pltpu.VMEM((heads, chunk, chunk), F32),
                        pltpu.VMEM((heads, chunk, RET_DIM), F32),
                        pltpu.VMEM((heads, chunk, RET_DIM), F32)],
        compiler_params=_params(("arbitrary", "arbitrary", "arbitrary")),
        name="retention",
    )(log_gamma, proj, proj, proj, proj, proj, gn_g.reshape(1, D_MODEL), s0)


def _att_kernel(sink_ref, q_ref, halo_ref, cur_ref, sg_ref, o_ref, *, cq, n_chunks, own_valid, mask_first):
    i = pl.program_id(1)
    kv = jnp.concatenate([halo_ref[...], cur_ref[...]], axis=0).astype(F32)
    n_keys = kv.shape[0]
    n_kc = n_keys // CHUNK
    n_tiles = (n_kc + 1) // 2
    lane = lax.broadcasted_iota(jnp.int32, (1, LANES), 1)
    lo = lane < ATT_HEAD_DIM
    ones_lo = jnp.broadcast_to(jnp.where(lo, 1.0, 0.0), (n_keys, LANES)).astype(BF16)
    ones_hi = jnp.broadcast_to(jnp.where(lo, 0.0, 1.0), (n_keys, LANES)).astype(BF16)
    own_ok = (lane % ATT_HEAD_DIM) < own_valid
    nt = (((1,), (1,)), ((), ()))
    pairs = ATT_GROUP // 2
    rows_q = pairs * cq
    k_off = ATT_KV_HEADS * ATT_HEAD_DIM
    users = [[c for c in range(n_chunks) if c <= 2 * u + 1 and c + 2 >= 2 * u] for u in range(n_tiles)]

    def tile_rows(a_lo, a_hi, u):
        parts = []
        for j in (2 * u, 2 * u + 1):
            for a in (a_lo, a_hi):
                parts.append(a[j * CHUNK:(j + 1) * CHUNK] if j < n_kc else jnp.zeros((CHUNK, a.shape[1]), a.dtype))
        return jnp.concatenate(parts, axis=0)

    for y in range(ATT_KV_HEADS // 2):
        ks = kv[:, y * LANES:(y + 1) * LANES]
        kr = pltpu.roll(ks, ATT_HEAD_DIM, 1)
        vs = kv[:, k_off + y * LANES:k_off + (y + 1) * LANES]
        vr = pltpu.roll(vs, ATT_HEAD_DIM, 1)
        for par in range(2):
            x = 2 * y + par
            k_even, k_odd = (ks, kr) if par == 0 else (kr, ks)
            v_even, v_odd = (vs, vr) if par == 0 else (vr, vs)
            klo = jnp.where(lo, k_even, 0.0).astype(BF16)
            khi = jnp.where(lo, 0.0, k_odd).astype(BF16)
            vlo = jnp.concatenate([jnp.where(lo, v_even, 0.0).astype(BF16), ones_lo], axis=1)
            vhi = jnp.concatenate([jnp.where(lo, 0.0, v_odd).astype(BF16), ones_hi], axis=1)
            sink_b = jnp.concatenate(
                [jnp.broadcast_to(jnp.where(lo, sink_ref[x * ATT_GROUP + 2 * p],
                                            sink_ref[x * ATT_GROUP + 2 * p + 1]), (cq, LANES))
                 for p in range(pairs)], axis=0)
            col0 = x * ATT_GROUP * ATT_HEAD_DIM
            qp = [jnp.concatenate([q_ref[c * cq:(c + 1) * cq, col0 + p * LANES:col0 + (p + 1) * LANES]
                                   for p in range(pairs)], axis=0) for c in range(n_chunks)]
            s_tiles = [lax.dot_general(jnp.concatenate([qp[c] for c in users[u]], axis=0), tile_rows(klo, khi, u),
                                       nt, preferred_element_type=F32) for u in range(n_tiles)]
            p_blocks, m_all = {}, {}
            for c in range(n_chunks):
                sj = []
                for jj in range(3):
                    u, half = (c + jj) // 2, (c + jj) % 2
                    r = users[u].index(c) * rows_q
                    sj.append(s_tiles[u][r:r + rows_q, half * LANES:(half + 1) * LANES])
                if mask_first:
                    for jj in range(2):
                        if c + jj < 2:
                            sj[jj] = jnp.where(i * n_chunks + (c + jj - 2) < 0, NEG_INF, sj[jj])
                if own_valid < CHUNK:
                    sj[2] = jnp.where(own_ok, sj[2], NEG_INF)
                m3 = jnp.maximum(jnp.maximum(sj[0], sj[1]), sj[2])
                m_even = jnp.max(jnp.where(lo, m3, NEG_INF), axis=1, keepdims=True)
                m_odd = jnp.max(jnp.where(lo, NEG_INF, m3), axis=1, keepdims=True)
                m_b = jnp.maximum(jnp.where(lo, m_even, m_odd), sink_b)
                m_all[c] = m_b
                for jj in range(3):
                    p_blocks[(c, c + jj)] = jnp.exp(sj[jj] - m_b).astype(BF16)
            zero_p = jnp.zeros((rows_q, LANES), BF16)
            o_tiles = []
            for u in range(n_tiles):
                p_u = jnp.concatenate(
                    [jnp.concatenate([p_blocks.get((c, j), zero_p) for j in (2 * u, 2 * u + 1)], axis=1)
                     for c in users[u]], axis=0)
                o_tiles.append(jnp.dot(p_u, tile_rows(vlo, vhi, u), preferred_element_type=F32))
            for c in range(n_chunks):
                oe = None
                for u in sorted({c // 2, (c + 2) // 2}):
                    r = users[u].index(c) * rows_q
                    part = o_tiles[u][r:r + rows_q]
                    oe = part if oe is None else oe + part
                den = oe[:, LANES:] + jnp.exp(sink_b - m_all[c])
                res = oe[:, :LANES] / den
                for p in range(pairs):
                    cols = slice(col0 + p * LANES, col0 + (p + 1) * LANES)
                    rows = slice(c * cq, (c + 1) * cq)
                    gate = sg_ref[rows, cols].astype(F32)
                    o_ref[rows, cols] = (gate * res[p * cq:(p + 1) * cq]).astype(BF16)


def _attention(proj, halo_src, cur_src, sinks, cq, n_chunks, own_valid, mask_first):
    B, T, _ = proj.shape
    tq = cq * n_chunks
    cur_rows = CHUNK * n_chunks
    q_blk, sg_blk = COL_QA // D_MODEL, COL_SGA // D_MODEL
    halo_per_cur = cur_rows // WINDOW if mask_first else 0

    def halo_map(b, i):
        return (b, jnp.maximum(i * halo_per_cur - 1, 0), 0) if mask_first else (b, 0, 0)

    return pl.pallas_call(
        functools.partial(_att_kernel, cq=cq, n_chunks=n_chunks, own_valid=own_valid, mask_first=mask_first),
        grid=(B, T // tq),
        in_specs=[pl.BlockSpec(memory_space=pltpu.SMEM),
                  pl.BlockSpec((None, tq, D_MODEL), lambda b, i: (b, i, q_blk)),
                  pl.BlockSpec((None, WINDOW, KV_W), halo_map),
                  pl.BlockSpec((None, cur_rows, KV_W), lambda b, i: (b, i, 0)),
                  pl.BlockSpec((None, tq, D_MODEL), lambda b, i: (b, i, sg_blk))],
        out_specs=pl.BlockSpec((None, tq, D_MODEL), lambda b, i: (b, i, 0)),
        out_shape=jax.ShapeDtypeStruct((B, T, D_MODEL), BF16),
        compiler_params=_params(("arbitrary", "arbitrary")),
        name="attention",
    )(sinks, proj, halo_src, cur_src, proj)


def _layernorm(z, g, b):
    mu = jnp.mean(z, axis=-1, keepdims=True)
    zc = z - mu
    var = jnp.mean(zc * zc, axis=-1, keepdims=True)
    return zc * lax.rsqrt(var + LN_EPS) * g + b


def _out_kernel(r_ref, a_ref, x_ref, gt_ref, sc_ref, sh_ref, w_ref, g_ref, b_ref, x1_ref, h2_ref):
    n_split = max(x_ref.shape[0] // OUT_SPLIT_ROWS, 1)
    rows_per = x_ref.shape[0] // n_split

    def mod_rows(ref, rows):
        return ref[...] if ref.shape[0] == 1 else ref[rows, :]

    for r in range(n_split):
        rows = slice(r * rows_per, (r + 1) * rows_per)
        merged = r_ref[rows, :] + a_ref[rows, :]
        mix = jnp.dot(merged, w_ref[...], preferred_element_type=F32)
        z = ALPHA * x_ref[rows, :] + mod_rows(gt_ref, rows) * mix
        x1 = _layernorm(z, g_ref[...], b_ref[...])
        x1_ref[rows, :] = x1
        h2_ref[rows, :] = (x1 * (1.0 + mod_rows(sc_ref, rows)) + mod_rows(sh_ref, rows)).astype(BF16)


def _out_proj(ret_m, att_m, x, gt, sc, sh, w_o, g, b, tm):
    B, T, _ = x.shape
    tok = pl.BlockSpec((None, tm, D_MODEL), lambda bb, t: (bb, t, 0))
    vec = pl.BlockSpec((1, D_MODEL), lambda bb, t: (0, 0))
    return pl.pallas_call(
        _out_kernel,
        grid=(B, T // tm),
        in_specs=[tok, tok, tok, _mod_spec(gt, tm), _mod_spec(sc, tm), _mod_spec(sh, tm),
                  pl.BlockSpec((D_MODEL, D_MODEL), lambda bb, t: (0, 0)), vec, vec],
        out_specs=[tok, tok],
        out_shape=[jax.ShapeDtypeStruct((B, T, D_MODEL), F32),
                   jax.ShapeDtypeStruct((B, T, D_MODEL), BF16)],
        compiler_params=_params(("arbitrary", "arbitrary")),
        name="out_proj",
    )(ret_m, att_m, x, gt, sc, sh, w_o, g.reshape(1, D_MODEL), b.reshape(1, D_MODEL))


def _ffn_kernel(h_ref, x1_ref, gt_ref, wg_ref, wu_ref, wd_ref, g_ref, b_ref, o_ref):
    f = pl.program_id(2)
    last = pl.num_programs(2) - 1
    tm = h_ref.shape[0]

    def row_blocks(rows_per):
        n_split = max(tm // rows_per, 1)
        return [slice(r * (tm // n_split), (r + 1) * (tm // n_split)) for r in range(n_split)]

    @pl.when(f == 0)
    def _():
        o_ref[...] = jnp.zeros_like(o_ref)

    def accumulate(rows, finalize):
        h = h_ref[rows, :]
        a = jnp.dot(h, wg_ref[...], preferred_element_type=F32)
        u = jnp.dot(h, wu_ref[...], preferred_element_type=F32)
        act = (a * jax.nn.sigmoid(a) * u).astype(BF16)
        acc = o_ref[rows, :] + jnp.dot(act, wd_ref[...], preferred_element_type=F32)
        if finalize:
            gt = gt_ref[...] if gt_ref.shape[0] == 1 else gt_ref[rows, :]
            acc = _layernorm(ALPHA * x1_ref[rows, :] + gt * acc, g_ref[...], b_ref[...])
        o_ref[rows, :] = acc

    @pl.when(f < last)
    def _():
        for rows in row_blocks(FFN_SPLIT_ROWS):
            accumulate(rows, False)

    @pl.when(f == last)
    def _():
        for rows in row_blocks(FFN_LAST_SPLIT_ROWS):
            accumulate(rows, True)


def _ffn(h2, x1, gt, wg, wu, wd, g, b, tm, tf):
    B, T, _ = x1.shape
    tok = pl.BlockSpec((None, tm, D_MODEL), lambda bb, t, f: (bb, t, 0))
    vec = pl.BlockSpec((1, D_MODEL), lambda bb, t, f: (0, 0))
    return pl.pallas_call(
        _ffn_kernel,
        grid=(B, T // tm, D_FF // tf),
        in_specs=[tok, tok, _mod_spec(gt, tm),
                  pl.BlockSpec((D_MODEL, tf), lambda bb, t, f: (0, f)),
                  pl.BlockSpec((D_MODEL, tf), lambda bb, t, f: (0, f)),
                  pl.BlockSpec((tf, D_MODEL), lambda bb, t, f: (f, 0)), vec, vec],
        out_specs=tok,
        out_shape=jax.ShapeDtypeStruct((B, T, D_MODEL), F32),
        compiler_params=_params(("arbitrary", "arbitrary", "arbitrary")),
        name="ffn",
    )(h2, x1, gt, wg, wu, wd, g.reshape(1, D_MODEL), b.reshape(1, D_MODEL))


def _largest_tile(total, cap):
    t = min(total, cap)
    while total % t:
        t //= 2
    return t


def kernel(x_prompt, x_sample, c_prompt, c_sample, cache_attn_k, cache_attn_v, state_ret, w_ada, b_ada, w_in,
           gn_g, attn_sinks, w_o, ln1_g, ln1_b, w_ffn_gate, w_ffn_up, w_ffn_down, ln2_g, ln2_b):
    B, T, _ = x_prompt.shape
    Bs, Ls, _ = x_sample.shape
    l = 0

    n_c = B + Bs
    pad = (-n_c) % 8
    c_all = jnp.concatenate([c_prompt, c_sample, jnp.zeros((pad, D_MODEL), F32)], axis=0)
    mods = _mods(c_all, w_ada[l], b_ada[l])
    mods_p = [m[:, None, :] for m in jnp.split(mods[:B], 6, axis=-1)]
    mods_s = [jnp.repeat(m, Ls, axis=0)[None] for m in jnp.split(mods[B:n_c], 6, axis=-1)]

    w = w_in[l]
    o_ka = 5 * D_MODEL
    o_gate = o_ka + KV_W
    w_main = _repack_w_in(w)
    w_kv = w[:, o_ka:o_gate].astype(BF16)
    wo, wg, wu, wd = (a[l].astype(BF16) for a in (w_o, w_ffn_gate, w_ffn_up, w_ffn_down))

    def layer(x, mods6, pos, s0, att_fn, tm_proj, tm_out, tm_ffn, ret_chunk, ret_tt):
        sh_a, sc_a, gt_a, sh_f, sc_f, gt_f = mods6
        tab = _rope_tables(pos)
        proj, kvb = _proj(x, sc_a, sh_a, tab, w_main, w_kv, tm_proj)
        ret_m, s_new = _retention(proj, gn_g[l], s0, ret_chunk, ret_tt, 1)
        att_m = att_fn(proj, kvb)
        x1, h2 = _out_proj(ret_m, att_m, x, gt_a, sc_f, sh_f, wo, ln1_g[l], ln1_b[l], tm_out)
        y = _ffn(h2, x1, gt_f, wg, wu, wd, ln2_g[l], ln2_b[l], tm_ffn, FFN_TF)
        return y, s_new, tab

    pos_p = jnp.arange(T)
    n_chunks_p = _largest_tile(T // CHUNK, 8)
    y_p, s_p, tab_p = layer(
        x_prompt, mods_p, pos_p, jnp.zeros((B, RET_HEADS, RET_DIM, RET_DIM), F32),
        lambda proj, kvb: _attention(proj, kvb, kvb, attn_sinks[l], CHUNK, n_chunks_p, CHUNK, True),
        _largest_tile(T, 1024), _largest_tile(T, 512), _largest_tile(T, 1024),
        _largest_tile(T, 256), _largest_tile(T, 2048))
    kv_p = _kv32(x_prompt[:, T - WINDOW:], mods_p[1], mods_p[0], tab_p[:, T - WINDOW:], w_kv)
    k_p = kv_p[..., :KV_W // 2].reshape(B, WINDOW, ATT_KV_HEADS, ATT_HEAD_DIM)
    v_p = kv_p[..., KV_W // 2:].reshape(B, WINDOW, ATT_KV_HEADS, ATT_HEAD_DIM)

    R = Bs * Ls
    xs = x_sample.reshape(1, R, D_MODEL)
    pos_s = jnp.tile(PAST_LEN + jnp.arange(Ls), Bs)
    cache = jnp.concatenate([cache_attn_k[l].reshape(Bs, WINDOW, KV_W // 2),
                             cache_attn_v[l].reshape(Bs, WINDOW, KV_W // 2)], axis=-1).astype(BF16)

    def att_sample(proj, kvb):
        new = jnp.pad(kvb.reshape(Bs, Ls, KV_W), ((0, 0), (0, CHUNK - Ls), (0, 0)))
        o = _attention(proj.reshape(Bs, Ls, MAIN_W), cache, new, attn_sinks[l], Ls, 1, Ls, False)
        return o.reshape(1, R, D_MODEL)

    def ret_sample(proj, s0):
        return _retention(proj.reshape(Bs, Ls, MAIN_W), gn_g[l], s0, Ls, Ls, RET_HEADS)

    sh_a, sc_a, gt_a, sh_f, sc_f, gt_f = mods_s
    tab_s = _rope_tables(pos_s)
    proj_s, kvb_s = _proj(xs, sc_a, sh_a, tab_s, w_main, w_kv, R)
    ret_s, s_s = ret_sample(proj_s, state_ret[l])
    att_s = att_sample(proj_s, kvb_s)
    x1_s, h2_s = _out_proj(ret_s.reshape(1, R, D_MODEL), att_s, xs, gt_a, sc_f, sh_f, wo, ln1_g[l], ln1_b[l], R)
    y_s = _ffn(h2_s, x1_s, gt_f, wg, wu, wd, ln2_g[l], ln2_b[l], R, FFN_TF).reshape(Bs, Ls, D_MODEL)
    kv_s = _kv32(xs, sc_a, sh_a, tab_s, w_kv).reshape(Bs, Ls, KV_W)
    k_s = kv_s[..., :KV_W // 2].reshape(Bs, Ls, ATT_KV_HEADS, ATT_HEAD_DIM)
    v_s = kv_s[..., KV_W // 2:].reshape(Bs, Ls, ATT_KV_HEADS, ATT_HEAD_DIM)

    return (y_p, y_s, k_p[None], v_p[None], s_p[None], k_s[None], v_s[None], s_s[None])
```

```python
import functools

import jax
import jax.numpy as jnp
from jax import lax
from jax.experimental import pallas as pl
from jax.experimental.pallas import tpu as pltpu

F32 = jnp.float32
BF16 = jnp.bfloat16

D_MODEL = 2048
CHUNK = 64
PAST_LEN = 1024
RET_HEADS = 8
RET_DIM = 256
RET_ROPE_THETA = 10000.0
ATT_HEADS = 32
ATT_KV_HEADS = 4
ATT_HEAD_DIM = 64
ATT_GROUP = ATT_HEADS // ATT_KV_HEADS
WINDOW = 128
ROPE_DIM = ATT_HEAD_DIM // 4
ROPE_THETA = 500000.0
D_FF = 5632
DEPTH = 1
ALPHA = (2.0 * DEPTH) ** 0.25
LN_EPS = 1e-5
NEG_INF = -1e30

LANES = 128
KV_W = 2 * ATT_KV_HEADS * ATT_HEAD_DIM
MAIN_W = 7 * D_MODEL
PROJ_TN = 1024
PROJ_SPLIT_ROWS = 256
CAST_SLAB = 128
FFN_TF = 512
FFN_SPLIT_ROWS = 512
FFN_LAST_SPLIT_ROWS = 256
OUT_SPLIT_ROWS = 256
VMEM_LIMIT = 60 * 1024 * 1024

COL_QR, COL_KR, COL_VR, COL_GR, COL_QA, COL_SGR, COL_SGA = (i * D_MODEL for i in range(7))


def _params(sem):
    return pltpu.CompilerParams(dimension_semantics=sem, vmem_limit_bytes=VMEM_LIMIT)


def _mods_kernel(c_ref, w_ref, b_ref, o_ref):
    c = c_ref[...]
    a = (c * jax.nn.sigmoid(c)).astype(BF16)
    o_ref[...] = jnp.dot(a, w_ref[...].astype(BF16), preferred_element_type=F32) + b_ref[...]


def _mods(c_all, w_ada, b_ada):
    rows = c_all.shape[0]
    n_out = w_ada.shape[1]
    tn = 1024
    return pl.pallas_call(
        _mods_kernel,
        grid=(n_out // tn,),
        in_specs=[pl.BlockSpec((rows, D_MODEL), lambda n: (0, 0)),
                  pl.BlockSpec((D_MODEL, tn), lambda n: (0, n)),
                  pl.BlockSpec((1, tn), lambda n: (0, n))],
        out_specs=pl.BlockSpec((rows, tn), lambda n: (0, n)),
        out_shape=jax.ShapeDtypeStruct((rows, n_out), F32),
        compiler_params=_params(("arbitrary",)),
        name="mods",
    )(c_all, w_ada, b_ada.reshape(1, n_out))


def _rope_tables(pos):
    posf = pos.astype(F32)[:, None]
    half_r = RET_DIM // 2
    inv_r = 1.0 / (RET_ROPE_THETA ** (jnp.arange(half_r, dtype=F32) / half_r))
    ang_r = posf * inv_r[None, :]
    half_a = ROPE_DIM // 2
    inv_a = 1.0 / (ROPE_THETA ** (jnp.arange(half_a, dtype=F32) / half_a))
    d = jnp.arange(LANES) % ATT_HEAD_DIM
    ang_a = posf * inv_a[None, :]
    cos_a = jnp.tile(jnp.cos(ang_a), (1, LANES // half_a))
    sin_a = jnp.tile(jnp.sin(ang_a), (1, LANES // half_a))
    ca = jnp.where(d[None, :] < ROPE_DIM, cos_a, 1.0)
    s1 = jnp.where(d[None, :] < half_a, -sin_a, 0.0)
    s2 = jnp.where((d[None, :] >= half_a) & (d[None, :] < ROPE_DIM), sin_a, 0.0)
    return jnp.stack([jnp.cos(ang_r), jnp.sin(ang_r), ca, s1, s2])


def _rot_att(x, ca, s1, s2):
    half = ROPE_DIM // 2
    return x * ca + pltpu.roll(x, LANES - half, 1) * s1 + pltpu.roll(x, half, 1) * s2


def _proj_kernel(*refs, cast_plan):
    n_cast = len(cast_plan)
    x_ref, sc_ref, sh_ref, tab_ref, w_ref, wkv_ref = refs[:6]
    cast_in = refs[6:6 + n_cast]
    o_ref, kv_ref = refs[6 + n_cast:8 + n_cast]
    cast_out = refs[8 + n_cast:8 + 2 * n_cast]
    h_ref = refs[-1]
    n = pl.program_id(2)
    tm = x_ref.shape[0]

    step = (pl.program_id(0) * pl.num_programs(1) + pl.program_id(1)) * pl.num_programs(2) + n
    for (first, count), src_ref, dst_ref in zip(cast_plan, cast_in, cast_out):
        @pl.when((step >= first) & (step < first + count))
        def _():
            dst_ref[...] = src_ref[...].astype(BF16)

    n_split = max(tm // PROJ_SPLIT_ROWS, 1)
    row_blocks = [slice(r * (tm // n_split), (r + 1) * (tm // n_split)) for r in range(n_split)]

    def mod_rows(ref, rows):
        return ref[...] if ref.shape[0] == 1 else ref[rows, :]

    def att_tabs(rows):
        return tab_ref[2, rows, :], tab_ref[3, rows, :], tab_ref[4, rows, :]

    def modulate_and_kv(rows):
        h = x_ref[rows, :] * (1.0 + mod_rows(sc_ref, rows)) + mod_rows(sh_ref, rows)
        h_ref[rows, :] = h.astype(BF16)
        kv = jnp.dot(h_ref[rows, :], wkv_ref[...], preferred_element_type=F32)
        ca, s1, s2 = att_tabs(rows)
        for s in range(KV_W // LANES):
            xs = kv[:, s * LANES:(s + 1) * LANES]
            if s < KV_W // LANES // 2:
                xs = _rot_att(xs, ca, s1, s2)
            kv_ref[rows, s * LANES:(s + 1) * LANES] = xs.astype(BF16)

    def matmul(rows):
        return jnp.dot(h_ref[rows, :], w_ref[...], preferred_element_type=F32)

    def rot_ret_rows(rows, scale):
        acc = matmul(rows)
        cr, sr = tab_ref[0, rows, :], tab_ref[1, rows, :]
        for j in range(PROJ_TN // RET_DIM):
            a = j * RET_DIM
            x1 = acc[:, a:a + LANES]
            x2 = acc[:, a + LANES:a + 2 * LANES]
            o_ref[rows, a:a + LANES] = ((x1 * cr - x2 * sr) * scale).astype(BF16)
            o_ref[rows, a + LANES:a + 2 * LANES] = ((x1 * sr + x2 * cr) * scale).astype(BF16)

    def rot_ret(scale):
        for rows in row_blocks:
            rot_ret_rows(rows, scale)

    region = n // (D_MODEL // PROJ_TN)

    @pl.when(n == 0)
    def _():
        for rows in row_blocks:
            modulate_and_kv(rows)
            rot_ret_rows(rows, 1.0)

    @pl.when(((region == 0) & (n > 0)) | (region == 1))
    def _():
        rot_ret(jnp.where(region == 0, 1.0, RET_DIM ** -0.5))

    @pl.when(region == 2)
    def _():
        for rows in row_blocks:
            o_ref[rows, :] = matmul(rows).astype(BF16)

    @pl.when(region == 3)
    def _():
        for rows in row_blocks:
            acc = matmul(rows)
            o_ref[rows, :] = (acc * jax.nn.sigmoid(acc)).astype(BF16)

    @pl.when(region == 4)
    def _():
        for rows in row_blocks:
            acc = matmul(rows)
            ca, s1, s2 = att_tabs(rows)
            for s in range(PROJ_TN // LANES):
                xs = _rot_att(acc[:, s * LANES:(s + 1) * LANES], ca, s1, s2)
                o_ref[rows, s * LANES:(s + 1) * LANES] = (xs * ATT_HEAD_DIM ** -0.5).astype(BF16)

    @pl.when(region >= 5)
    def _():
        for rows in row_blocks:
            o_ref[rows, :] = jax.nn.sigmoid(matmul(rows)).astype(BF16)


def _mod_spec(mod, tm):
    if mod.shape[1] == 1:
        return pl.BlockSpec((None, 1, D_MODEL), lambda b, t, *_: (b, 0, 0))
    return pl.BlockSpec((None, tm, D_MODEL), lambda b, t, *_: (b, t, 0))


def _proj(x, sc, sh, tab, w_main, w_kv, tm, cast=()):
    B, T, _ = x.shape
    nt = T // tm
    n_tiles = MAIN_W // PROJ_TN
    cast_plan, cast_specs, cast_shapes, first = [], [], [], 0
    for wt, axis in cast:
        count = wt.shape[axis] // CAST_SLAB
        block = tuple(CAST_SLAB if d == axis else wt.shape[d] for d in range(2))

        def slab_map(b, t, n, first=first, count=count, axis=axis):
            slab = jnp.clip((b * nt + t) * n_tiles + n - first, 0, count - 1)
            return (slab, 0) if axis == 0 else (0, slab)

        cast_plan.append((first, count))
        cast_specs.append(pl.BlockSpec(block, slab_map))
        cast_shapes.append(jax.ShapeDtypeStruct(wt.shape, BF16))
        first += count
    outs = pl.pallas_call(
        functools.partial(_proj_kernel, cast_plan=tuple(cast_plan)),
        grid=(B, nt, n_tiles),
        in_specs=[pl.BlockSpec((None, tm, D_MODEL), lambda b, t, n: (b, t, 0)),
                  _mod_spec(sc, tm), _mod_spec(sh, tm),
                  pl.BlockSpec((5, tm, LANES), lambda b, t, n: (0, t, 0)),
                  pl.BlockSpec((D_MODEL, PROJ_TN), lambda b, t, n: (0, n)),
                  pl.BlockSpec((D_MODEL, KV_W), lambda b, t, n: (0, 0), pipeline_mode=pl.Buffered(1))] + cast_specs,
        out_specs=[pl.BlockSpec((None, tm, PROJ_TN), lambda b, t, n: (b, t, n)),
                   pl.BlockSpec((None, tm, KV_W), lambda b, t, n: (b, t, 0))] + cast_specs,
        out_shape=[jax.ShapeDtypeStruct((B, T, MAIN_W), BF16),
                   jax.ShapeDtypeStruct((B, T, KV_W), BF16)] + cast_shapes,
        scratch_shapes=[pltpu.VMEM((tm, D_MODEL), BF16)],
        compiler_params=_params(("arbitrary", "arbitrary", "arbitrary")),
        name="proj",
    )(x, sc, sh, tab, w_main, w_kv, *[wt for wt, _ in cast])
    return outs[0], outs[1], outs[2:]


def _repack_kernel(w_ref, o_ref):
    o_ref[...] = w_ref[...].astype(BF16)


def _repack_w_in(w):
    kv_blk = 5 * D_MODEL // KV_W
    return pl.pallas_call(
        _repack_kernel,
        grid=(MAIN_W // KV_W,),
        in_specs=[pl.BlockSpec((D_MODEL, KV_W), lambda n: (0, jnp.where(n >= kv_blk, n + 1, n)))],
        out_specs=pl.BlockSpec((D_MODEL, KV_W), lambda n: (0, n)),
        out_shape=jax.ShapeDtypeStruct((D_MODEL, MAIN_W), BF16),
        compiler_params=_params(("arbitrary",)),
        name="repack_w_in",
    )(w)


def _kv32_kernel(x_ref, sc_ref, sh_ref, tab_ref, wkv_ref, o_ref):
    h = (x_ref[...] * (1.0 + sc_ref[...]) + sh_ref[...]).astype(BF16)
    kv = jnp.dot(h, wkv_ref[...], preferred_element_type=F32)
    ca, s1, s2 = tab_ref[2], tab_ref[3], tab_ref[4]
    for s in range(KV_W // LANES):
        xs = kv[:, s * LANES:(s + 1) * LANES]
        if s < KV_W // LANES // 2:
            xs = _rot_att(xs, ca, s1, s2)
        o_ref[:, s * LANES:(s + 1) * LANES] = xs


def _kv32(x, sc, sh, tab, w_kv):
    B, R, _ = x.shape
    return pl.pallas_call(
        _kv32_kernel,
        grid=(B, 1),
        in_specs=[pl.BlockSpec((None, R, D_MODEL), lambda b, t: (b, 0, 0)),
                  _mod_spec(sc, R), _mod_spec(sh, R),
                  pl.BlockSpec((5, R, LANES), lambda b, t: (0, 0, 0)),
                  pl.BlockSpec((D_MODEL, KV_W), lambda b, t: (0, 0))],
        out_specs=pl.BlockSpec((None, R, KV_W), lambda b, t: (b, 0, 0)),
        out_shape=jax.ShapeDtypeStruct((B, R, KV_W), F32),
        compiler_params=_params(("arbitrary", "arbitrary")),
        name="kv32",
    )(x, sc, sh, tab, w_kv)


def _ret_kernel(lg_ref, q_ref, k_ref, v_ref, gs_ref, sg_ref, gn_ref, s0_ref, o_ref, sout_ref,
                s_ref, dm_ref, qd_ref, kd_ref, *, chunk, n_chunks, heads):
    hg = pl.program_id(1)
    t = pl.program_id(2)
    nt = (((1,), (1,)), ((), ()))
    tn = (((0,), (0,)), ((), ()))
    for hl in range(heads):
        lg = lg_ref[hg * heads + hl]
        cols = slice(hl * RET_DIM, (hl + 1) * RET_DIM)

        @pl.when(t == 0)
        def _():
            s_ref[hl] = s0_ref[hl]
            i = lax.broadcasted_iota(jnp.int32, (chunk, chunk), 0)
            j = lax.broadcasted_iota(jnp.int32, (chunk, chunk), 1)
            diff = (i - j).astype(F32)
            dm_ref[hl] = jnp.where(diff >= 0, jnp.exp(lg * jnp.maximum(diff, 0.0)), 0.0)
            r = lax.broadcasted_iota(jnp.int32, (chunk, RET_DIM), 0).astype(F32)
            qd_ref[hl] = jnp.exp(lg * (r + 1.0))
            kd_ref[hl] = jnp.exp(lg * (chunk - 1.0 - r))

        g_chunk = jnp.exp(jnp.full((1, RET_DIM), lg * chunk, F32))
        for c in range(n_chunks):
            rows = pl.ds(c * chunk, chunk)
            q, k, v = q_ref[rows, cols], k_ref[rows, cols], v_ref[rows, cols]
            inner = (lax.dot_general(q, k, nt, preferred_element_type=F32) * dm_ref[hl]).astype(BF16)
            s_prev = s_ref[hl]
            if chunk % LANES == 0:
                q_dec = (q.astype(F32) * qd_ref[hl]).astype(BF16)
                o = jnp.dot(jnp.concatenate([inner, q_dec], axis=1),
                            jnp.concatenate([v, s_prev.astype(BF16)], axis=0), preferred_element_type=F32)
            else:
                o = jnp.dot(inner, v, preferred_element_type=F32)
                o = o + jnp.dot(q, s_prev.astype(BF16), preferred_element_type=F32) * qd_ref[hl]
            kdec = (k.astype(F32) * kd_ref[hl]).astype(BF16)
            s_ref[hl] = s_prev * g_chunk + lax.dot_general(kdec, v, tn, preferred_element_type=F32)
            mu = jnp.mean(o, axis=-1, keepdims=True)
            xc = o - mu
            var = jnp.mean(xc * xc, axis=-1, keepdims=True)
            gn = xc * lax.rsqrt(var + LN_EPS) * gn_ref[:, cols]
            out = sg_ref[rows, cols].astype(F32) * (gs_ref[rows, cols].astype(F32) * gn)
            o_ref[rows, cols] = out.astype(BF16)

    @pl.when(t == pl.num_programs(2) - 1)
    def _():
        sout_ref[...] = s_ref[...]


def _retention(proj, gn_g, s0, chunk, tt, heads):
    B, T, _ = proj.shape
    n_chunks = tt // chunk
    width = heads * RET_DIM
    log_gamma = jnp.log1p(-jnp.exp2(-5.0 - jnp.arange(RET_HEADS, dtype=F32)))

    def col(off):
        base = off // width
        return pl.BlockSpec((None, tt, width), lambda b, h, t: (b, t, base + h))

    state = pl.BlockSpec((None, heads, RET_DIM, RET_DIM), lambda b, h, t: (b, h, 0, 0))
    return pl.pallas_call(
        functools.partial(_ret_kernel, chunk=chunk, n_chunks=n_chunks, heads=heads),
        grid=(B, RET_HEADS // heads, T // tt),
        in_specs=[pl.BlockSpec(memory_space=pltpu.SMEM),
                  col(COL_QR), col(COL_KR), col(COL_VR), col(COL_GR), col(COL_SGR),
                  pl.BlockSpec((1, width), lambda b, h, t: (0, h)), state],
        out_specs=[pl.BlockSpec((None, tt, width), lambda b, h, t: (b, t, h)), state],
        out_shape=[jax.ShapeDtypeStruct((B, T, D_MODEL), BF16),
                   jax.ShapeDtypeStruct((B, RET_HEADS, RET_DIM, RET_DIM), F32)],
        scratch_shapes=[pltpu.VMEM((heads, RET_DIM, RET_DIM), F32),
                        pltpu.VMEM((heads, chunk, chunk), F32),
                        pltpu.VMEM((heads, chunk, RET_DIM), F32),
                        pltpu.VMEM((heads, chunk, RET_DIM), F32)],
        compiler_params=_params(("arbitrary", "arbitrary", "arbitrary")),
        name="retention",
    )(log_gamma, proj, proj, proj, proj, proj, gn_g.reshape(1, D_MODEL), s0)


def _att_kernel(sink_ref, q_ref, halo_ref, cur_ref, sg_ref, o_ref, *, cq, n_chunks, own_valid, mask_first):
    i = pl.program_id(1)
    kv = jnp.concatenate([halo_ref[...], cur_ref[...]], axis=0).astype(F32)
    n_keys = kv.shape[0]
    n_kc = n_keys // CHUNK
    n_tiles = (n_kc + 1) // 2
    lane = lax.broadcasted_iota(jnp.int32, (1, LANES), 1)
    lo = lane < ATT_HEAD_DIM
    ones_lo = jnp.broadcast_to(jnp.where(lo, 1.0, 0.0), (n_keys, LANES)).astype(BF16)
    ones_hi = jnp.broadcast_to(jnp.where(lo, 0.0, 1.0), (n_keys, LANES)).astype(BF16)
    own_ok = (lane % ATT_HEAD_DIM) < own_valid
    nt = (((1,), (1,)), ((), ()))
    pairs = ATT_GROUP // 2
    rows_q = pairs * cq
    k_off = ATT_KV_HEADS * ATT_HEAD_DIM
    users = [[c for c in range(n_chunks) if c <= 2 * u + 1 and c + 2 >= 2 * u] for u in range(n_tiles)]

    def tile_rows(a_lo, a_hi, u):
        parts = []
        for j in (2 * u, 2 * u + 1):
            for a in (a_lo, a_hi):
                parts.append(a[j * CHUNK:(j + 1) * CHUNK] if j < n_kc else jnp.zeros((CHUNK, a.shape[1]), a.dtype))
        return jnp.concatenate(parts, axis=0)

    for y in range(ATT_KV_HEADS // 2):
        ks = kv[:, y * LANES:(y + 1) * LANES]
        kr = pltpu.roll(ks, ATT_HEAD_DIM, 1)
        vs = kv[:, k_off + y * LANES:k_off + (y + 1) * LANES]
        vr = pltpu.roll(vs, ATT_HEAD_DIM, 1)
        for par in range(2):
            x = 2 * y + par
            k_even, k_odd = (ks, kr) if par == 0 else (kr, ks)
            v_even, v_odd = (vs, vr) if par == 0 else (vr, vs)
            klo = jnp.where(lo, k_even, 0.0).astype(BF16)
            khi = jnp.where(lo, 0.0, k_odd).astype(BF16)
            vlo = jnp.concatenate([jnp.where(lo, v_even, 0.0).astype(BF16), ones_lo], axis=1)
            vhi = jnp.concatenate([jnp.where(lo, 0.0, v_odd).astype(BF16), ones_hi], axis=1)
            sink_b = jnp.concatenate(
                [jnp.broadcast_to(jnp.where(lo, sink_ref[x * ATT_GROUP + 2 * p],
                                            sink_ref[x * ATT_GROUP + 2 * p + 1]), (cq, LANES))
                 for p in range(pairs)], axis=0)
            col0 = x * ATT_GROUP * ATT_HEAD_DIM
            qp = [jnp.concatenate([q_ref[c * cq:(c + 1) * cq, col0 + p * LANES:col0 + (p + 1) * LANES]
                                   for p in range(pairs)], axis=0) for c in range(n_chunks)]
            s_tiles = [lax.dot_general(jnp.concatenate([qp[c] for c in users[u]], axis=0), tile_rows(klo, khi, u),
                                       nt, preferred_element_type=F32) for u in range(n_tiles)]
            p_blocks, m_all = {}, {}
            for c in range(n_chunks):
                sj = []
                for jj in range(3):
                    u, half = (c + jj) // 2, (c + jj) % 2
                    r = users[u].index(c) * rows_q
                    sj.append(s_tiles[u][r:r + rows_q, half * LANES:(half + 1) * LANES])
                if mask_first:
                    for jj in range(2):
                        if c + jj < 2:
                            sj[jj] = jnp.where(i * n_chunks + (c + jj - 2) < 0, NEG_INF, sj[jj])
                if own_valid < CHUNK:
                    sj[2] = jnp.where(own_ok, sj[2], NEG_INF)
                m3 = jnp.maximum(jnp.maximum(sj[0], sj[1]), sj[2])
                m_even = jnp.max(jnp.where(lo, m3, NEG_INF), axis=1, keepdims=True)
                m_odd = jnp.max(jnp.where(lo, NEG_INF, m3), axis=1, keepdims=True)
                m_b = jnp.maximum(jnp.where(lo, m_even, m_odd), sink_b)
                m_all[c] = m_b
                for jj in range(3):
                    p_blocks[(c, c + jj)] = jnp.exp(sj[jj] - m_b).astype(BF16)
            zero_p = jnp.zeros((rows_q, LANES), BF16)
            o_tiles = []
            for u in range(n_tiles):
                p_u = jnp.concatenate(
                    [jnp.concatenate([p_blocks.get((c, j), zero_p) for j in (2 * u, 2 * u + 1)], axis=1)
                     for c in users[u]], axis=0)
                o_tiles.append(jnp.dot(p_u, tile_rows(vlo, vhi, u), preferred_element_type=F32))
            for c in range(n_chunks):
                oe = None
                for u in sorted({c // 2, (c + 2) // 2}):
                    r = users[u].index(c) * rows_q
                    part = o_tiles[u][r:r + rows_q]
                    oe = part if oe is None else oe + part
                den = oe[:, LANES:] + jnp.exp(sink_b - m_all[c])
                res = oe[:, :LANES] / den
                for p in range(pairs):
                    cols = slice(col0 + p * LANES, col0 + (p + 1) * LANES)
                    rows = slice(c * cq, (c + 1) * cq)
                    gate = sg_ref[rows, cols].astype(F32)
                    o_ref[rows, cols] = (gate * res[p * cq:(p + 1) * cq]).astype(BF16)


def _attention(proj, halo_src, cur_src, sinks, cq, n_chunks, own_valid, mask_first):
    B, T, _ = proj.shape
    tq = cq * n_chunks
    cur_rows = CHUNK * n_chunks
    q_blk, sg_blk = COL_QA // D_MODEL, COL_SGA // D_MODEL
    halo_per_cur = cur_rows // WINDOW if mask_first else 0

    def halo_map(b, i):
        return (b, jnp.maximum(i * halo_per_cur - 1, 0), 0) if mask_first else (b, 0, 0)

    return pl.pallas_call(
        functools.partial(_att_kernel, cq=cq, n_chunks=n_chunks, own_valid=own_valid, mask_first=mask_first),
        grid=(B, T // tq),
        in_specs=[pl.BlockSpec(memory_space=pltpu.SMEM),
                  pl.BlockSpec((None, tq, D_MODEL), lambda b, i: (b, i, q_blk)),
                  pl.BlockSpec((None, WINDOW, KV_W), halo_map),
                  pl.BlockSpec((None, cur_rows, KV_W), lambda b, i: (b, i, 0)),
                  pl.BlockSpec((None, tq, D_MODEL), lambda b, i: (b, i, sg_blk))],
        out_specs=pl.BlockSpec((None, tq, D_MODEL), lambda b, i: (b, i, 0)),
        out_shape=jax.ShapeDtypeStruct((B, T, D_MODEL), BF16),
        compiler_params=_params(("arbitrary", "arbitrary")),
        name="attention",
    )(sinks, proj, halo_src, cur_src, proj)


def _layernorm(z, g, b):
    mu = jnp.mean(z, axis=-1, keepdims=True)
    zc = z - mu
    var = jnp.mean(zc * zc, axis=-1, keepdims=True)
    return zc * lax.rsqrt(var + LN_EPS) * g + b


def _out_kernel(r_ref, a_ref, x_ref, gt_ref, sc_ref, sh_ref, w_ref, g_ref, b_ref, x1_ref, h2_ref):
    n_split = max(x_ref.shape[0] // OUT_SPLIT_ROWS, 1)
    rows_per = x_ref.shape[0] // n_split

    def mod_rows(ref, rows):
        return ref[...] if ref.shape[0] == 1 else ref[rows, :]

    for r in range(n_split):
        rows = slice(r * rows_per, (r + 1) * rows_per)
        merged = r_ref[rows, :] + a_ref[rows, :]
        mix = jnp.dot(merged, w_ref[...], preferred_element_type=F32)
        z = ALPHA * x_ref[rows, :] + mod_rows(gt_ref, rows) * mix
        x1 = _layernorm(z, g_ref[...], b_ref[...])
        x1_ref[rows, :] = x1
        h2_ref[rows, :] = (x1 * (1.0 + mod_rows(sc_ref, rows)) + mod_rows(sh_ref, rows)).astype(BF16)


def _out_proj(ret_m, att_m, x, gt, sc, sh, w_o, g, b, tm):
    B, T, _ = x.shape
    tok = pl.BlockSpec((None, tm, D_MODEL), lambda bb, t: (bb, t, 0))
    vec = pl.BlockSpec((1, D_MODEL), lambda bb, t: (0, 0))
    return pl.pallas_call(
        _out_kernel,
        grid=(B, T // tm),
        in_specs=[tok, tok, tok, _mod_spec(gt, tm), _mod_spec(sc, tm), _mod_spec(sh, tm),
                  pl.BlockSpec((D_MODEL, D_MODEL), lambda bb, t: (0, 0)), vec, vec],
        out_specs=[tok, tok],
        out_shape=[jax.ShapeDtypeStruct((B, T, D_MODEL), F32),
                   jax.ShapeDtypeStruct((B, T, D_MODEL), BF16)],
        compiler_params=_params(("arbitrary", "arbitrary")),
        name="out_proj",
    )(ret_m, att_m, x, gt, sc, sh, w_o, g.reshape(1, D_MODEL), b.reshape(1, D_MODEL))


def _ffn_kernel(h_ref, x1_ref, gt_ref, wg_ref, wu_ref, wd_ref, g_ref, b_ref, o_ref):
    f = pl.program_id(2)
    last = pl.num_programs(2) - 1
    tm = h_ref.shape[0]

    def row_blocks(rows_per):
        n_split = max(tm // rows_per, 1)
        return [slice(r * (tm // n_split), (r + 1) * (tm // n_split)) for r in range(n_split)]

    @pl.when(f == 0)
    def _():
        o_ref[...] = jnp.zeros_like(o_ref)

    def accumulate(rows, finalize):
        h = h_ref[rows, :]
        a = jnp.dot(h, wg_ref[...], preferred_element_type=F32)
        u = jnp.dot(h, wu_ref[...], preferred_element_type=F32)
        act = (a * jax.nn.sigmoid(a) * u).astype(BF16)
        acc = o_ref[rows, :] + jnp.dot(act, wd_ref[...], preferred_element_type=F32)
        if finalize:
            gt = gt_ref[...] if gt_ref.shape[0] == 1 else gt_ref[rows, :]
            acc = _layernorm(ALPHA * x1_ref[rows, :] + gt * acc, g_ref[...], b_ref[...])
        o_ref[rows, :] = acc

    @pl.when(f < last)
    def _():
        for rows in row_blocks(FFN_SPLIT_ROWS):
            accumulate(rows, False)

    @pl.when(f == last)
    def _():
        for rows in row_blocks(FFN_LAST_SPLIT_ROWS):
            accumulate(rows, True)


def _ffn(h2, x1, gt, wg, wu, wd, g, b, tm, tf):
    B, T, _ = x1.shape
    tok = pl.BlockSpec((None, tm, D_MODEL), lambda bb, t, f: (bb, t, 0))
    vec = pl.BlockSpec((1, D_MODEL), lambda bb, t, f: (0, 0))
    return pl.pallas_call(
        _ffn_kernel,
        grid=(B, T // tm, D_FF // tf),
        in_specs=[tok, tok, _mod_spec(gt, tm),
                  pl.BlockSpec((D_MODEL, tf), lambda bb, t, f: (0, f)),
                  pl.BlockSpec((D_MODEL, tf), lambda bb, t, f: (0, f)),
                  pl.BlockSpec((tf, D_MODEL), lambda bb, t, f: (f, 0)), vec, vec],
        out_specs=tok,
        out_shape=jax.ShapeDtypeStruct((B, T, D_MODEL), F32),
        compiler_params=_params(("arbitrary", "arbitrary", "arbitrary")),
        name="ffn",
    )(h2, x1, gt, wg, wu, wd, g.reshape(1, D_MODEL), b.reshape(1, D_MODEL))


def _largest_tile(total, cap):
    t = min(total, cap)
    while total % t:
        t //= 2
    return t


def kernel(x_prompt, x_sample, c_prompt, c_sample, cache_attn_k, cache_attn_v, state_ret, w_ada, b_ada, w_in,
           gn_g, attn_sinks, w_o, ln1_g, ln1_b, w_ffn_gate, w_ffn_up, w_ffn_down, ln2_g, ln2_b):
    B, T, _ = x_prompt.shape
    Bs, Ls, _ = x_sample.shape
    l = 0

    n_c = B + Bs
    pad = (-n_c) % 8
    c_all = jnp.concatenate([c_prompt, c_sample, jnp.zeros((pad, D_MODEL), F32)], axis=0)
    mods = _mods(c_all, w_ada[l], b_ada[l])
    mods_p = [m[:, None, :] for m in jnp.split(mods[:B], 6, axis=-1)]
    mods_s = [jnp.repeat(m, Ls, axis=0)[None] for m in jnp.split(mods[B:n_c], 6, axis=-1)]

    w = w_in[l]
    o_ka = 5 * D_MODEL
    o_gate = o_ka + KV_W
    w_main = _repack_w_in(w)
    w_kv = w[:, o_ka:o_gate].astype(BF16)

    sh_a, sc_a, gt_a, sh_f, sc_f, gt_f = mods_p
    tab_p = _rope_tables(jnp.arange(T))
    tm_proj = _largest_tile(T, 1024)
    later = ((w_ffn_gate[l], 1), (w_ffn_up[l], 1), (w_ffn_down[l], 0), (w_o[l], 1))
    if sum(wt.shape[axis] // CAST_SLAB for wt, axis in later) <= B * (T // tm_proj) * (MAIN_W // PROJ_TN):
        proj, kvb, (wg, wu, wd, wo) = _proj(x_prompt, sc_a, sh_a, tab_p, w_main, w_kv, tm_proj, cast=later)
    else:
        proj, kvb, _ = _proj(x_prompt, sc_a, sh_a, tab_p, w_main, w_kv, tm_proj)
        wg, wu, wd, wo = (wt.astype(BF16) for wt, _ in later)
    ret_m, s_p = _retention(proj, gn_g[l], jnp.zeros((B, RET_HEADS, RET_DIM, RET_DIM), F32),
                            _largest_tile(T, 256), _largest_tile(T, 2048), 1)
    att_m = _attention(proj, kvb, kvb, attn_sinks[l], CHUNK, _largest_tile(T // CHUNK, 8), CHUNK, True)
    x1, h2 = _out_proj(ret_m, att_m, x_prompt, gt_a, sc_f, sh_f, wo, ln1_g[l], ln1_b[l], _largest_tile(T, 512))
    y_p = _ffn(h2, x1, gt_f, wg, wu, wd, ln2_g[l], ln2_b[l], _largest_tile(T, 1024), FFN_TF)
    kv_p = _kv32(x_prompt[:, T - WINDOW:], mods_p[1], mods_p[0], tab_p[:, T - WINDOW:], w_kv)
    k_p = kv_p[..., :KV_W // 2].reshape(B, WINDOW, ATT_KV_HEADS, ATT_HEAD_DIM)
    v_p = kv_p[..., KV_W // 2:].reshape(B, WINDOW, ATT_KV_HEADS, ATT_HEAD_DIM)

    R = Bs * Ls
    xs = x_sample.reshape(1, R, D_MODEL)
    pos_s = jnp.tile(PAST_LEN + jnp.arange(Ls), Bs)
    cache = jnp.concatenate([cache_attn_k[l].reshape(Bs, WINDOW, KV_W // 2),
                             cache_attn_v[l].reshape(Bs, WINDOW, KV_W // 2)], axis=-1).astype(BF16)

    def att_sample(proj, kvb):
        new = jnp.pad(kvb.reshape(Bs, Ls, KV_W), ((0, 0), (0, CHUNK - Ls), (0, 0)))
        o = _attention(proj.reshape(Bs, Ls, MAIN_W), cache, new, attn_sinks[l], Ls, 1, Ls, False)
        return o.reshape(1, R, D_MODEL)

    def ret_sample(proj, s0):
        return _retention(proj.reshape(Bs, Ls, MAIN_W), gn_g[l], s0, Ls, Ls, RET_HEADS)

    sh_a, sc_a, gt_a, sh_f, sc_f, gt_f = mods_s
    tab_s = _rope_tables(pos_s)
    proj_s, kvb_s, _ = _proj(xs, sc_a, sh_a, tab_s, w_main, w_kv, R)
    ret_s, s_s = ret_sample(proj_s, state_ret[l])
    att_s = att_sample(proj_s, kvb_s)
    x1_s, h2_s = _out_proj(ret_s.reshape(1, R, D_MODEL), att_s, xs, gt_a, sc_f, sh_f, wo, ln1_g[l], ln1_b[l], R)
    y_s = _ffn(h2_s, x1_s, gt_f, wg, wu, wd, ln2_g[l], ln2_b[l], R, FFN_TF).reshape(Bs, Ls, D_MODEL)
    kv_s = _kv32(xs, sc_a, sh_a, tab_s, w_kv).reshape(Bs, Ls, KV_W)
    k_s = kv_s[..., :KV_W // 2].reshape(Bs, Ls, ATT_KV_HEADS, ATT_HEAD_DIM)
    v_s = kv_s[..., KV_W // 2:].reshape(Bs, Ls, ATT_KV_HEADS, ATT_HEAD_DIM)

    return (y_p, y_s, k_p[None], v_p[None], s_p[None], k_s[None], v_s[None], s_s[None])
```

```python
import functools

import jax
import jax.numpy as jnp
from jax import lax
from jax.experimental import pallas as pl
from jax.experimental.pallas import tpu as pltpu

F32 = jnp.float32
BF16 = jnp.bfloat16

D_MODEL = 2048
CHUNK = 64
PAST_LEN = 1024
RET_HEADS = 8
RET_DIM = 256
RET_ROPE_THETA = 10000.0
ATT_HEADS = 32
ATT_KV_HEADS = 4
ATT_HEAD_DIM = 64
ATT_GROUP = ATT_HEADS // ATT_KV_HEADS
WINDOW = 128
ROPE_DIM = ATT_HEAD_DIM // 4
ROPE_THETA = 500000.0
D_FF = 5632
DEPTH = 1
ALPHA = (2.0 * DEPTH) ** 0.25
LN_EPS = 1e-5
NEG_INF = -1e30

LANES = 128
KV_W = 2 * ATT_KV_HEADS * ATT_HEAD_DIM
MAIN_W = 7 * D_MODEL
PROJ_TN = 1024
PROJ_SPLIT_ROWS = 256
PROJ_LIGHT_SPLIT_ROWS = 512
CAST_SLAB = 128
FFN_TF = 512
FFN_TF_SAMPLE = 1408
FFN_SPLIT_ROWS = 512
FFN_LAST_SPLIT_ROWS = 256
OUT_SPLIT_ROWS = 256
VMEM_LIMIT = 60 * 1024 * 1024

COL_QR, COL_KR, COL_VR, COL_GR, COL_QA, COL_SGR, COL_SGA = (i * D_MODEL for i in range(7))


def _params(sem):
    return pltpu.CompilerParams(dimension_semantics=sem, vmem_limit_bytes=VMEM_LIMIT)


def _mods_kernel(c_ref, w_ref, b_ref, o_ref):
    c = c_ref[...]
    a = (c * jax.nn.sigmoid(c)).astype(BF16)
    o_ref[...] = jnp.dot(a, w_ref[...].astype(BF16), preferred_element_type=F32) + b_ref[...]


def _mods(c_all, w_ada, b_ada):
    rows = c_all.shape[0]
    n_out = w_ada.shape[1]
    tn = 1024
    return pl.pallas_call(
        _mods_kernel,
        grid=(n_out // tn,),
        in_specs=[pl.BlockSpec((rows, D_MODEL), lambda n: (0, 0)),
                  pl.BlockSpec((D_MODEL, tn), lambda n: (0, n)),
                  pl.BlockSpec((1, tn), lambda n: (0, n))],
        out_specs=pl.BlockSpec((rows, tn), lambda n: (0, n)),
        out_shape=jax.ShapeDtypeStruct((rows, n_out), F32),
        compiler_params=_params(("arbitrary",)),
        name="mods",
    )(c_all, w_ada, b_ada.reshape(1, n_out))


def _rope_tables(pos):
    posf = pos.astype(F32)[:, None]
    half_r = RET_DIM // 2
    inv_r = 1.0 / (RET_ROPE_THETA ** (jnp.arange(half_r, dtype=F32) / half_r))
    ang_r = posf * inv_r[None, :]
    half_a = ROPE_DIM // 2
    inv_a = 1.0 / (ROPE_THETA ** (jnp.arange(half_a, dtype=F32) / half_a))
    d = jnp.arange(LANES) % ATT_HEAD_DIM
    ang_a = posf * inv_a[None, :]
    cos_a = jnp.tile(jnp.cos(ang_a), (1, LANES // half_a))
    sin_a = jnp.tile(jnp.sin(ang_a), (1, LANES // half_a))
    ca = jnp.where(d[None, :] < ROPE_DIM, cos_a, 1.0)
    s1 = jnp.where(d[None, :] < half_a, -sin_a, 0.0)
    s2 = jnp.where((d[None, :] >= half_a) & (d[None, :] < ROPE_DIM), sin_a, 0.0)
    return jnp.stack([jnp.cos(ang_r), jnp.sin(ang_r), ca, s1, s2])


def _rot_att(x, ca, s1, s2):
    half = ROPE_DIM // 2
    return x * ca + pltpu.roll(x, LANES - half, 1) * s1 + pltpu.roll(x, half, 1) * s2


def _proj_kernel(*refs, cast_plan):
    n_cast = len(cast_plan)
    x_ref, sc_ref, sh_ref, tab_ref, w_ref, wkv_ref = refs[:6]
    cast_in = refs[6:6 + n_cast]
    o_ref, kv_ref = refs[6 + n_cast:8 + n_cast]
    cast_out = refs[8 + n_cast:8 + 2 * n_cast]
    h_ref = refs[-1]
    n = pl.program_id(2)
    tm = x_ref.shape[0]

    step = (pl.program_id(0) * pl.num_programs(1) + pl.program_id(1)) * pl.num_programs(2) + n
    for (first, count), src_ref, dst_ref in zip(cast_plan, cast_in, cast_out):
        @pl.when((step >= first) & (step < first + count))
        def _():
            dst_ref[...] = src_ref[...].astype(BF16)

    def blocks(rows_per):
        n_split = max(tm // rows_per, 1)
        return [slice(r * (tm // n_split), (r + 1) * (tm // n_split)) for r in range(n_split)]

    row_blocks = blocks(PROJ_SPLIT_ROWS)
    light_blocks = blocks(PROJ_LIGHT_SPLIT_ROWS)
    tn = w_ref.shape[1]

    def mod_rows(ref, rows):
        return ref[...] if ref.shape[0] == 1 else ref[rows, :]

    def att_tabs(rows):
        return tab_ref[2, rows, :], tab_ref[3, rows, :], tab_ref[4, rows, :]

    def modulate_and_kv(rows):
        h = x_ref[rows, :] * (1.0 + mod_rows(sc_ref, rows)) + mod_rows(sh_ref, rows)
        h_ref[rows, :] = h.astype(BF16)
        kv = jnp.dot(h_ref[rows, :], wkv_ref[...], preferred_element_type=F32)
        ca, s1, s2 = att_tabs(rows)
        for s in range(KV_W // LANES):
            xs = kv[:, s * LANES:(s + 1) * LANES]
            if s < KV_W // LANES // 2:
                xs = _rot_att(xs, ca, s1, s2)
            kv_ref[rows, s * LANES:(s + 1) * LANES] = xs.astype(BF16)

    def matmul(rows):
        return jnp.dot(h_ref[rows, :], w_ref[...], preferred_element_type=F32)

    def rot_ret_rows(rows, scale):
        acc = matmul(rows)
        cr, sr = tab_ref[0, rows, :], tab_ref[1, rows, :]
        for j in range(tn // RET_DIM):
            a = j * RET_DIM
            x1 = acc[:, a:a + LANES]
            x2 = acc[:, a + LANES:a + 2 * LANES]
            o_ref[rows, a:a + LANES] = ((x1 * cr - x2 * sr) * scale).astype(BF16)
            o_ref[rows, a + LANES:a + 2 * LANES] = ((x1 * sr + x2 * cr) * scale).astype(BF16)

    def rot_ret(scale):
        for rows in light_blocks:
            rot_ret_rows(rows, scale)

    region = n // (D_MODEL // tn)

    @pl.when(n == 0)
    def _():
        for rows in row_blocks:
            modulate_and_kv(rows)
            rot_ret_rows(rows, 1.0)

    @pl.when(((region == 0) & (n > 0)) | (region == 1))
    def _():
        rot_ret(jnp.where(region == 0, 1.0, RET_DIM ** -0.5))

    @pl.when(region == 2)
    def _():
        for rows in light_blocks:
            o_ref[rows, :] = matmul(rows).astype(BF16)

    @pl.when(region == 3)
    def _():
        for rows in row_blocks:
            acc = matmul(rows)
            o_ref[rows, :] = (acc * jax.nn.sigmoid(acc)).astype(BF16)

    @pl.when(region == 4)
    def _():
        for rows in row_blocks:
            acc = matmul(rows)
            ca, s1, s2 = att_tabs(rows)
            for s in range(tn // LANES):
                xs = _rot_att(acc[:, s * LANES:(s + 1) * LANES], ca, s1, s2)
                o_ref[rows, s * LANES:(s + 1) * LANES] = (xs * ATT_HEAD_DIM ** -0.5).astype(BF16)

    @pl.when(region >= 5)
    def _():
        for rows in row_blocks:
            o_ref[rows, :] = jax.nn.sigmoid(matmul(rows)).astype(BF16)


def _mod_spec(mod, tm):
    if mod.shape[1] == 1:
        return pl.BlockSpec((None, 1, D_MODEL), lambda b, t, *_: (b, 0, 0))
    return pl.BlockSpec((None, tm, D_MODEL), lambda b, t, *_: (b, t, 0))


def _proj(x, sc, sh, tab, w_main, w_kv, tm, tn=PROJ_TN, cast=()):
    B, T, _ = x.shape
    nt = T // tm
    n_tiles = MAIN_W // tn
    cast_plan, cast_specs, cast_shapes, first = [], [], [], 0
    for wt, axis in cast:
        count = wt.shape[axis] // CAST_SLAB
        block = tuple(CAST_SLAB if d == axis else wt.shape[d] for d in range(2))

        def slab_map(b, t, n, first=first, count=count, axis=axis):
            slab = jnp.clip((b * nt + t) * n_tiles + n - first, 0, count - 1)
            return (slab, 0) if axis == 0 else (0, slab)

        cast_plan.append((first, count))
        cast_specs.append(pl.BlockSpec(block, slab_map))
        cast_shapes.append(jax.ShapeDtypeStruct(wt.shape, BF16))
        first += count
    outs = pl.pallas_call(
        functools.partial(_proj_kernel, cast_plan=tuple(cast_plan)),
        grid=(B, nt, n_tiles),
        in_specs=[pl.BlockSpec((None, tm, D_MODEL), lambda b, t, n: (b, t, 0)),
                  _mod_spec(sc, tm), _mod_spec(sh, tm),
                  pl.BlockSpec((5, tm, LANES), lambda b, t, n: (0, t, 0)),
                  pl.BlockSpec((D_MODEL, tn), lambda b, t, n: (0, n)),
                  pl.BlockSpec((D_MODEL, KV_W), lambda b, t, n: (0, 0), pipeline_mode=pl.Buffered(1))] + cast_specs,
        out_specs=[pl.BlockSpec((None, tm, tn), lambda b, t, n: (b, t, n)),
                   pl.BlockSpec((None, tm, KV_W), lambda b, t, n: (b, t, 0))] + cast_specs,
        out_shape=[jax.ShapeDtypeStruct((B, T, MAIN_W), BF16),
                   jax.ShapeDtypeStruct((B, T, KV_W), BF16)] + cast_shapes,
        scratch_shapes=[pltpu.VMEM((tm, D_MODEL), BF16)],
        compiler_params=_params(("arbitrary", "arbitrary", "arbitrary")),
        name="proj",
    )(x, sc, sh, tab, w_main, w_kv, *[wt for wt, _ in cast])
    return outs[0], outs[1], outs[2:]


def _repack_kernel(w_ref, o_ref):
    o_ref[...] = w_ref[...].astype(BF16)


def _repack_w_in(w):
    kv_blk = 5 * D_MODEL // KV_W
    return pl.pallas_call(
        _repack_kernel,
        grid=(MAIN_W // KV_W,),
        in_specs=[pl.BlockSpec((D_MODEL, KV_W), lambda n: (0, jnp.where(n >= kv_blk, n + 1, n)))],
        out_specs=pl.BlockSpec((D_MODEL, KV_W), lambda n: (0, n)),
        out_shape=jax.ShapeDtypeStruct((D_MODEL, MAIN_W), BF16),
        compiler_params=_params(("arbitrary",)),
        name="repack_w_in",
    )(w)


def _kv32_kernel(x_ref, sc_ref, sh_ref, tab_ref, wkv_ref, o_ref):
    h = (x_ref[...] * (1.0 + sc_ref[...]) + sh_ref[...]).astype(BF16)
    kv = jnp.dot(h, wkv_ref[...], preferred_element_type=F32)
    ca, s1, s2 = tab_ref[2], tab_ref[3], tab_ref[4]
    for s in range(KV_W // LANES):
        xs = kv[:, s * LANES:(s + 1) * LANES]
        if s < KV_W // LANES // 2:
            xs = _rot_att(xs, ca, s1, s2)
        o_ref[:, s * LANES:(s + 1) * LANES] = xs


def _kv32(x, sc, sh, tab, w_kv):
    B, R, _ = x.shape
    return pl.pallas_call(
        _kv32_kernel,
        grid=(B, 1),
        in_specs=[pl.BlockSpec((None, R, D_MODEL), lambda b, t: (b, 0, 0)),
                  _mod_spec(sc, R), _mod_spec(sh, R),
                  pl.BlockSpec((5, R, LANES), lambda b, t: (0, 0, 0)),
                  pl.BlockSpec((D_MODEL, KV_W), lambda b, t: (0, 0))],
        out_specs=pl.BlockSpec((None, R, KV_W), lambda b, t: (b, 0, 0)),
        out_shape=jax.ShapeDtypeStruct((B, R, KV_W), F32),
        compiler_params=_params(("arbitrary", "arbitrary")),
        name="kv32",
    )(x, sc, sh, tab, w_kv)


def _ret_kernel(lg_ref, q_ref, k_ref, v_ref, gs_ref, sg_ref, gn_ref, s0_ref, o_ref, sout_ref,
                s_ref, dm_ref, qd_ref, kd_ref, *, chunk, n_chunks, heads):
    hg = pl.program_id(1)
    t = pl.program_id(2)
    nt = (((1,), (1,)), ((), ()))
    tn = (((0,), (0,)), ((), ()))
    for hl in range(heads):
        lg = lg_ref[hg * heads + hl]
        cols = slice(hl * RET_DIM, (hl + 1) * RET_DIM)

        @pl.when(t == 0)
        def _():
            s_ref[hl] = s0_ref[hl]
            i = lax.broadcasted_iota(jnp.int32, (chunk, chunk), 0)
            j = lax.broadcasted_iota(jnp.int32, (chunk, chunk), 1)
            diff = (i - j).astype(F32)
            dm_ref[hl] = jnp.where(diff >= 0, jnp.exp(lg * jnp.maximum(diff, 0.0)), 0.0)
            r = lax.broadcasted_iota(jnp.int32, (chunk, RET_DIM), 0).astype(F32)
            qd_ref[hl] = jnp.exp(lg * (r + 1.0))
            kd_ref[hl] = jnp.exp(lg * (chunk - 1.0 - r))

        g_chunk = jnp.exp(jnp.full((1, RET_DIM), lg * chunk, F32))
        for c in range(n_chunks):
            rows = pl.ds(c * chunk, chunk)
            q, k, v = q_ref[rows, cols], k_ref[rows, cols], v_ref[rows, cols]
            inner = (lax.dot_general(q, k, nt, preferred_element_type=F32) * dm_ref[hl]).astype(BF16)
            s_prev = s_ref[hl]
            if chunk % LANES == 0:
                q_dec = (q.astype(F32) * qd_ref[hl]).astype(BF16)
                o = jnp.dot(jnp.concatenate([inner, q_dec], axis=1),
                            jnp.concatenate([v, s_prev.astype(BF16)], axis=0), preferred_element_type=F32)
            else:
                o = jnp.dot(inner, v, preferred_element_type=F32)
                o = o + jnp.dot(q, s_prev.astype(BF16), preferred_element_type=F32) * qd_ref[hl]
            kdec = (k.astype(F32) * kd_ref[hl]).astype(BF16)
            s_ref[hl] = s_prev * g_chunk + lax.dot_general(kdec, v, tn, preferred_element_type=F32)
            mu = jnp.mean(o, axis=-1, keepdims=True)
            xc = o - mu
            var = jnp.mean(xc * xc, axis=-1, keepdims=True)
            gn = xc * lax.rsqrt(var + LN_EPS) * gn_ref[:, cols]
            out = sg_ref[rows, cols].astype(F32) * (gs_ref[rows, cols].astype(F32) * gn)
            o_ref[rows, cols] = out.astype(BF16)

    @pl.when(t == pl.num_programs(2) - 1)
    def _():
        sout_ref[...] = s_ref[...]


def _retention(proj, gn_g, s0, chunk, tt, heads):
    B, T, _ = proj.shape
    n_chunks = tt // chunk
    width = heads * RET_DIM
    log_gamma = jnp.log1p(-jnp.exp2(-5.0 - jnp.arange(RET_HEADS, dtype=F32)))

    def col(off):
        base = off // width
        return pl.BlockSpec((None, tt, width), lambda b, h, t: (b, t, base + h))

    state = pl.BlockSpec((None, heads, RET_DIM, RET_DIM), lambda b, h, t: (b, h, 0, 0))
    return pl.pallas_call(
        functools.partial(_ret_kernel, chunk=chunk, n_chunks=n_chunks, heads=heads),
        grid=(B, RET_HEADS // heads, T // tt),
        in_specs=[pl.BlockSpec(memory_space=pltpu.SMEM),
                  col(COL_QR), col(COL_KR), col(COL_VR), col(COL_GR), col(COL_SGR),
                  pl.BlockSpec((1, width), lambda b, h, t: (0, h)), state],
        out_specs=[pl.BlockSpec((None, tt, width), lambda b, h, t: (b, t, h)), state],
        out_shape=[jax.ShapeDtypeStruct((B, T, D_MODEL), BF16),
                   jax.ShapeDtypeStruct((B, RET_HEADS, RET_DIM, RET_DIM), F32)],
        scratch_shapes=[pltpu.VMEM((heads, RET_DIM, RET_DIM), F32),
                        pltpu.VMEM((heads, chunk, chunk), F32),
                        pltpu.VMEM((heads, chunk, RET_DIM), F32),
                        pltpu.VMEM((heads, chunk, RET_DIM), F32)],
        compiler_params=_params(("arbitrary", "arbitrary", "arbitrary")),
        name="retention",
    )(log_gamma, proj, proj, proj, proj, proj, gn_g.reshape(1, D_MODEL), s0)


def _att_kernel(sink_ref, q_ref, halo_ref, cur_ref, sg_ref, o_ref, *, cq, n_chunks, own_valid, mask_first):
    i = pl.program_id(1)
    kv = jnp.concatenate([halo_ref[...], cur_ref[...]], axis=0).astype(F32)
    n_keys = kv.shape[0]
    n_kc = n_keys // CHUNK
    n_tiles = (n_kc + 1) // 2
    lane = lax.broadcasted_iota(jnp.int32, (1, LANES), 1)
    lo = lane < ATT_HEAD_DIM
    ones_lo = jnp.broadcast_to(jnp.where(lo, 1.0, 0.0), (n_keys, LANES)).astype(BF16)
    ones_hi = jnp.broadcast_to(jnp.where(lo, 0.0, 1.0), (n_keys, LANES)).astype(BF16)
    own_ok = (lane % ATT_HEAD_DIM) < own_valid
    nt = (((1,), (1,)), ((), ()))
    pairs = ATT_GROUP // 2
    rows_q = pairs * cq
    k_off = ATT_KV_HEADS * ATT_HEAD_DIM
    users = [[c for c in range(n_chunks) if c <= 2 * u + 1 and c + 2 >= 2 * u] for u in range(n_tiles)]

    def tile_rows(a_lo, a_hi, u):
        parts = []
        for j in (2 * u, 2 * u + 1):
            for a in (a_lo, a_hi):
                parts.append(a[j * CHUNK:(j + 1) * CHUNK] if j < n_kc else jnp.zeros((CHUNK, a.shape[1]), a.dtype))
        return jnp.concatenate(parts, axis=0)

    for y in range(ATT_KV_HEADS // 2):
        ks = kv[:, y * LANES:(y + 1) * LANES]
        kr = pltpu.roll(ks, ATT_HEAD_DIM, 1)
        vs = kv[:, k_off + y * LANES:k_off + (y + 1) * LANES]
        vr = pltpu.roll(vs, ATT_HEAD_DIM, 1)
        for par in range(2):
            x = 2 * y + par
            k_even, k_odd = (ks, kr) if par == 0 else (kr, ks)
            v_even, v_odd = (vs, vr) if par == 0 else (vr, vs)
            klo = jnp.where(lo, k_even, 0.0).astype(BF16)
            khi = jnp.where(lo, 0.0, k_odd).astype(BF16)
            vlo = jnp.concatenate([jnp.where(lo, v_even, 0.0).astype(BF16), ones_lo], axis=1)
            vhi = jnp.concatenate([jnp.where(lo, 0.0, v_odd).astype(BF16), ones_hi], axis=1)
            sink_b = jnp.concatenate(
                [jnp.broadcast_to(jnp.where(lo, sink_ref[x * ATT_GROUP + 2 * p],
                                            sink_ref[x * ATT_GROUP + 2 * p + 1]), (cq, LANES))
                 for p in range(pairs)], axis=0)
            col0 = x * ATT_GROUP * ATT_HEAD_DIM
            qp = [jnp.concatenate([q_ref[c * cq:(c + 1) * cq, col0 + p * LANES:col0 + (p + 1) * LANES]
                                   for p in range(pairs)], axis=0) for c in range(n_chunks)]
            s_tiles = [lax.dot_general(jnp.concatenate([qp[c] for c in users[u]], axis=0), tile_rows(klo, khi, u),
                                       nt, preferred_element_type=F32) for u in range(n_tiles)]
            p_blocks, m_all = {}, {}
            for c in range(n_chunks):
                sj = []
                for jj in range(3):
                    u, half = (c + jj) // 2, (c + jj) % 2
                    r = users[u].index(c) * rows_q
                    sj.append(s_tiles[u][r:r + rows_q, half * LANES:(half + 1) * LANES])
                if mask_first:
                    for jj in range(2):
                        if c + jj < 2:
                            sj[jj] = jnp.where(i * n_chunks + (c + jj - 2) < 0, NEG_INF, sj[jj])
                if own_valid < CHUNK:
                    sj[2] = jnp.where(own_ok, sj[2], NEG_INF)
                m3 = jnp.maximum(jnp.maximum(sj[0], sj[1]), sj[2])
                m_even = jnp.max(jnp.where(lo, m3, NEG_INF), axis=1, keepdims=True)
                m_odd = jnp.max(jnp.where(lo, NEG_INF, m3), axis=1, keepdims=True)
                m_b = jnp.maximum(jnp.where(lo, m_even, m_odd), sink_b)
                m_all[c] = m_b
                for jj in range(3):
                    p_blocks[(c, c + jj)] = jnp.exp(sj[jj] - m_b).astype(BF16)
            zero_p = jnp.zeros((rows_q, LANES), BF16)
            o_tiles = []
            for u in range(n_tiles):
                p_u = jnp.concatenate(
                    [jnp.concatenate([p_blocks.get((c, j), zero_p) for j in (2 * u, 2 * u + 1)], axis=1)
                     for c in users[u]], axis=0)
                o_tiles.append(jnp.dot(p_u, tile_rows(vlo, vhi, u), preferred_element_type=F32))
            for c in range(n_chunks):
                oe = None
                for u in sorted({c // 2, (c + 2) // 2}):
                    r = users[u].index(c) * rows_q
                    part = o_tiles[u][r:r + rows_q]
                    oe = part if oe is None else oe + part
                den = oe[:, LANES:] + jnp.exp(sink_b - m_all[c])
                res = oe[:, :LANES] / den
                for p in range(pairs):
                    cols = slice(col0 + p * LANES, col0 + (p + 1) * LANES)
                    rows = slice(c * cq, (c + 1) * cq)
                    gate = sg_ref[rows, cols].astype(F32)
                    o_ref[rows, cols] = (gate * res[p * cq:(p + 1) * cq]).astype(BF16)


def _attention(proj, halo_src, cur_src, sinks, cq, n_chunks, own_valid, mask_first):
    B, T, _ = proj.shape
    tq = cq * n_chunks
    cur_rows = CHUNK * n_chunks
    q_blk, sg_blk = COL_QA // D_MODEL, COL_SGA // D_MODEL
    halo_per_cur = cur_rows // WINDOW if mask_first else 0

    def halo_map(b, i):
        return (b, jnp.maximum(i * halo_per_cur - 1, 0), 0) if mask_first else (b, 0, 0)

    return pl.pallas_call(
        functools.partial(_att_kernel, cq=cq, n_chunks=n_chunks, own_valid=own_valid, mask_first=mask_first),
        grid=(B, T // tq),
        in_specs=[pl.BlockSpec(memory_space=pltpu.SMEM),
                  pl.BlockSpec((None, tq, D_MODEL), lambda b, i: (b, i, q_blk)),
                  pl.BlockSpec((None, WINDOW, KV_W), halo_map),
                  pl.BlockSpec((None, cur_rows, KV_W), lambda b, i: (b, i, 0)),
                  pl.BlockSpec((None, tq, D_MODEL), lambda b, i: (b, i, sg_blk))],
        out_specs=pl.BlockSpec((None, tq, D_MODEL), lambda b, i: (b, i, 0)),
        out_shape=jax.ShapeDtypeStruct((B, T, D_MODEL), BF16),
        compiler_params=_params(("arbitrary", "arbitrary")),
        name="attention",
    )(sinks, proj, halo_src, cur_src, proj)


def _layernorm(z, g, b):
    mu = jnp.mean(z, axis=-1, keepdims=True)
    zc = z - mu
    var = jnp.mean(zc * zc, axis=-1, keepdims=True)
    return zc * lax.rsqrt(var + LN_EPS) * g + b


def _out_kernel(r_ref, a_ref, x_ref, gt_ref, sc_ref, sh_ref, w_ref, g_ref, b_ref, x1_ref, h2_ref):
    n_split = max(x_ref.shape[0] // OUT_SPLIT_ROWS, 1)
    rows_per = x_ref.shape[0] // n_split

    def mod_rows(ref, rows):
        return ref[...] if ref.shape[0] == 1 else ref[rows, :]

    for r in range(n_split):
        rows = slice(r * rows_per, (r + 1) * rows_per)
        merged = r_ref[rows, :] + a_ref[rows, :]
        mix = jnp.dot(merged, w_ref[...], preferred_element_type=F32)
        z = ALPHA * x_ref[rows, :] + mod_rows(gt_ref, rows) * mix
        x1 = _layernorm(z, g_ref[...], b_ref[...])
        x1_ref[rows, :] = x1
        h2_ref[rows, :] = (x1 * (1.0 + mod_rows(sc_ref, rows)) + mod_rows(sh_ref, rows)).astype(BF16)


def _out_proj(ret_m, att_m, x, gt, sc, sh, w_o, g, b, tm):
    B, T, _ = x.shape
    tok = pl.BlockSpec((None, tm, D_MODEL), lambda bb, t: (bb, t, 0))
    vec = pl.BlockSpec((1, D_MODEL), lambda bb, t: (0, 0))
    return pl.pallas_call(
        _out_kernel,
        grid=(B, T // tm),
        in_specs=[tok, tok, tok, _mod_spec(gt, tm), _mod_spec(sc, tm), _mod_spec(sh, tm),
                  pl.BlockSpec((D_MODEL, D_MODEL), lambda bb, t: (0, 0)), vec, vec],
        out_specs=[tok, tok],
        out_shape=[jax.ShapeDtypeStruct((B, T, D_MODEL), F32),
                   jax.ShapeDtypeStruct((B, T, D_MODEL), BF16)],
        compiler_params=_params(("arbitrary", "arbitrary")),
        name="out_proj",
    )(ret_m, att_m, x, gt, sc, sh, w_o, g.reshape(1, D_MODEL), b.reshape(1, D_MODEL))


def _ffn_kernel(h_ref, x1_ref, gt_ref, wg_ref, wu_ref, wd_ref, g_ref, b_ref, o_ref):
    f = pl.program_id(2)
    last = pl.num_programs(2) - 1
    tm = h_ref.shape[0]

    def row_blocks(rows_per):
        n_split = max(tm // rows_per, 1)
        return [slice(r * (tm // n_split), (r + 1) * (tm // n_split)) for r in range(n_split)]

    @pl.when(f == 0)
    def _():
        o_ref[...] = jnp.zeros_like(o_ref)

    def step(rows_per, finalize):
        h = h_ref[...]
        a = jnp.dot(h, wg_ref[...], preferred_element_type=F32)
        u = jnp.dot(h, wu_ref[...], preferred_element_type=F32)
        act = (a * jax.nn.sigmoid(a) * u).astype(BF16)
        for rows in row_blocks(rows_per):
            acc = o_ref[rows, :] + jnp.dot(act[rows, :], wd_ref[...], preferred_element_type=F32)
            if finalize:
                gt = gt_ref[...] if gt_ref.shape[0] == 1 else gt_ref[rows, :]
                acc = _layernorm(ALPHA * x1_ref[rows, :] + gt * acc, g_ref[...], b_ref[...])
            o_ref[rows, :] = acc

    @pl.when(f < last)
    def _():
        step(FFN_SPLIT_ROWS, False)

    @pl.when(f == last)
    def _():
        step(FFN_LAST_SPLIT_ROWS, True)


def _ffn(h2, x1, gt, wg, wu, wd, g, b, tm, tf):
    B, T, _ = x1.shape
    tok = pl.BlockSpec((None, tm, D_MODEL), lambda bb, t, f: (bb, t, 0))
    vec = pl.BlockSpec((1, D_MODEL), lambda bb, t, f: (0, 0))
    return pl.pallas_call(
        _ffn_kernel,
        grid=(B, T // tm, D_FF // tf),
        in_specs=[tok, tok, _mod_spec(gt, tm),
                  pl.BlockSpec((D_MODEL, tf), lambda bb, t, f: (0, f)),
                  pl.BlockSpec((D_MODEL, tf), lambda bb, t, f: (0, f)),
                  pl.BlockSpec((tf, D_MODEL), lambda bb, t, f: (f, 0)), vec, vec],
        out_specs=tok,
        out_shape=jax.ShapeDtypeStruct((B, T, D_MODEL), F32),
        compiler_params=_params(("arbitrary", "arbitrary", "arbitrary")),
        name="ffn",
    )(h2, x1, gt, wg, wu, wd, g.reshape(1, D_MODEL), b.reshape(1, D_MODEL))


def _largest_tile(total, cap):
    t = min(total, cap)
    while total % t:
        t //= 2
    return t


def kernel(x_prompt, x_sample, c_prompt, c_sample, cache_attn_k, cache_attn_v, state_ret, w_ada, b_ada, w_in,
           gn_g, attn_sinks, w_o, ln1_g, ln1_b, w_ffn_gate, w_ffn_up, w_ffn_down, ln2_g, ln2_b):
    B, T, _ = x_prompt.shape
    Bs, Ls, _ = x_sample.shape
    l = 0

    n_c = B + Bs
    pad = (-n_c) % 8
    c_all = jnp.concatenate([c_prompt, c_sample, jnp.zeros((pad, D_MODEL), F32)], axis=0)
    mods = _mods(c_all, w_ada[l], b_ada[l])
    mods_p = [m[:, None, :] for m in jnp.split(mods[:B], 6, axis=-1)]
    mods_s = [jnp.repeat(m, Ls, axis=0)[None] for m in jnp.split(mods[B:n_c], 6, axis=-1)]

    w = w_in[l]
    o_ka = 5 * D_MODEL
    o_gate = o_ka + KV_W
    w_main = _repack_w_in(w)
    w_kv = w[:, o_ka:o_gate].astype(BF16)

    sh_a, sc_a, gt_a, sh_f, sc_f, gt_f = mods_p
    tab_p = _rope_tables(jnp.arange(T))
    tm_proj = _largest_tile(T, 1024)
    later = ((w_ffn_gate[l], 1), (w_ffn_up[l], 1), (w_ffn_down[l], 0), (w_o[l], 1))
    if sum(wt.shape[axis] // CAST_SLAB for wt, axis in later) <= B * (T // tm_proj) * (MAIN_W // PROJ_TN):
        proj, kvb, (wg, wu, wd, wo) = _proj(x_prompt, sc_a, sh_a, tab_p, w_main, w_kv, tm_proj, cast=later)
    else:
        proj, kvb, _ = _proj(x_prompt, sc_a, sh_a, tab_p, w_main, w_kv, tm_proj)
        wg, wu, wd, wo = (wt.astype(BF16) for wt, _ in later)
    ret_m, s_p = _retention(proj, gn_g[l], jnp.zeros((B, RET_HEADS, RET_DIM, RET_DIM), F32),
                            _largest_tile(T, 256), _largest_tile(T, 8192), 1)
    att_m = _attention(proj, kvb, kvb, attn_sinks[l], CHUNK, _largest_tile(T // CHUNK, 32), CHUNK, True)
    x1, h2 = _out_proj(ret_m, att_m, x_prompt, gt_a, sc_f, sh_f, wo, ln1_g[l], ln1_b[l], _largest_tile(T, 512))
    y_p = _ffn(h2, x1, gt_f, wg, wu, wd, ln2_g[l], ln2_b[l], _largest_tile(T, 1024), FFN_TF)
    kv_p = _kv32(x_prompt[:, T - WINDOW:], mods_p[1], mods_p[0], tab_p[:, T - WINDOW:], w_kv)
    k_p = kv_p[..., :KV_W // 2].reshape(B, WINDOW, ATT_KV_HEADS, ATT_HEAD_DIM)
    v_p = kv_p[..., KV_W // 2:].reshape(B, WINDOW, ATT_KV_HEADS, ATT_HEAD_DIM)

    R = Bs * Ls
    xs = x_sample.reshape(1, R, D_MODEL)
    pos_s = jnp.tile(PAST_LEN + jnp.arange(Ls), Bs)
    cache = jnp.concatenate([cache_attn_k[l].reshape(Bs, WINDOW, KV_W // 2),
                             cache_attn_v[l].reshape(Bs, WINDOW, KV_W // 2)], axis=-1).astype(BF16)

    def att_sample(proj, kvb):
        new = jnp.pad(kvb.reshape(Bs, Ls, KV_W), ((0, 0), (0, CHUNK - Ls), (0, 0)))
        o = _attention(proj.reshape(Bs, Ls, MAIN_W), cache, new, attn_sinks[l], Ls, 1, Ls, False)
        return o.reshape(1, R, D_MODEL)

    def ret_sample(proj, s0):
        return _retention(proj.reshape(Bs, Ls, MAIN_W), gn_g[l], s0, Ls, Ls, RET_HEADS)

    sh_a, sc_a, gt_a, sh_f, sc_f, gt_f = mods_s
    tab_s = _rope_tables(pos_s)
    proj_s, kvb_s, _ = _proj(xs, sc_a, sh_a, tab_s, w_main, w_kv, R, tn=D_MODEL)
    ret_s, s_s = ret_sample(proj_s, state_ret[l])
    att_s = att_sample(proj_s, kvb_s)
    x1_s, h2_s = _out_proj(ret_s.reshape(1, R, D_MODEL), att_s, xs, gt_a, sc_f, sh_f, wo, ln1_g[l], ln1_b[l], R)
    y_s = _ffn(h2_s, x1_s, gt_f, wg, wu, wd, ln2_g[l], ln2_b[l], R, FFN_TF_SAMPLE).reshape(Bs, Ls, D_MODEL)
    kv_s = _kv32(xs, sc_a, sh_a, tab_s, w_kv).reshape(Bs, Ls, KV_W)
    k_s = kv_s[..., :KV_W // 2].reshape(Bs, Ls, ATT_KV_HEADS, ATT_HEAD_DIM)
    v_s = kv_s[..., KV_W // 2:].reshape(Bs, Ls, ATT_KV_HEADS, ATT_HEAD_DIM)

    return (y_p, y_s, k_p[None], v_p[None], s_p[None], k_s[None], v_s[None], s_s[None])
```

```python
import functools

import jax
import jax.numpy as jnp
from jax import lax
from jax.experimental import pallas as pl
from jax.experimental.pallas import tpu as pltpu

F32 = jnp.float32
BF16 = jnp.bfloat16

D_MODEL = 2048
CHUNK = 64
PAST_LEN = 1024
RET_HEADS = 8
RET_DIM = 256
RET_ROPE_THETA = 10000.0
ATT_HEADS = 32
ATT_KV_HEADS = 4
ATT_HEAD_DIM = 64
ATT_GROUP = ATT_HEADS // ATT_KV_HEADS
WINDOW = 128
ROPE_DIM = ATT_HEAD_DIM // 4
ROPE_THETA = 500000.0
D_FF = 5632
DEPTH = 1
ALPHA = (2.0 * DEPTH) ** 0.25
LN_EPS = 1e-5
NEG_INF = -1e30

LANES = 128
KV_W = 2 * ATT_KV_HEADS * ATT_HEAD_DIM
MAIN_W = 7 * D_MODEL
PROJ_TN = 1024
PROJ_SPLIT_ROWS = 256
PROJ_LIGHT_SPLIT_ROWS = 512
CAST_SLAB = 128
FFN_TF = 512
FFN_TF_SAMPLE = 1408
FFN_SPLIT_ROWS = 512
FFN_LAST_SPLIT_ROWS = 256
OUT_SPLIT_ROWS = 256
VMEM_LIMIT = 60 * 1024 * 1024

COL_QR, COL_KR, COL_GATE, COL_VR, COL_QA, COL_SGA = 0, D_MODEL, 2 * D_MODEL, 4 * D_MODEL, 5 * D_MODEL, 6 * D_MODEL
GATE_GROUP = 2 * KV_W


def _params(sem):
    return pltpu.CompilerParams(dimension_semantics=sem, vmem_limit_bytes=VMEM_LIMIT)


def _mods_kernel(c_ref, w_ref, b_ref, o_ref):
    c = c_ref[...]
    a = (c * jax.nn.sigmoid(c)).astype(BF16)
    o_ref[...] = jnp.dot(a, w_ref[...].astype(BF16), preferred_element_type=F32) + b_ref[...]


def _mods(c_all, w_ada, b_ada):
    rows = c_all.shape[0]
    n_out = w_ada.shape[1]
    tn = 1024
    return pl.pallas_call(
        _mods_kernel,
        grid=(n_out // tn,),
        in_specs=[pl.BlockSpec((rows, D_MODEL), lambda n: (0, 0)),
                  pl.BlockSpec((D_MODEL, tn), lambda n: (0, n)),
                  pl.BlockSpec((1, tn), lambda n: (0, n))],
        out_specs=pl.BlockSpec((rows, tn), lambda n: (0, n)),
        out_shape=jax.ShapeDtypeStruct((rows, n_out), F32),
        compiler_params=_params(("arbitrary",)),
        name="mods",
    )(c_all, w_ada, b_ada.reshape(1, n_out))


def _rope_tables(pos):
    posf = pos.astype(F32)[:, None]
    half_r = RET_DIM // 2
    inv_r = 1.0 / (RET_ROPE_THETA ** (jnp.arange(half_r, dtype=F32) / half_r))
    ang_r = posf * inv_r[None, :]
    half_a = ROPE_DIM // 2
    inv_a = 1.0 / (ROPE_THETA ** (jnp.arange(half_a, dtype=F32) / half_a))
    d = jnp.arange(LANES) % ATT_HEAD_DIM
    ang_a = posf * inv_a[None, :]
    cos_a = jnp.tile(jnp.cos(ang_a), (1, LANES // half_a))
    sin_a = jnp.tile(jnp.sin(ang_a), (1, LANES // half_a))
    ca = jnp.where(d[None, :] < ROPE_DIM, cos_a, 1.0)
    s1 = jnp.where(d[None, :] < half_a, -sin_a, 0.0)
    s2 = jnp.where((d[None, :] >= half_a) & (d[None, :] < ROPE_DIM), sin_a, 0.0)
    return jnp.stack([jnp.cos(ang_r), jnp.sin(ang_r), ca, s1, s2])


def _rot_att(x, ca, s1, s2):
    half = ROPE_DIM // 2
    return x * ca + pltpu.roll(x, LANES - half, 1) * s1 + pltpu.roll(x, half, 1) * s2


def _proj_kernel(*refs, cast_plan):
    n_cast = len(cast_plan)
    x_ref, sc_ref, sh_ref, tab_ref, w_ref, wkv_ref = refs[:6]
    cast_in = refs[6:6 + n_cast]
    o_ref, kv_ref = refs[6 + n_cast:8 + n_cast]
    cast_out = refs[8 + n_cast:8 + 2 * n_cast]
    h_ref = refs[-1]
    n = pl.program_id(2)
    tm = x_ref.shape[0]

    step = (pl.program_id(0) * pl.num_programs(1) + pl.program_id(1)) * pl.num_programs(2) + n
    for (first, count), src_ref, dst_ref in zip(cast_plan, cast_in, cast_out):
        @pl.when((step >= first) & (step < first + count))
        def _():
            dst_ref[...] = src_ref[...].astype(BF16)

    def blocks(rows_per):
        n_split = max(tm // rows_per, 1)
        return [slice(r * (tm // n_split), (r + 1) * (tm // n_split)) for r in range(n_split)]

    row_blocks = blocks(PROJ_SPLIT_ROWS)
    light_blocks = blocks(PROJ_LIGHT_SPLIT_ROWS)
    tn = w_ref.shape[1]

    def mod_rows(ref, rows):
        return ref[...] if ref.shape[0] == 1 else ref[rows, :]

    def att_tabs(rows):
        return tab_ref[2, rows, :], tab_ref[3, rows, :], tab_ref[4, rows, :]

    def modulate_and_kv(rows):
        h = x_ref[rows, :] * (1.0 + mod_rows(sc_ref, rows)) + mod_rows(sh_ref, rows)
        h_ref[rows, :] = h.astype(BF16)
        kv = jnp.dot(h_ref[rows, :], wkv_ref[...], preferred_element_type=F32)
        ca, s1, s2 = att_tabs(rows)
        for s in range(KV_W // LANES):
            xs = kv[:, s * LANES:(s + 1) * LANES]
            if s < KV_W // LANES // 2:
                xs = _rot_att(xs, ca, s1, s2)
            kv_ref[rows, s * LANES:(s + 1) * LANES] = xs.astype(BF16)

    def matmul(rows):
        return jnp.dot(h_ref[rows, :], w_ref[...], preferred_element_type=F32)

    def rot_ret_rows(rows, scale):
        acc = matmul(rows)
        cr, sr = tab_ref[0, rows, :], tab_ref[1, rows, :]
        for j in range(tn // RET_DIM):
            a = j * RET_DIM
            x1 = acc[:, a:a + LANES]
            x2 = acc[:, a + LANES:a + 2 * LANES]
            o_ref[rows, a:a + LANES] = ((x1 * cr - x2 * sr) * scale).astype(BF16)
            o_ref[rows, a + LANES:a + 2 * LANES] = ((x1 * sr + x2 * cr) * scale).astype(BF16)

    def rot_ret(scale):
        for rows in light_blocks:
            rot_ret_rows(rows, scale)

    region = n // (D_MODEL // tn)

    @pl.when(n == 0)
    def _():
        for rows in row_blocks:
            modulate_and_kv(rows)
            rot_ret_rows(rows, 1.0)

    @pl.when(((region == 0) & (n > 0)) | (region == 1))
    def _():
        rot_ret(jnp.where(region == 0, 1.0, RET_DIM ** -0.5))

    @pl.when((region == 2) | (region == 3))
    def _():
        half = GATE_GROUP // 2
        for rows in row_blocks:
            acc = matmul(rows)
            for grp in range(tn // GATE_GROUP):
                g = acc[:, grp * GATE_GROUP:grp * GATE_GROUP + half]
                gate = acc[:, grp * GATE_GROUP + half:(grp + 1) * GATE_GROUP]
                prod = g * jax.nn.sigmoid(g) * jax.nn.sigmoid(gate)
                o_ref[rows, grp * GATE_GROUP:grp * GATE_GROUP + half] = prod.astype(BF16)
                o_ref[rows, grp * GATE_GROUP + half:(grp + 1) * GATE_GROUP] = jnp.zeros(prod.shape, BF16)

    @pl.when(region == 4)
    def _():
        for rows in light_blocks:
            o_ref[rows, :] = matmul(rows).astype(BF16)

    @pl.when(region == 5)
    def _():
        for rows in row_blocks:
            acc = matmul(rows)
            ca, s1, s2 = att_tabs(rows)
            for s in range(tn // LANES):
                xs = _rot_att(acc[:, s * LANES:(s + 1) * LANES], ca, s1, s2)
                o_ref[rows, s * LANES:(s + 1) * LANES] = (xs * ATT_HEAD_DIM ** -0.5).astype(BF16)

    @pl.when(region == 6)
    def _():
        for rows in row_blocks:
            o_ref[rows, :] = jax.nn.sigmoid(matmul(rows)).astype(BF16)


def _mod_spec(mod, tm):
    if mod.shape[1] == 1:
        return pl.BlockSpec((None, 1, D_MODEL), lambda b, t, *_: (b, 0, 0))
    return pl.BlockSpec((None, tm, D_MODEL), lambda b, t, *_: (b, t, 0))


def _proj(x, sc, sh, tab, w_main, w_kv, tm, tn=PROJ_TN, cast=()):
    B, T, _ = x.shape
    nt = T // tm
    n_tiles = MAIN_W // tn
    cast_plan, cast_specs, cast_shapes, first = [], [], [], 0
    for wt, axis in cast:
        count = wt.shape[axis] // CAST_SLAB
        block = tuple(CAST_SLAB if d == axis else wt.shape[d] for d in range(2))

        def slab_map(b, t, n, first=first, count=count, axis=axis):
            slab = jnp.clip((b * nt + t) * n_tiles + n - first, 0, count - 1)
            return (slab, 0) if axis == 0 else (0, slab)

        cast_plan.append((first, count))
        cast_specs.append(pl.BlockSpec(block, slab_map))
        cast_shapes.append(jax.ShapeDtypeStruct(wt.shape, BF16))
        first += count
    outs = pl.pallas_call(
        functools.partial(_proj_kernel, cast_plan=tuple(cast_plan)),
        grid=(B, nt, n_tiles),
        in_specs=[pl.BlockSpec((None, tm, D_MODEL), lambda b, t, n: (b, t, 0)),
                  _mod_spec(sc, tm), _mod_spec(sh, tm),
                  pl.BlockSpec((5, tm, LANES), lambda b, t, n: (0, t, 0)),
                  pl.BlockSpec((D_MODEL, tn), lambda b, t, n: (0, n)),
                  pl.BlockSpec((D_MODEL, KV_W), lambda b, t, n: (0, 0), pipeline_mode=pl.Buffered(1))] + cast_specs,
        out_specs=[pl.BlockSpec((None, tm, tn), lambda b, t, n: (b, t, n)),
                   pl.BlockSpec((None, tm, KV_W), lambda b, t, n: (b, t, 0))] + cast_specs,
        out_shape=[jax.ShapeDtypeStruct((B, T, MAIN_W), BF16),
                   jax.ShapeDtypeStruct((B, T, KV_W), BF16)] + cast_shapes,
        scratch_shapes=[pltpu.VMEM((tm, D_MODEL), BF16)],
        compiler_params=_params(("arbitrary", "arbitrary", "arbitrary")),
        name="proj",
    )(x, sc, sh, tab, w_main, w_kv, *[wt for wt, _ in cast])
    return outs[0], outs[1], outs[2:]


def _repack_kernel(w_ref, o_ref):
    o_ref[...] = w_ref[...].astype(BF16)


def _repack_w_in(w):
    per = D_MODEL // KV_W

    def src_block(n):
        k = n - 2 * per
        pair = jnp.where(k % 2 == 0, 3 * per + k // 2, 5 * per + 1 + k // 2)
        return jnp.where(n < 2 * per, n,
                         jnp.where(n < 4 * per, pair,
                                   jnp.where(n < 5 * per, n - 2 * per,
                                             jnp.where(n < 6 * per, n - per, n + 1))))

    return pl.pallas_call(
        _repack_kernel,
        grid=(MAIN_W // KV_W,),
        in_specs=[pl.BlockSpec((D_MODEL, KV_W), lambda n: (0, src_block(n)))],
        out_specs=pl.BlockSpec((D_MODEL, KV_W), lambda n: (0, n)),
        out_shape=jax.ShapeDtypeStruct((D_MODEL, MAIN_W), BF16),
        compiler_params=_params(("arbitrary",)),
        name="repack_w_in",
    )(w)


def _kv32_kernel(x_ref, sc_ref, sh_ref, tab_ref, wkv_ref, o_ref):
    h = (x_ref[...] * (1.0 + sc_ref[...]) + sh_ref[...]).astype(BF16)
    kv = jnp.dot(h, wkv_ref[...], preferred_element_type=F32)
    ca, s1, s2 = tab_ref[2], tab_ref[3], tab_ref[4]
    for s in range(KV_W // LANES):
        xs = kv[:, s * LANES:(s + 1) * LANES]
        if s < KV_W // LANES // 2:
            xs = _rot_att(xs, ca, s1, s2)
        o_ref[:, s * LANES:(s + 1) * LANES] = xs


def _kv32(x, sc, sh, tab, w_kv):
    B, R, _ = x.shape
    return pl.pallas_call(
        _kv32_kernel,
        grid=(B, 1),
        in_specs=[pl.BlockSpec((None, R, D_MODEL), lambda b, t: (b, 0, 0)),
                  _mod_spec(sc, R), _mod_spec(sh, R),
                  pl.BlockSpec((5, R, LANES), lambda b, t: (0, 0, 0)),
                  pl.BlockSpec((D_MODEL, KV_W), lambda b, t: (0, 0))],
        out_specs=pl.BlockSpec((None, R, KV_W), lambda b, t: (b, 0, 0)),
        out_shape=jax.ShapeDtypeStruct((B, R, KV_W), F32),
        compiler_params=_params(("arbitrary", "arbitrary")),
        name="kv32",
    )(x, sc, sh, tab, w_kv)


def _ret_kernel(lg_ref, q_ref, k_ref, v_ref, gate_ref, gn_ref, s0_ref, o_ref, sout_ref,
                s_ref, dm_ref, qd_ref, kd_ref, *, chunk, n_chunks, heads):
    hg = pl.program_id(1)
    t = pl.program_id(2)
    nt = (((1,), (1,)), ((), ()))
    tn = (((0,), (0,)), ((), ()))
    for hl in range(heads):
        lg = lg_ref[hg * heads + hl]
        cols = slice(hl * RET_DIM, (hl + 1) * RET_DIM)
        g0 = 0 if heads == 1 else (hl // 2) * GATE_GROUP + (hl % 2) * RET_DIM
        gate_cols = slice(g0, g0 + RET_DIM)

        @pl.when(t == 0)
        def _():
            s_ref[hl] = s0_ref[hl]
            i = lax.broadcasted_iota(jnp.int32, (chunk, chunk), 0)
            j = lax.broadcasted_iota(jnp.int32, (chunk, chunk), 1)
            diff = (i - j).astype(F32)
            dm_ref[hl] = jnp.where(diff >= 0, jnp.exp(lg * jnp.maximum(diff, 0.0)), 0.0)
            r = lax.broadcasted_iota(jnp.int32, (chunk, RET_DIM), 0).astype(F32)
            qd_ref[hl] = jnp.exp(lg * (r + 1.0))
            kd_ref[hl] = jnp.exp(lg * (chunk - 1.0 - r))

        g_chunk = jnp.exp(jnp.full((1, RET_DIM), lg * chunk, F32))
        for c in range(n_chunks):
            rows = pl.ds(c * chunk, chunk)
            q, k, v = q_ref[rows, cols], k_ref[rows, cols], v_ref[rows, cols]
            inner = (lax.dot_general(q, k, nt, preferred_element_type=F32) * dm_ref[hl]).astype(BF16)
            s_prev = s_ref[hl]
            if chunk % LANES == 0:
                q_dec = (q.astype(F32) * qd_ref[hl]).astype(BF16)
                o = jnp.dot(jnp.concatenate([inner, q_dec], axis=1),
                            jnp.concatenate([v, s_prev.astype(BF16)], axis=0), preferred_element_type=F32)
            else:
                o = jnp.dot(inner, v, preferred_element_type=F32)
                o = o + jnp.dot(q, s_prev.astype(BF16), preferred_element_type=F32) * qd_ref[hl]
            kdec = (k.astype(F32) * kd_ref[hl]).astype(BF16)
            s_ref[hl] = s_prev * g_chunk + lax.dot_general(kdec, v, tn, preferred_element_type=F32)
            mu = jnp.mean(o, axis=-1, keepdims=True)
            xc = o - mu
            var = jnp.mean(xc * xc, axis=-1, keepdims=True)
            gn = xc * lax.rsqrt(var + LN_EPS) * gn_ref[:, cols]
            out = gate_ref[rows, gate_cols].astype(F32) * gn
            o_ref[rows, cols] = out.astype(BF16)

    @pl.when(t == pl.num_programs(2) - 1)
    def _():
        sout_ref[...] = s_ref[...]


def _retention(proj, gn_g, s0, chunk, tt, heads):
    B, T, _ = proj.shape
    n_chunks = tt // chunk
    width = heads * RET_DIM
    log_gamma = jnp.log1p(-jnp.exp2(-5.0 - jnp.arange(RET_HEADS, dtype=F32)))

    def col(off):
        base = off // width
        return pl.BlockSpec((None, tt, width), lambda b, h, t: (b, t, base + h))

    if heads == 1:
        gate = pl.BlockSpec((None, tt, RET_DIM),
                            lambda b, h, t: (b, t, COL_GATE // RET_DIM + (h // 2) * (GATE_GROUP // RET_DIM) + h % 2))
    else:
        assert heads == RET_HEADS
        gate = pl.BlockSpec((None, tt, 2 * D_MODEL), lambda b, h, t: (b, t, COL_GATE // (2 * D_MODEL)))
    state = pl.BlockSpec((None, heads, RET_DIM, RET_DIM), lambda b, h, t: (b, h, 0, 0))
    return pl.pallas_call(
        functools.partial(_ret_kernel, chunk=chunk, n_chunks=n_chunks, heads=heads),
        grid=(B, RET_HEADS // heads, T // tt),
        in_specs=[pl.BlockSpec(memory_space=pltpu.SMEM),
                  col(COL_QR), col(COL_KR), col(COL_VR), gate,
                  pl.BlockSpec((1, width), lambda b, h, t: (0, h)), state],
        out_specs=[pl.BlockSpec((None, tt, width), lambda b, h, t: (b, t, h)), state],
        out_shape=[jax.ShapeDtypeStruct((B, T, D_MODEL), BF16),
                   jax.ShapeDtypeStruct((B, RET_HEADS, RET_DIM, RET_DIM), F32)],
        scratch_shapes=[pltpu.VMEM((heads, RET_DIM, RET_DIM), F32),
                        pltpu.VMEM((heads, chunk, chunk), F32),
                        pltpu.VMEM((heads, chunk, RET_DIM), F32),
                        pltpu.VMEM((heads, chunk, RET_DIM), F32)],
        compiler_params=_params(("arbitrary", "arbitrary", "arbitrary")),
        name="retention",
    )(log_gamma, proj, proj, proj, proj, gn_g.reshape(1, D_MODEL), s0)


def _att_kernel(sink_ref, q_ref, halo_ref, cur_ref, sg_ref, o_ref, *, cq, n_chunks, own_valid, mask_first):
    i = pl.program_id(1)
    kv = jnp.concatenate([halo_ref[...], cur_ref[...]], axis=0).astype(F32)
    n_keys = kv.shape[0]
    n_kc = n_keys // CHUNK
    n_tiles = (n_kc + 1) // 2
    lane = lax.broadcasted_iota(jnp.int32, (1, LANES), 1)
    lo = lane < ATT_HEAD_DIM
    ones_lo = jnp.broadcast_to(jnp.where(lo, 1.0, 0.0), (n_keys, LANES)).astype(BF16)
    ones_hi = jnp.broadcast_to(jnp.where(lo, 0.0, 1.0), (n_keys, LANES)).astype(BF16)
    own_ok = (lane % ATT_HEAD_DIM) < own_valid
    nt = (((1,), (1,)), ((), ()))
    pairs = ATT_GROUP // 2
    rows_q = pairs * cq
    k_off = ATT_KV_HEADS * ATT_HEAD_DIM
    users = [[c for c in range(n_chunks) if c <= 2 * u + 1 and c + 2 >= 2 * u] for u in range(n_tiles)]

    def tile_rows(a_lo, a_hi, u):
        parts = []
        for j in (2 * u, 2 * u + 1):
            for a in (a_lo, a_hi):
                parts.append(a[j * CHUNK:(j + 1) * CHUNK] if j < n_kc else jnp.zeros((CHUNK, a.shape[1]), a.dtype))
        return jnp.concatenate(parts, axis=0)

    for y in range(ATT_KV_HEADS // 2):
        ks = kv[:, y * LANES:(y + 1) * LANES]
        kr = pltpu.roll(ks, ATT_HEAD_DIM, 1)
        vs = kv[:, k_off + y * LANES:k_off + (y + 1) * LANES]
        vr = pltpu.roll(vs, ATT_HEAD_DIM, 1)
        for par in range(2):
            x = 2 * y + par
            k_even, k_odd = (ks, kr) if par == 0 else (kr, ks)
            v_even, v_odd = (vs, vr) if par == 0 else (vr, vs)
            klo = jnp.where(lo, k_even, 0.0).astype(BF16)
            khi = jnp.where(lo, 0.0, k_odd).astype(BF16)
            vlo = jnp.concatenate([jnp.where(lo, v_even, 0.0).astype(BF16), ones_lo], axis=1)
            vhi = jnp.concatenate([jnp.where(lo, 0.0, v_odd).astype(BF16), ones_hi], axis=1)
            sink_b = jnp.concatenate(
                [jnp.broadcast_to(jnp.where(lo, sink_ref[x * ATT_GROUP + 2 * p],
                                            sink_ref[x * ATT_GROUP + 2 * p + 1]), (cq, LANES))
                 for p in range(pairs)], axis=0)
            col0 = x * ATT_GROUP * ATT_HEAD_DIM
            qp = [jnp.concatenate([q_ref[c * cq:(c + 1) * cq, col0 + p * LANES:col0 + (p + 1) * LANES]
                                   for p in range(pairs)], axis=0) for c in range(n_chunks)]
            s_tiles = [lax.dot_general(jnp.concatenate([qp[c] for c in users[u]], axis=0), tile_rows(klo, khi, u),
                                       nt, preferred_element_type=F32) for u in range(n_tiles)]
            p_blocks, m_all = {}, {}
            for c in range(n_chunks):
                sj = []
                for jj in range(3):
                    u, half = (c + jj) // 2, (c + jj) % 2
                    r = users[u].index(c) * rows_q
                    sj.append(s_tiles[u][r:r + rows_q, half * LANES:(half + 1) * LANES])
                if mask_first:
                    for jj in range(2):
                        if c + jj < 2:
                            sj[jj] = jnp.where(i * n_chunks + (c + jj - 2) < 0, NEG_INF, sj[jj])
                if own_valid < CHUNK:
                    sj[2] = jnp.where(own_ok, sj[2], NEG_INF)
                m3 = jnp.maximum(jnp.maximum(sj[0], sj[1]), sj[2])
                m_even = jnp.max(jnp.where(lo, m3, NEG_INF), axis=1, keepdims=True)
                m_odd = jnp.max(jnp.where(lo, NEG_INF, m3), axis=1, keepdims=True)
                m_b = jnp.maximum(jnp.where(lo, m_even, m_odd), sink_b)
                m_all[c] = m_b
                for jj in range(3):
                    p_blocks[(c, c + jj)] = jnp.exp(sj[jj] - m_b).astype(BF16)
            zero_p = jnp.zeros((rows_q, LANES), BF16)
            o_tiles = []
            for u in range(n_tiles):
                p_u = jnp.concatenate(
                    [jnp.concatenate([p_blocks.get((c, j), zero_p) for j in (2 * u, 2 * u + 1)], axis=1)
                     for c in users[u]], axis=0)
                o_tiles.append(jnp.dot(p_u, tile_rows(vlo, vhi, u), preferred_element_type=F32))
            for c in range(n_chunks):
                oe = None
                for u in sorted({c // 2, (c + 2) // 2}):
                    r = users[u].index(c) * rows_q
                    part = o_tiles[u][r:r + rows_q]
                    oe = part if oe is None else oe + part
                den = oe[:, LANES:] + jnp.exp(sink_b - m_all[c])
                res = oe[:, :LANES] / den
                for p in range(pairs):
                    cols = slice(col0 + p * LANES, col0 + (p + 1) * LANES)
                    rows = slice(c * cq, (c + 1) * cq)
                    gate = sg_ref[rows, cols].astype(F32)
                    o_ref[rows, cols] = (gate * res[p * cq:(p + 1) * cq]).astype(BF16)


def _attention(proj, halo_src, cur_src, sinks, cq, n_chunks, own_valid, mask_first):
    B, T, _ = proj.shape
    tq = cq * n_chunks
    cur_rows = CHUNK * n_chunks
    q_blk, sg_blk = COL_QA // D_MODEL, COL_SGA // D_MODEL
    halo_per_cur = cur_rows // WINDOW if mask_first else 0

    def halo_map(b, i):
        return (b, jnp.maximum(i * halo_per_cur - 1, 0), 0) if mask_first else (b, 0, 0)

    return pl.pallas_call(
        functools.partial(_att_kernel, cq=cq, n_chunks=n_chunks, own_valid=own_valid, mask_first=mask_first),
        grid=(B, T // tq),
        in_specs=[pl.BlockSpec(memory_space=pltpu.SMEM),
                  pl.BlockSpec((None, tq, D_MODEL), lambda b, i: (b, i, q_blk)),
                  pl.BlockSpec((None, WINDOW, KV_W), halo_map),
                  pl.BlockSpec((None, cur_rows, KV_W), lambda b, i: (b, i, 0)),
                  pl.BlockSpec((None, tq, D_MODEL), lambda b, i: (b, i, sg_blk))],
        out_specs=pl.BlockSpec((None, tq, D_MODEL), lambda b, i: (b, i, 0)),
        out_shape=jax.ShapeDtypeStruct((B, T, D_MODEL), BF16),
        compiler_params=_params(("arbitrary", "arbitrary")),
        name="attention",
    )(sinks, proj, halo_src, cur_src, proj)


def _layernorm(z, g, b):
    mu = jnp.mean(z, axis=-1, keepdims=True)
    zc = z - mu
    var = jnp.mean(zc * zc, axis=-1, keepdims=True)
    return zc * lax.rsqrt(var + LN_EPS) * g + b


def _out_kernel(r_ref, a_ref, x_ref, gt_ref, sc_ref, sh_ref, w_ref, g_ref, b_ref, x1_ref, h2_ref):
    n_split = max(x_ref.shape[0] // OUT_SPLIT_ROWS, 1)
    rows_per = x_ref.shape[0] // n_split

    def mod_rows(ref, rows):
        return ref[...] if ref.shape[0] == 1 else ref[rows, :]

    for r in range(n_split):
        rows = slice(r * rows_per, (r + 1) * rows_per)
        merged = r_ref[rows, :] + a_ref[rows, :]
        mix = jnp.dot(merged, w_ref[...], preferred_element_type=F32)
        z = ALPHA * x_ref[rows, :] + mod_rows(gt_ref, rows) * mix
        x1 = _layernorm(z, g_ref[...], b_ref[...])
        x1_ref[rows, :] = x1
        h2_ref[rows, :] = (x1 * (1.0 + mod_rows(sc_ref, rows)) + mod_rows(sh_ref, rows)).astype(BF16)


def _out_proj(ret_m, att_m, x, gt, sc, sh, w_o, g, b, tm):
    B, T, _ = x.shape
    tok = pl.BlockSpec((None, tm, D_MODEL), lambda bb, t: (bb, t, 0))
    vec = pl.BlockSpec((1, D_MODEL), lambda bb, t: (0, 0))
    return pl.pallas_call(
        _out_kernel,
        grid=(B, T // tm),
        in_specs=[tok, tok, tok, _mod_spec(gt, tm), _mod_spec(sc, tm), _mod_spec(sh, tm),
                  pl.BlockSpec((D_MODEL, D_MODEL), lambda bb, t: (0, 0)), vec, vec],
        out_specs=[tok, tok],
        out_shape=[jax.ShapeDtypeStruct((B, T, D_MODEL), F32),
                   jax.ShapeDtypeStruct((B, T, D_MODEL), BF16)],
        compiler_params=_params(("arbitrary", "arbitrary")),
        name="out_proj",
    )(ret_m, att_m, x, gt, sc, sh, w_o, g.reshape(1, D_MODEL), b.reshape(1, D_MODEL))


def _ffn_kernel(h_ref, x1_ref, gt_ref, wg_ref, wu_ref, wd_ref, g_ref, b_ref, o_ref):
    f = pl.program_id(2)
    last = pl.num_programs(2) - 1
    tm = h_ref.shape[0]

    def row_blocks(rows_per):
        n_split = max(tm // rows_per, 1)
        return [slice(r * (tm // n_split), (r + 1) * (tm // n_split)) for r in range(n_split)]

    @pl.when(f == 0)
    def _():
        o_ref[...] = jnp.zeros_like(o_ref)

    def step(rows_per, finalize):
        h = h_ref[...]
        a = jnp.dot(h, wg_ref[...], preferred_element_type=F32)
        u = jnp.dot(h, wu_ref[...], preferred_element_type=F32)
        act = (a * jax.nn.sigmoid(a) * u).astype(BF16)
        for rows in row_blocks(rows_per):
            acc = o_ref[rows, :] + jnp.dot(act[rows, :], wd_ref[...], preferred_element_type=F32)
            if finalize:
                gt = gt_ref[...] if gt_ref.shape[0] == 1 else gt_ref[rows, :]
                acc = _layernorm(ALPHA * x1_ref[rows, :] + gt * acc, g_ref[...], b_ref[...])
            o_ref[rows, :] = acc

    @pl.when(f < last)
    def _():
        step(FFN_SPLIT_ROWS, False)

    @pl.when(f == last)
    def _():
        step(FFN_LAST_SPLIT_ROWS, True)


def _ffn(h2, x1, gt, wg, wu, wd, g, b, tm, tf):
    B, T, _ = x1.shape
    tok = pl.BlockSpec((None, tm, D_MODEL), lambda bb, t, f: (bb, t, 0))
    vec = pl.BlockSpec((1, D_MODEL), lambda bb, t, f: (0, 0))
    return pl.pallas_call(
        _ffn_kernel,
        grid=(B, T // tm, D_FF // tf),
        in_specs=[tok, tok, _mod_spec(gt, tm),
                  pl.BlockSpec((D_MODEL, tf), lambda bb, t, f: (0, f)),
                  pl.BlockSpec((D_MODEL, tf), lambda bb, t, f: (0, f)),
                  pl.BlockSpec((tf, D_MODEL), lambda bb, t, f: (f, 0)), vec, vec],
        out_specs=tok,
        out_shape=jax.ShapeDtypeStruct((B, T, D_MODEL), F32),
        compiler_params=_params(("arbitrary", "arbitrary", "arbitrary")),
        name="ffn",
    )(h2, x1, gt, wg, wu, wd, g.reshape(1, D_MODEL), b.reshape(1, D_MODEL))


def _largest_tile(total, cap):
    t = min(total, cap)
    while total % t:
        t //= 2
    return t


def kernel(x_prompt, x_sample, c_prompt, c_sample, cache_attn_k, cache_attn_v, state_ret, w_ada, b_ada, w_in,
           gn_g, attn_sinks, w_o, ln1_g, ln1_b, w_ffn_gate, w_ffn_up, w_ffn_down, ln2_g, ln2_b):
    B, T, _ = x_prompt.shape
    Bs, Ls, _ = x_sample.shape
    l = 0

    n_c = B + Bs
    pad = (-n_c) % 8
    c_all = jnp.concatenate([c_prompt, c_sample, jnp.zeros((pad, D_MODEL), F32)], axis=0)
    mods = _mods(c_all, w_ada[l], b_ada[l])
    mods_p = [m[:, None, :] for m in jnp.split(mods[:B], 6, axis=-1)]
    mods_s = [jnp.repeat(m, Ls, axis=0)[None] for m in jnp.split(mods[B:n_c], 6, axis=-1)]

    w = w_in[l]
    o_ka = 5 * D_MODEL
    o_gate = o_ka + KV_W
    w_main = _repack_w_in(w)
    w_kv = w[:, o_ka:o_gate].astype(BF16)

    sh_a, sc_a, gt_a, sh_f, sc_f, gt_f = mods_p
    tab_p = _rope_tables(jnp.arange(T))
    tm_proj = _largest_tile(T, 1024)
    later = ((w_ffn_gate[l], 1), (w_ffn_up[l], 1), (w_ffn_down[l], 0), (w_o[l], 1))
    if sum(wt.shape[axis] // CAST_SLAB for wt, axis in later) <= B * (T // tm_proj) * (MAIN_W // PROJ_TN):
        proj, kvb, (wg, wu, wd, wo) = _proj(x_prompt, sc_a, sh_a, tab_p, w_main, w_kv, tm_proj, cast=later)
    else:
        proj, kvb, _ = _proj(x_prompt, sc_a, sh_a, tab_p, w_main, w_kv, tm_proj)
        wg, wu, wd, wo = (wt.astype(BF16) for wt, _ in later)
    ret_m, s_p = _retention(proj, gn_g[l], jnp.zeros((B, RET_HEADS, RET_DIM, RET_DIM), F32),
                            _largest_tile(T, 256), _largest_tile(T, 8192), 1)
    att_m = _attention(proj, kvb, kvb, attn_sinks[l], CHUNK, _largest_tile(T // CHUNK, 32), CHUNK, True)
    x1, h2 = _out_proj(ret_m, att_m, x_prompt, gt_a, sc_f, sh_f, wo, ln1_g[l], ln1_b[l], _largest_tile(T, 512))
    y_p = _ffn(h2, x1, gt_f, wg, wu, wd, ln2_g[l], ln2_b[l], _largest_tile(T, 1024), FFN_TF)
    kv_p = _kv32(x_prompt[:, T - WINDOW:], mods_p[1], mods_p[0], tab_p[:, T - WINDOW:], w_kv)
    k_p = kv_p[..., :KV_W // 2].reshape(B, WINDOW, ATT_KV_HEADS, ATT_HEAD_DIM)
    v_p = kv_p[..., KV_W // 2:].reshape(B, WINDOW, ATT_KV_HEADS, ATT_HEAD_DIM)

    R = Bs * Ls
    xs = x_sample.reshape(1, R, D_MODEL)
    pos_s = jnp.tile(PAST_LEN + jnp.arange(Ls), Bs)
    cache = jnp.concatenate([cache_attn_k[l].reshape(Bs, WINDOW, KV_W // 2),
                             cache_attn_v[l].reshape(Bs, WINDOW, KV_W // 2)], axis=-1).astype(BF16)

    def att_sample(proj, kvb):
        new = jnp.pad(kvb.reshape(Bs, Ls, KV_W), ((0, 0), (0, CHUNK - Ls), (0, 0)))
        o = _attention(proj.reshape(Bs, Ls, MAIN_W), cache, new, attn_sinks[l], Ls, 1, Ls, False)
        return o.reshape(1, R, D_MODEL)

    def ret_sample(proj, s0):
        return _retention(proj.reshape(Bs, Ls, MAIN_W), gn_g[l], s0, Ls, Ls, RET_HEADS)

    sh_a, sc_a, gt_a, sh_f, sc_f, gt_f = mods_s
    tab_s = _rope_tables(pos_s)
    proj_s, kvb_s, _ = _proj(xs, sc_a, sh_a, tab_s, w_main, w_kv, R, tn=D_MODEL)
    ret_s, s_s = ret_sample(proj_s, state_ret[l])
    att_s = att_sample(proj_s, kvb_s)
    x1_s, h2_s = _out_proj(ret_s.reshape(1, R, D_MODEL), att_s, xs, gt_a, sc_f, sh_f, wo, ln1_g[l], ln1_b[l], R)
    y_s = _ffn(h2_s, x1_s, gt_f, wg, wu, wd, ln2_g[l], ln2_b[l], R, FFN_TF_SAMPLE).reshape(Bs, Ls, D_MODEL)
    kv_s = _kv32(xs, sc_a, sh_a, tab_s, w_kv).reshape(Bs, Ls, KV_W)
    k_s = kv_s[..., :KV_W // 2].reshape(Bs, Ls, ATT_KV_HEADS, ATT_HEAD_DIM)
    v_s = kv_s[..., KV_W // 2:].reshape(Bs, Ls, ATT_KV_HEADS, ATT_HEAD_DIM)

    return (y_p, y_s, k_p[None], v_p[None], s_p[None], k_s[None], v_s[None], s_s[None])
```

```python
import functools

import jax
import jax.numpy as jnp
from jax import lax
from jax.experimental import pallas as pl
from jax.experimental.pallas import tpu as pltpu

F32 = jnp.float32
BF16 = jnp.bfloat16

D_MODEL = 2048
CHUNK = 64
PAST_LEN = 1024
RET_HEADS = 8
RET_DIM = 256
RET_ROPE_THETA = 10000.0
ATT_HEADS = 32
ATT_KV_HEADS = 4
ATT_HEAD_DIM = 64
ATT_GROUP = ATT_HEADS // ATT_KV_HEADS
WINDOW = 128
ROPE_DIM = ATT_HEAD_DIM // 4
ROPE_THETA = 500000.0
D_FF = 5632
DEPTH = 1
ALPHA = (2.0 * DEPTH) ** 0.25
LN_EPS = 1e-5
NEG_INF = -1e30

LANES = 128
KV_W = 2 * ATT_KV_HEADS * ATT_HEAD_DIM
MAIN_W = 7 * D_MODEL
PROJ_TN = 1024
PROJ_SPLIT_ROWS = 256
PROJ_LIGHT_SPLIT_ROWS = 512
CAST_SLAB = 128
FFN_TF = 512
FFN_TF_SAMPLE = 1408
FFN_SPLIT_ROWS = 512
FFN_LAST_SPLIT_ROWS = 256
OUT_SPLIT_ROWS = 256
VMEM_LIMIT = 60 * 1024 * 1024

COL_QR, COL_KR, COL_VR, COL_GR, COL_QA, COL_SGR, COL_SGA = (i * D_MODEL for i in range(7))


def _params(sem):
    return pltpu.CompilerParams(dimension_semantics=sem, vmem_limit_bytes=VMEM_LIMIT)


def _mods_kernel(c_ref, w_ref, b_ref, o_ref):
    c = c_ref[...]
    a = (c * jax.nn.sigmoid(c)).astype(BF16)
    o_ref[...] = jnp.dot(a, w_ref[...].astype(BF16), preferred_element_type=F32) + b_ref[...]


def _mods(c_all, w_ada, b_ada):
    rows = c_all.shape[0]
    n_out = w_ada.shape[1]
    tn = 1024
    return pl.pallas_call(
        _mods_kernel,
        grid=(n_out // tn,),
        in_specs=[pl.BlockSpec((rows, D_MODEL), lambda n: (0, 0)),
                  pl.BlockSpec((D_MODEL, tn), lambda n: (0, n)),
                  pl.BlockSpec((1, tn), lambda n: (0, n))],
        out_specs=pl.BlockSpec((rows, tn), lambda n: (0, n)),
        out_shape=jax.ShapeDtypeStruct((rows, n_out), F32),
        compiler_params=_params(("arbitrary",)),
        name="mods",
    )(c_all, w_ada, b_ada.reshape(1, n_out))


def _rope_tables(pos):
    posf = pos.astype(F32)[:, None]
    half_r = RET_DIM // 2
    inv_r = 1.0 / (RET_ROPE_THETA ** (jnp.arange(half_r, dtype=F32) / half_r))
    ang_r = posf * inv_r[None, :]
    half_a = ROPE_DIM // 2
    inv_a = 1.0 / (ROPE_THETA ** (jnp.arange(half_a, dtype=F32) / half_a))
    d = jnp.arange(LANES) % ATT_HEAD_DIM
    ang_a = posf * inv_a[None, :]
    cos_a = jnp.tile(jnp.cos(ang_a), (1, LANES // half_a))
    sin_a = jnp.tile(jnp.sin(ang_a), (1, LANES // half_a))
    ca = jnp.where(d[None, :] < ROPE_DIM, cos_a, 1.0)
    s1 = jnp.where(d[None, :] < half_a, -sin_a, 0.0)
    s2 = jnp.where((d[None, :] >= half_a) & (d[None, :] < ROPE_DIM), sin_a, 0.0)
    return jnp.stack([jnp.cos(ang_r), jnp.sin(ang_r), ca, s1, s2])


def _rot_att(x, ca, s1, s2):
    half = ROPE_DIM // 2
    return x * ca + pltpu.roll(x, LANES - half, 1) * s1 + pltpu.roll(x, half, 1) * s2


def _proj_kernel(*refs, cast_plan):
    n_cast = len(cast_plan)
    x_ref, sc_ref, sh_ref, tab_ref, w_ref, wkv_ref = refs[:6]
    cast_in = refs[6:6 + n_cast]
    o_ref, kv_ref, kv32_ref = refs[6 + n_cast:9 + n_cast]
    cast_out = refs[9 + n_cast:9 + 2 * n_cast]
    h_ref = refs[-1]
    n = pl.program_id(2)
    tm = x_ref.shape[0]

    step = (pl.program_id(0) * pl.num_programs(1) + pl.program_id(1)) * pl.num_programs(2) + n
    for (first, count), src_ref, dst_ref in zip(cast_plan, cast_in, cast_out):
        @pl.when((step >= first) & (step < first + count))
        def _():
            dst_ref[...] = src_ref[...].astype(BF16)

    def blocks(rows_per):
        n_split = max(tm // rows_per, 1)
        return [slice(r * (tm // n_split), (r + 1) * (tm // n_split)) for r in range(n_split)]

    row_blocks = blocks(PROJ_SPLIT_ROWS)
    light_blocks = blocks(PROJ_LIGHT_SPLIT_ROWS)
    tn = w_ref.shape[1]

    def mod_rows(ref, rows):
        return ref[...] if ref.shape[0] == 1 else ref[rows, :]

    def att_tabs(rows):
        return tab_ref[2, rows, :], tab_ref[3, rows, :], tab_ref[4, rows, :]

    def modulate_and_kv(rows):
        h = x_ref[rows, :] * (1.0 + mod_rows(sc_ref, rows)) + mod_rows(sh_ref, rows)
        h_ref[rows, :] = h.astype(BF16)
        kv = jnp.dot(h_ref[rows, :], wkv_ref[...], preferred_element_type=F32)
        ca, s1, s2 = att_tabs(rows)
        tail0 = tm - kv32_ref.shape[0]
        keep = slice(max(rows.start, tail0), rows.stop)
        for s in range(KV_W // LANES):
            xs = kv[:, s * LANES:(s + 1) * LANES]
            if s < KV_W // LANES // 2:
                xs = _rot_att(xs, ca, s1, s2)
            kv_ref[rows, s * LANES:(s + 1) * LANES] = xs.astype(BF16)
            if keep.start < keep.stop:
                @pl.when(pl.program_id(1) == pl.num_programs(1) - 1)
                def _():
                    kv32_ref[keep.start - tail0:keep.stop - tail0, s * LANES:(s + 1) * LANES] = (
                        xs[keep.start - rows.start:keep.stop - rows.start])

    def matmul(rows):
        return jnp.dot(h_ref[rows, :], w_ref[...], preferred_element_type=F32)

    def rot_ret_rows(rows, scale):
        acc = matmul(rows)
        cr, sr = tab_ref[0, rows, :], tab_ref[1, rows, :]
        for j in range(tn // RET_DIM):
            a = j * RET_DIM
            x1 = acc[:, a:a + LANES]
            x2 = acc[:, a + LANES:a + 2 * LANES]
            o_ref[rows, a:a + LANES] = ((x1 * cr - x2 * sr) * scale).astype(BF16)
            o_ref[rows, a + LANES:a + 2 * LANES] = ((x1 * sr + x2 * cr) * scale).astype(BF16)

    def rot_ret(scale):
        for rows in light_blocks:
            rot_ret_rows(rows, scale)

    region = n // (D_MODEL // tn)

    @pl.when(n == 0)
    def _():
        for rows in row_blocks:
            modulate_and_kv(rows)
            rot_ret_rows(rows, 1.0)

    @pl.when(((region == 0) & (n > 0)) | (region == 1))
    def _():
        rot_ret(jnp.where(region == 0, 1.0, RET_DIM ** -0.5))

    @pl.when(region == 2)
    def _():
        for rows in light_blocks:
            o_ref[rows, :] = matmul(rows).astype(BF16)

    @pl.when(region == 3)
    def _():
        for rows in row_blocks:
            acc = matmul(rows)
            o_ref[rows, :] = (acc * jax.nn.sigmoid(acc)).astype(BF16)

    @pl.when(region == 4)
    def _():
        for rows in row_blocks:
            acc = matmul(rows)
            ca, s1, s2 = att_tabs(rows)
            for s in range(tn // LANES):
                xs = _rot_att(acc[:, s * LANES:(s + 1) * LANES], ca, s1, s2)
                o_ref[rows, s * LANES:(s + 1) * LANES] = (xs * ATT_HEAD_DIM ** -0.5).astype(BF16)

    @pl.when(region >= 5)
    def _():
        for rows in row_blocks:
            o_ref[rows, :] = jax.nn.sigmoid(matmul(rows)).astype(BF16)


def _mod_spec(mod, tm):
    if mod.shape[1] == 1:
        return pl.BlockSpec((None, 1, D_MODEL), lambda b, t, *_: (b, 0, 0))
    return pl.BlockSpec((None, tm, D_MODEL), lambda b, t, *_: (b, t, 0))


def _proj(x, sc, sh, tab, w_main, w_kv, tm, kv32_rows, tn=PROJ_TN, cast=()):
    B, T, _ = x.shape
    nt = T // tm
    n_tiles = MAIN_W // tn
    cast_plan, cast_specs, cast_shapes, first = [], [], [], 0
    for wt, axis in cast:
        count = wt.shape[axis] // CAST_SLAB
        block = tuple(CAST_SLAB if d == axis else wt.shape[d] for d in range(2))

        def slab_map(b, t, n, first=first, count=count, axis=axis):
            slab = jnp.clip((b * nt + t) * n_tiles + n - first, 0, count - 1)
            return (slab, 0) if axis == 0 else (0, slab)

        cast_plan.append((first, count))
        cast_specs.append(pl.BlockSpec(block, slab_map))
        cast_shapes.append(jax.ShapeDtypeStruct(wt.shape, BF16))
        first += count
    outs = pl.pallas_call(
        functools.partial(_proj_kernel, cast_plan=tuple(cast_plan)),
        grid=(B, nt, n_tiles),
        in_specs=[pl.BlockSpec((None, tm, D_MODEL), lambda b, t, n: (b, t, 0)),
                  _mod_spec(sc, tm), _mod_spec(sh, tm),
                  pl.BlockSpec((5, tm, LANES), lambda b, t, n: (0, t, 0)),
                  pl.BlockSpec((D_MODEL, tn), lambda b, t, n: (0, n)),
                  pl.BlockSpec((D_MODEL, KV_W), lambda b, t, n: (0, 0), pipeline_mode=pl.Buffered(1))] + cast_specs,
        out_specs=[pl.BlockSpec((None, tm, tn), lambda b, t, n: (b, t, n)),
                   pl.BlockSpec((None, tm, KV_W), lambda b, t, n: (b, t, 0)),
                   pl.BlockSpec((None, kv32_rows, KV_W), lambda b, t, n: (b, 0, 0))] + cast_specs,
        out_shape=[jax.ShapeDtypeStruct((B, T, MAIN_W), BF16),
                   jax.ShapeDtypeStruct((B, T, KV_W), BF16),
                   jax.ShapeDtypeStruct((B, kv32_rows, KV_W), F32)] + cast_shapes,
        scratch_shapes=[pltpu.VMEM((tm, D_MODEL), BF16)],
        compiler_params=_params(("arbitrary", "arbitrary", "arbitrary")),
        name="proj",
    )(x, sc, sh, tab, w_main, w_kv, *[wt for wt, _ in cast])
    return outs[0], outs[1], outs[2], outs[3:]


def _repack_kernel(w_ref, o_ref):
    o_ref[...] = w_ref[...].astype(BF16)


def _repack_w_in(w):
    kv_blk = 5 * D_MODEL // KV_W
    return pl.pallas_call(
        _repack_kernel,
        grid=(MAIN_W // KV_W,),
        in_specs=[pl.BlockSpec((D_MODEL, KV_W), lambda n: (0, jnp.where(n >= kv_blk, n + 1, n)))],
        out_specs=pl.BlockSpec((D_MODEL, KV_W), lambda n: (0, n)),
        out_shape=jax.ShapeDtypeStruct((D_MODEL, MAIN_W), BF16),
        compiler_params=_params(("arbitrary",)),
        name="repack_w_in",
    )(w)


def _ret_kernel(lg_ref, q_ref, k_ref, v_ref, gs_ref, sg_ref, gn_ref, s0_ref, o_ref, sout_ref,
                s_ref, dm_ref, qd_ref, kd_ref, *, chunk, n_chunks, heads):
    hg = pl.program_id(1)
    t = pl.program_id(2)
    nt = (((1,), (1,)), ((), ()))
    tn = (((0,), (0,)), ((), ()))
    for hl in range(heads):
        lg = lg_ref[hg * heads + hl]
        cols = slice(hl * RET_DIM, (hl + 1) * RET_DIM)

        @pl.when(t == 0)
        def _():
            s_ref[hl] = s0_ref[hl]
            i = lax.broadcasted_iota(jnp.int32, (chunk, chunk), 0)
            j = lax.broadcasted_iota(jnp.int32, (chunk, chunk), 1)
            diff = (i - j).astype(F32)
            dm_ref[hl] = jnp.where(diff >= 0, jnp.exp(lg * jnp.maximum(diff, 0.0)), 0.0)
            r = lax.broadcasted_iota(jnp.int32, (chunk, RET_DIM), 0).astype(F32)
            qd_ref[hl] = jnp.exp(lg * (r + 1.0))
            kd_ref[hl] = jnp.exp(lg * (chunk - 1.0 - r))

        g_chunk = jnp.exp(jnp.full((1, RET_DIM), lg * chunk, F32))
        for c in range(n_chunks):
            rows = pl.ds(c * chunk, chunk)
            q, k, v = q_ref[rows, cols], k_ref[rows, cols], v_ref[rows, cols]
            inner = (lax.dot_general(q, k, nt, preferred_element_type=F32) * dm_ref[hl]).astype(BF16)
            s_prev = s_ref[hl]
            if chunk % LANES == 0:
                q_dec = (q.astype(F32) * qd_ref[hl]).astype(BF16)
                o = jnp.dot(jnp.concatenate([inner, q_dec], axis=1),
                            jnp.concatenate([v, s_prev.astype(BF16)], axis=0), preferred_element_type=F32)
            else:
                o = jnp.dot(inner, v, preferred_element_type=F32)
                o = o + jnp.dot(q, s_prev.astype(BF16), preferred_element_type=F32) * qd_ref[hl]
            kdec = (k.astype(F32) * kd_ref[hl]).astype(BF16)
            s_ref[hl] = s_prev * g_chunk + lax.dot_general(kdec, v, tn, preferred_element_type=F32)
            mu = jnp.mean(o, axis=-1, keepdims=True)
            xc = o - mu
            var = jnp.mean(xc * xc, axis=-1, keepdims=True)
            gn = xc * lax.rsqrt(var + LN_EPS) * gn_ref[:, cols]
            out = sg_ref[rows, cols].astype(F32) * (gs_ref[rows, cols].astype(F32) * gn)
            o_ref[rows, cols] = out.astype(BF16)

    @pl.when(t == pl.num_programs(2) - 1)
    def _():
        sout_ref[...] = s_ref[...]


def _retention(proj, gn_g, s0, chunk, tt, heads):
    B, T, _ = proj.shape
    n_chunks = tt // chunk
    width = heads * RET_DIM
    log_gamma = jnp.log1p(-jnp.exp2(-5.0 - jnp.arange(RET_HEADS, dtype=F32)))

    def col(off):
        base = off // width
        return pl.BlockSpec((None, tt, width), lambda b, h, t: (b, t, base + h))

    state = pl.BlockSpec((None, heads, RET_DIM, RET_DIM), lambda b, h, t: (b, h, 0, 0))
    return pl.pallas_call(
        functools.partial(_ret_kernel, chunk=chunk, n_chunks=n_chunks, heads=heads),
        grid=(B, RET_HEADS // heads, T // tt),
        in_specs=[pl.BlockSpec(memory_space=pltpu.SMEM),
                  col(COL_QR), col(COL_KR), col(COL_VR), col(COL_GR), col(COL_SGR),
                  pl.BlockSpec((1, width), lambda b, h, t: (0, h)), state],
        out_specs=[pl.BlockSpec((None, tt, width), lambda b, h, t: (b, t, h)), state],
        out_shape=[jax.ShapeDtypeStruct((B, T, D_MODEL), BF16),
                   jax.ShapeDtypeStruct((B, RET_HEADS, RET_DIM, RET_DIM), F32)],
        scratch_shapes=[pltpu.VMEM((heads, RET_DIM, RET_DIM), F32),
                        pltpu.VMEM((heads, chunk, chunk), F32),
                        pltpu.VMEM((heads, chunk, RET_DIM), F32),
                        pltpu.VMEM((heads, chunk, RET_DIM), F32)],
        compiler_params=_params(("arbitrary", "arbitrary", "arbitrary")),
        name="retention",
    )(log_gamma, proj, proj, proj, proj, proj, gn_g.reshape(1, D_MODEL), s0)


def _att_kernel(sink_ref, q_ref, halo_ref, cur_ref, sg_ref, o_ref, *, cq, n_chunks, own_valid, mask_first):
    i = pl.program_id(1)
    kv = jnp.concatenate([halo_ref[...], cur_ref[...]], axis=0).astype(F32)
    n_keys = kv.shape[0]
    n_kc = n_keys // CHUNK
    n_tiles = (n_kc + 1) // 2
    lane = lax.broadcasted_iota(jnp.int32, (1, LANES), 1)
    lo = lane < ATT_HEAD_DIM
    ones_lo = jnp.broadcast_to(jnp.where(lo, 1.0, 0.0), (n_keys, LANES)).astype(BF16)
    ones_hi = jnp.broadcast_to(jnp.where(lo, 0.0, 1.0), (n_keys, LANES)).astype(BF16)
    own_ok = (lane % ATT_HEAD_DIM) < own_valid
    nt = (((1,), (1,)), ((), ()))
    pairs = ATT_GROUP // 2
    rows_q = pairs * cq
    k_off = ATT_KV_HEADS * ATT_HEAD_DIM
    users = [[c for c in range(n_chunks) if c <= 2 * u + 1 and c + 2 >= 2 * u] for u in range(n_tiles)]

    def tile_rows(a_lo, a_hi, u):
        parts = []
        for j in (2 * u, 2 * u + 1):
            for a in (a_lo, a_hi):
                parts.append(a[j * CHUNK:(j + 1) * CHUNK] if j < n_kc else jnp.zeros((CHUNK, a.shape[1]), a.dtype))
        return jnp.concatenate(parts, axis=0)

    for y in range(ATT_KV_HEADS // 2):
        ks = kv[:, y * LANES:(y + 1) * LANES]
        kr = pltpu.roll(ks, ATT_HEAD_DIM, 1)
        vs = kv[:, k_off + y * LANES:k_off + (y + 1) * LANES]
        vr = pltpu.roll(vs, ATT_HEAD_DIM, 1)
        for par in range(2):
            x = 2 * y + par
            k_even, k_odd = (ks, kr) if par == 0 else (kr, ks)
            v_even, v_odd = (vs, vr) if par == 0 else (vr, vs)
            klo = jnp.where(lo, k_even, 0.0).astype(BF16)
            khi = jnp.where(lo, 0.0, k_odd).astype(BF16)
            vlo = jnp.concatenate([jnp.where(lo, v_even, 0.0).astype(BF16), ones_lo], axis=1)
            vhi = jnp.concatenate([jnp.where(lo, 0.0, v_odd).astype(BF16), ones_hi], axis=1)
            sink_b = jnp.concatenate(
                [jnp.broadcast_to(jnp.where(lo, sink_ref[x * ATT_GROUP + 2 * p],
                                            sink_ref[x * ATT_GROUP + 2 * p + 1]), (cq, LANES))
                 for p in range(pairs)], axis=0)
            col0 = x * ATT_GROUP * ATT_HEAD_DIM
            qp = [jnp.concatenate([q_ref[c * cq:(c + 1) * cq, col0 + p * LANES:col0 + (p + 1) * LANES]
                                   for p in range(pairs)], axis=0) for c in range(n_chunks)]
            s_tiles = [lax.dot_general(jnp.concatenate([qp[c] for c in users[u]], axis=0), tile_rows(klo, khi, u),
                                       nt, preferred_element_type=F32) for u in range(n_tiles)]
            p_blocks, m_all = {}, {}
            for c in range(n_chunks):
                sj = []
                for jj in range(3):
                    u, half = (c + jj) // 2, (c + jj) % 2
                    r = users[u].index(c) * rows_q
                    sj.append(s_tiles[u][r:r + rows_q, half * LANES:(half + 1) * LANES])
                if mask_first:
                    for jj in range(2):
                        if c + jj < 2:
                            sj[jj] = jnp.where(i * n_chunks + (c + jj - 2) < 0, NEG_INF, sj[jj])
                if own_valid < CHUNK:
                    sj[2] = jnp.where(own_ok, sj[2], NEG_INF)
                m3 = jnp.maximum(jnp.maximum(sj[0], sj[1]), sj[2])
                m_even = jnp.max(jnp.where(lo, m3, NEG_INF), axis=1, keepdims=True)
                m_odd = jnp.max(jnp.where(lo, NEG_INF, m3), axis=1, keepdims=True)
                m_b = jnp.maximum(jnp.where(lo, m_even, m_odd), sink_b)
                m_all[c] = m_b
                for jj in range(3):
                    p_blocks[(c, c + jj)] = jnp.exp(sj[jj] - m_b).astype(BF16)
            zero_p = jnp.zeros((rows_q, LANES), BF16)
            o_tiles = []
            for u in range(n_tiles):
                p_u = jnp.concatenate(
                    [jnp.concatenate([p_blocks.get((c, j), zero_p) for j in (2 * u, 2 * u + 1)], axis=1)
                     for c in users[u]], axis=0)
                o_tiles.append(jnp.dot(p_u, tile_rows(vlo, vhi, u), preferred_element_type=F32))
            for c in range(n_chunks):
                oe = None
                for u in sorted({c // 2, (c + 2) // 2}):
                    r = users[u].index(c) * rows_q
                    part = o_tiles[u][r:r + rows_q]
                    oe = part if oe is None else oe + part
                den = oe[:, LANES:] + jnp.exp(sink_b - m_all[c])
                res = oe[:, :LANES] / den
                for p in range(pairs):
                    cols = slice(col0 + p * LANES, col0 + (p + 1) * LANES)
                    rows = slice(c * cq, (c + 1) * cq)
                    gate = sg_ref[rows, cols].astype(F32)
                    o_ref[rows, cols] = (gate * res[p * cq:(p + 1) * cq]).astype(BF16)


def _attention(proj, halo_src, cur_src, sinks, cq, n_chunks, own_valid, mask_first):
    B, T, _ = proj.shape
    tq = cq * n_chunks
    cur_rows = CHUNK * n_chunks
    q_blk, sg_blk = COL_QA // D_MODEL, COL_SGA // D_MODEL
    halo_per_cur = cur_rows // WINDOW if mask_first else 0

    def halo_map(b, i):
        return (b, jnp.maximum(i * halo_per_cur - 1, 0), 0) if mask_first else (b, 0, 0)

    return pl.pallas_call(
        functools.partial(_att_kernel, cq=cq, n_chunks=n_chunks, own_valid=own_valid, mask_first=mask_first),
        grid=(B, T // tq),
        in_specs=[pl.BlockSpec(memory_space=pltpu.SMEM),
                  pl.BlockSpec((None, tq, D_MODEL), lambda b, i: (b, i, q_blk)),
                  pl.BlockSpec((None, WINDOW, KV_W), halo_map),
                  pl.BlockSpec((None, cur_rows, KV_W), lambda b, i: (b, i, 0)),
                  pl.BlockSpec((None, tq, D_MODEL), lambda b, i: (b, i, sg_blk))],
        out_specs=pl.BlockSpec((None, tq, D_MODEL), lambda b, i: (b, i, 0)),
        out_shape=jax.ShapeDtypeStruct((B, T, D_MODEL), BF16),
        compiler_params=_params(("arbitrary", "arbitrary")),
        name="attention",
    )(sinks, proj, halo_src, cur_src, proj)


def _layernorm(z, g, b):
    mu = jnp.mean(z, axis=-1, keepdims=True)
    zc = z - mu
    var = jnp.mean(zc * zc, axis=-1, keepdims=True)
    return zc * lax.rsqrt(var + LN_EPS) * g + b


def _out_kernel(r_ref, a_ref, x_ref, gt_ref, sc_ref, sh_ref, w_ref, g_ref, b_ref, x1_ref, h2_ref):
    n_split = max(x_ref.shape[0] // OUT_SPLIT_ROWS, 1)
    rows_per = x_ref.shape[0] // n_split

    def mod_rows(ref, rows):
        return ref[...] if ref.shape[0] == 1 else ref[rows, :]

    for r in range(n_split):
        rows = slice(r * rows_per, (r + 1) * rows_per)
        merged = r_ref[rows, :] + a_ref[rows, :]
        mix = jnp.dot(merged, w_ref[...], preferred_element_type=F32)
        z = ALPHA * x_ref[rows, :] + mod_rows(gt_ref, rows) * mix
        x1 = _layernorm(z, g_ref[...], b_ref[...])
        x1_ref[rows, :] = x1
        h2_ref[rows, :] = (x1 * (1.0 + mod_rows(sc_ref, rows)) + mod_rows(sh_ref, rows)).astype(BF16)


def _out_proj(ret_m, att_m, x, gt, sc, sh, w_o, g, b, tm):
    B, T, _ = x.shape
    tok = pl.BlockSpec((None, tm, D_MODEL), lambda bb, t: (bb, t, 0))
    vec = pl.BlockSpec((1, D_MODEL), lambda bb, t: (0, 0))
    return pl.pallas_call(
        _out_kernel,
        grid=(B, T // tm),
        in_specs=[tok, tok, tok, _mod_spec(gt, tm), _mod_spec(sc, tm), _mod_spec(sh, tm),
                  pl.BlockSpec((D_MODEL, D_MODEL), lambda bb, t: (0, 0)), vec, vec],
        out_specs=[tok, tok],
        out_shape=[jax.ShapeDtypeStruct((B, T, D_MODEL), F32),
                   jax.ShapeDtypeStruct((B, T, D_MODEL), BF16)],
        compiler_params=_params(("arbitrary", "arbitrary")),
        name="out_proj",
    )(ret_m, att_m, x, gt, sc, sh, w_o, g.reshape(1, D_MODEL), b.reshape(1, D_MODEL))


def _ffn_kernel(h_ref, x1_ref, gt_ref, wg_ref, wu_ref, wd_ref, g_ref, b_ref, o_ref):
    f = pl.program_id(2)
    last = pl.num_programs(2) - 1
    tm = h_ref.shape[0]

    def row_blocks(rows_per):
        n_split = max(tm // rows_per, 1)
        return [slice(r * (tm // n_split), (r + 1) * (tm // n_split)) for r in range(n_split)]

    @pl.when(f == 0)
    def _():
        o_ref[...] = jnp.zeros_like(o_ref)

    def step(rows_per, finalize):
        h = h_ref[...]
        a = jnp.dot(h, wg_ref[...], preferred_element_type=F32)
        u = jnp.dot(h, wu_ref[...], preferred_element_type=F32)
        act = (a * jax.nn.sigmoid(a) * u).astype(BF16)
        for rows in row_blocks(rows_per):
            acc = o_ref[rows, :] + jnp.dot(act[rows, :], wd_ref[...], preferred_element_type=F32)
            if finalize:
                gt = gt_ref[...] if gt_ref.shape[0] == 1 else gt_ref[rows, :]
                acc = _layernorm(ALPHA * x1_ref[rows, :] + gt * acc, g_ref[...], b_ref[...])
            o_ref[rows, :] = acc

    @pl.when(f < last)
    def _():
        step(FFN_SPLIT_ROWS, False)

    @pl.when(f == last)
    def _():
        step(FFN_LAST_SPLIT_ROWS, True)


def _ffn(h2, x1, gt, wg, wu, wd, g, b, tm, tf):
    B, T, _ = x1.shape
    tok = pl.BlockSpec((None, tm, D_MODEL), lambda bb, t, f: (bb, t, 0))
    vec = pl.BlockSpec((1, D_MODEL), lambda bb, t, f: (0, 0))
    return pl.pallas_call(
        _ffn_kernel,
        grid=(B, T // tm, D_FF // tf),
        in_specs=[tok, tok, _mod_spec(gt, tm),
                  pl.BlockSpec((D_MODEL, tf), lambda bb, t, f: (0, f)),
                  pl.BlockSpec((D_MODEL, tf), lambda bb, t, f: (0, f)),
                  pl.BlockSpec((tf, D_MODEL), lambda bb, t, f: (f, 0)), vec, vec],
        out_specs=tok,
        out_shape=jax.ShapeDtypeStruct((B, T, D_MODEL), F32),
        compiler_params=_params(("arbitrary", "arbitrary", "arbitrary")),
        name="ffn",
    )(h2, x1, gt, wg, wu, wd, g.reshape(1, D_MODEL), b.reshape(1, D_MODEL))


def _largest_tile(total, cap):
    t = min(total, cap)
    while total % t:
        t //= 2
    return t


def kernel(x_prompt, x_sample, c_prompt, c_sample, cache_attn_k, cache_attn_v, state_ret, w_ada, b_ada, w_in,
           gn_g, attn_sinks, w_o, ln1_g, ln1_b, w_ffn_gate, w_ffn_up, w_ffn_down, ln2_g, ln2_b):
    B, T, _ = x_prompt.shape
    Bs, Ls, _ = x_sample.shape
    l = 0

    n_c = B + Bs
    pad = (-n_c) % 8
    c_all = jnp.concatenate([c_prompt, c_sample, jnp.zeros((pad, D_MODEL), F32)], axis=0)
    mods = _mods(c_all, w_ada[l], b_ada[l])
    mods_p = [m[:, None, :] for m in jnp.split(mods[:B], 6, axis=-1)]
    mods_s = [jnp.repeat(m, Ls, axis=0)[None] for m in jnp.split(mods[B:n_c], 6, axis=-1)]

    w = w_in[l]
    o_ka = 5 * D_MODEL
    o_gate = o_ka + KV_W
    w_main = _repack_w_in(w)
    w_kv = w[:, o_ka:o_gate].astype(BF16)

    sh_a, sc_a, gt_a, sh_f, sc_f, gt_f = mods_p
    tab_p = _rope_tables(jnp.arange(T))
    tm_proj = _largest_tile(T, 1024)
    later = ((w_ffn_gate[l], 1), (w_ffn_up[l], 1), (w_ffn_down[l], 0), (w_o[l], 1))
    if sum(wt.shape[axis] // CAST_SLAB for wt, axis in later) <= B * (T // tm_proj) * (MAIN_W // PROJ_TN):
        proj, kvb, kv_p, (wg, wu, wd, wo) = _proj(x_prompt, sc_a, sh_a, tab_p, w_main, w_kv, tm_proj, WINDOW, cast=later)
    else:
        proj, kvb, kv_p, _ = _proj(x_prompt, sc_a, sh_a, tab_p, w_main, w_kv, tm_proj, WINDOW)
        wg, wu, wd, wo = (wt.astype(BF16) for wt, _ in later)
    ret_m, s_p = _retention(proj, gn_g[l], jnp.zeros((B, RET_HEADS, RET_DIM, RET_DIM), F32),
                            _largest_tile(T, 256), _largest_tile(T, 8192), 1)
    att_m = _attention(proj, kvb, kvb, attn_sinks[l], CHUNK, _largest_tile(T // CHUNK, 32), CHUNK, True)
    x1, h2 = _out_proj(ret_m, att_m, x_prompt, gt_a, sc_f, sh_f, wo, ln1_g[l], ln1_b[l], _largest_tile(T, 512))
    y_p = _ffn(h2, x1, gt_f, wg, wu, wd, ln2_g[l], ln2_b[l], _largest_tile(T, 1024), FFN_TF)
    k_p = kv_p[..., :KV_W // 2].reshape(B, WINDOW, ATT_KV_HEADS, ATT_HEAD_DIM)
    v_p = kv_p[..., KV_W // 2:].reshape(B, WINDOW, ATT_KV_HEADS, ATT_HEAD_DIM)

    R = Bs * Ls
    xs = x_sample.reshape(1, R, D_MODEL)
    pos_s = jnp.tile(PAST_LEN + jnp.arange(Ls), Bs)
    cache = jnp.concatenate([cache_attn_k[l].reshape(Bs, WINDOW, KV_W // 2),
                             cache_attn_v[l].reshape(Bs, WINDOW, KV_W // 2)], axis=-1).astype(BF16)

    def att_sample(proj, kvb):
        new = jnp.pad(kvb.reshape(Bs, Ls, KV_W), ((0, 0), (0, CHUNK - Ls), (0, 0)))
        o = _attention(proj.reshape(Bs, Ls, MAIN_W), cache, new, attn_sinks[l], Ls, 1, Ls, False)
        return o.reshape(1, R, D_MODEL)

    def ret_sample(proj, s0):
        return _retention(proj.reshape(Bs, Ls, MAIN_W), gn_g[l], s0, Ls, Ls, RET_HEADS)

    sh_a, sc_a, gt_a, sh_f, sc_f, gt_f = mods_s
    tab_s = _rope_tables(pos_s)
    proj_s, kvb_s, kv_s, _ = _proj(xs, sc_a, sh_a, tab_s, w_main, w_kv, R, R, tn=D_MODEL)
    ret_s, s_s = ret_sample(proj_s, state_ret[l])
    att_s = att_sample(proj_s, kvb_s)
    x1_s, h2_s = _out_proj(ret_s.reshape(1, R, D_MODEL), att_s, xs, gt_a, sc_f, sh_f, wo, ln1_g[l], ln1_b[l], R)
    y_s = _ffn(h2_s, x1_s, gt_f, wg, wu, wd, ln2_g[l], ln2_b[l], R, FFN_TF_SAMPLE).reshape(Bs, Ls, D_MODEL)
    kv_s = kv_s.reshape(Bs, Ls, KV_W)
    k_s = kv_s[..., :KV_W // 2].reshape(Bs, Ls, ATT_KV_HEADS, ATT_HEAD_DIM)
    v_s = kv_s[..., KV_W // 2:].reshape(Bs, Ls, ATT_KV_HEADS, ATT_HEAD_DIM)

    return (y_p, y_s, k_p[None], v_p[None], s_p[None], k_s[None], v_s[None], s_s[None])
```

```python
import functools

import jax
import jax.numpy as jnp
from jax import lax
from jax.experimental import pallas as pl
from jax.experimental.pallas import tpu as pltpu

F32 = jnp.float32
BF16 = jnp.bfloat16

D_MODEL = 2048
CHUNK = 64
PAST_LEN = 1024
RET_HEADS = 8
RET_DIM = 256
RET_ROPE_THETA = 10000.0
ATT_HEADS = 32
ATT_KV_HEADS = 4
ATT_HEAD_DIM = 64
ATT_GROUP = ATT_HEADS // ATT_KV_HEADS
WINDOW = 128
ROPE_DIM = ATT_HEAD_DIM // 4
ROPE_THETA = 500000.0
D_FF = 5632
DEPTH = 1
ALPHA = (2.0 * DEPTH) ** 0.25
LN_EPS = 1e-5
NEG_INF = -1e30

LANES = 128
KV_W = 2 * ATT_KV_HEADS * ATT_HEAD_DIM
MAIN_W = 7 * D_MODEL
PROJ_TN = 1024
PROJ_SPLIT_ROWS = 256
CAST_SLAB = 128
FFN_TF = 512
MODS_TN = 2048
FFN_SPLIT_ROWS = 512
FFN_LAST_SPLIT_ROWS = 256
OUT_SPLIT_ROWS = 256
VMEM_LIMIT = 60 * 1024 * 1024

COL_QR, COL_KR, COL_VR, COL_GR, COL_QA, COL_SGR, COL_SGA = (i * D_MODEL for i in range(7))


def _params(sem):
    return pltpu.CompilerParams(dimension_semantics=sem, vmem_limit_bytes=VMEM_LIMIT)


def _mods_kernel(c_ref, w_ref, b_ref, o_ref):
    c = c_ref[...]
    a = (c * jax.nn.sigmoid(c)).astype(BF16)
    o_ref[...] = jnp.dot(a, w_ref[...].astype(BF16), preferred_element_type=F32) + b_ref[...]


def _mods(c_all, w_ada, b_ada):
    rows = c_all.shape[0]
    n_out = w_ada.shape[1]
    tn = MODS_TN
    return pl.pallas_call(
        _mods_kernel,
        grid=(n_out // tn,),
        in_specs=[pl.BlockSpec((rows, D_MODEL), lambda n: (0, 0)),
                  pl.BlockSpec((D_MODEL, tn), lambda n: (0, n)),
                  pl.BlockSpec((1, tn), lambda n: (0, n))],
        out_specs=pl.BlockSpec((rows, tn), lambda n: (0, n)),
        out_shape=jax.ShapeDtypeStruct((rows, n_out), F32),
        compiler_params=_params(("arbitrary",)),
        name="mods",
    )(c_all, w_ada, b_ada.reshape(1, n_out))


def _rope_tables(pos):
    posf = pos.astype(F32)[:, None]
    half_r = RET_DIM // 2
    inv_r = 1.0 / (RET_ROPE_THETA ** (jnp.arange(half_r, dtype=F32) / half_r))
    ang_r = posf * inv_r[None, :]
    half_a = ROPE_DIM // 2
    inv_a = 1.0 / (ROPE_THETA ** (jnp.arange(half_a, dtype=F32) / half_a))
    d = jnp.arange(LANES) % ATT_HEAD_DIM
    ang_a = posf * inv_a[None, :]
    cos_a = jnp.tile(jnp.cos(ang_a), (1, LANES // half_a))
    sin_a = jnp.tile(jnp.sin(ang_a), (1, LANES // half_a))
    ca = jnp.where(d[None, :] < ROPE_DIM, cos_a, 1.0)
    s1 = jnp.where(d[None, :] < half_a, -sin_a, 0.0)
    s2 = jnp.where((d[None, :] >= half_a) & (d[None, :] < ROPE_DIM), sin_a, 0.0)
    return jnp.stack([jnp.cos(ang_r), jnp.sin(ang_r), ca, s1, s2])


def _rot_att(x, ca, s1, s2):
    half = ROPE_DIM // 2
    return x * ca + pltpu.roll(x, LANES - half, 1) * s1 + pltpu.roll(x, half, 1) * s2


def _proj_kernel(*refs, cast_plan):
    n_cast = len(cast_plan)
    x_ref, sc_ref, sh_ref, tab_ref, w_ref, wkv_ref = refs[:6]
    cast_in = refs[6:6 + n_cast]
    o_ref, kv_ref, kv32_ref = refs[6 + n_cast:9 + n_cast]
    cast_out = refs[9 + n_cast:9 + 2 * n_cast]
    h_ref = refs[-1]
    n = pl.program_id(2)
    tm = x_ref.shape[0]

    step = (pl.program_id(0) * pl.num_programs(1) + pl.program_id(1)) * pl.num_programs(2) + n
    for (first, count), src_ref, dst_ref in zip(cast_plan, cast_in, cast_out):
        @pl.when((step >= first) & (step < first + count))
        def _():
            dst_ref[...] = src_ref[...].astype(BF16)

    n_split = max(tm // PROJ_SPLIT_ROWS, 1)
    row_blocks = [slice(r * (tm // n_split), (r + 1) * (tm // n_split)) for r in range(n_split)]
    tn = w_ref.shape[1]

    def mod_rows(ref, rows):
        return ref[...] if ref.shape[0] == 1 else ref[rows, :]

    def att_tabs(rows):
        return tab_ref[2, rows, :], tab_ref[3, rows, :], tab_ref[4, rows, :]

    def modulate_and_kv(rows):
        h = x_ref[rows, :] * (1.0 + mod_rows(sc_ref, rows)) + mod_rows(sh_ref, rows)
        h_ref[rows, :] = h.astype(BF16)
        kv = jnp.dot(h_ref[rows, :], wkv_ref[...], preferred_element_type=F32)
        ca, s1, s2 = att_tabs(rows)
        tail0 = tm - kv32_ref.shape[0]
        keep = slice(max(rows.start, tail0), rows.stop)
        for s in range(KV_W // LANES):
            xs = kv[:, s * LANES:(s + 1) * LANES]
            if s < KV_W // LANES // 2:
                xs = _rot_att(xs, ca, s1, s2)
            kv_ref[rows, s * LANES:(s + 1) * LANES] = xs.astype(BF16)
            if keep.start < keep.stop:
                @pl.when(pl.program_id(1) == pl.num_programs(1) - 1)
                def _():
                    kv32_ref[keep.start - tail0:keep.stop - tail0, s * LANES:(s + 1) * LANES] = (
                        xs[keep.start - rows.start:keep.stop - rows.start])

    def matmul(rows):
        return jnp.dot(h_ref[rows, :], w_ref[...], preferred_element_type=F32)

    def rot_ret_rows(rows, scale):
        acc = matmul(rows)
        cr, sr = tab_ref[0, rows, :], tab_ref[1, rows, :]
        for j in range(tn // RET_DIM):
            a = j * RET_DIM
            x1 = acc[:, a:a + LANES]
            x2 = acc[:, a + LANES:a + 2 * LANES]
            o_ref[rows, a:a + LANES] = ((x1 * cr - x2 * sr) * scale).astype(BF16)
            o_ref[rows, a + LANES:a + 2 * LANES] = ((x1 * sr + x2 * cr) * scale).astype(BF16)

    def rot_ret(scale):
        for rows in row_blocks:
            rot_ret_rows(rows, scale)

    region = n // (D_MODEL // tn)

    @pl.when(n == 0)
    def _():
        for rows in row_blocks:
            modulate_and_kv(rows)
            rot_ret_rows(rows, 1.0)

    @pl.when(((region == 0) & (n > 0)) | (region == 1))
    def _():
        rot_ret(jnp.where(region == 0, 1.0, RET_DIM ** -0.5))

    @pl.when(region == 2)
    def _():
        for rows in row_blocks:
            o_ref[rows, :] = matmul(rows).astype(BF16)

    @pl.when(region == 3)
    def _():
        for rows in row_blocks:
            acc = matmul(rows)
            o_ref[rows, :] = (acc * jax.nn.sigmoid(acc)).astype(BF16)

    @pl.when(region == 4)
    def _():
        for rows in row_blocks:
            acc = matmul(rows)
            ca, s1, s2 = att_tabs(rows)
            for s in range(tn // LANES):
                xs = _rot_att(acc[:, s * LANES:(s + 1) * LANES], ca, s1, s2)
                o_ref[rows, s * LANES:(s + 1) * LANES] = (xs * ATT_HEAD_DIM ** -0.5).astype(BF16)

    @pl.when(region >= 5)
    def _():
        for rows in row_blocks:
            o_ref[rows, :] = jax.nn.sigmoid(matmul(rows)).astype(BF16)


def _mod_spec(mod, tm):
    if mod.shape[1] == 1:
        return pl.BlockSpec((None, 1, D_MODEL), lambda b, t, *_: (b, 0, 0))
    return pl.BlockSpec((None, tm, D_MODEL), lambda b, t, *_: (b, t, 0))


def _proj(x, sc, sh, tab, w_main, w_kv, tm, kv32_rows, tn=PROJ_TN, cast=()):
    B, T, _ = x.shape
    nt = T // tm
    n_tiles = MAIN_W // tn
    cast_plan, cast_specs, cast_shapes, first = [], [], [], 0
    for wt, axis in cast:
        count = wt.shape[axis] // CAST_SLAB
        block = tuple(CAST_SLAB if d == axis else wt.shape[d] for d in range(2))

        def slab_map(b, t, n, first=first, count=count, axis=axis):
            slab = jnp.clip((b * nt + t) * n_tiles + n - first, 0, count - 1)
            return (slab, 0) if axis == 0 else (0, slab)

        cast_plan.append((first, count))
        cast_specs.append(pl.BlockSpec(block, slab_map))
        cast_shapes.append(jax.ShapeDtypeStruct(wt.shape, BF16))
        first += count
    outs = pl.pallas_call(
        functools.partial(_proj_kernel, cast_plan=tuple(cast_plan)),
        grid=(B, nt, n_tiles),
        in_specs=[pl.BlockSpec((None, tm, D_MODEL), lambda b, t, n: (b, t, 0)),
                  _mod_spec(sc, tm), _mod_spec(sh, tm),
                  pl.BlockSpec((5, tm, LANES), lambda b, t, n: (0, t, 0)),
                  pl.BlockSpec((D_MODEL, tn), lambda b, t, n: (0, n)),
                  pl.BlockSpec((D_MODEL, KV_W), lambda b, t, n: (0, 0), pipeline_mode=pl.Buffered(1))] + cast_specs,
        out_specs=[pl.BlockSpec((None, tm, tn), lambda b, t, n: (b, t, n)),
                   pl.BlockSpec((None, tm, KV_W), lambda b, t, n: (b, t, 0)),
                   pl.BlockSpec((None, kv32_rows, KV_W), lambda b, t, n: (b, 0, 0))] + cast_specs,
        out_shape=[jax.ShapeDtypeStruct((B, T, MAIN_W), BF16),
                   jax.ShapeDtypeStruct((B, T, KV_W), BF16),
                   jax.ShapeDtypeStruct((B, kv32_rows, KV_W), F32)] + cast_shapes,
        scratch_shapes=[pltpu.VMEM((tm, D_MODEL), BF16)],
        compiler_params=_params(("arbitrary", "arbitrary", "arbitrary")),
        name="proj",
    )(x, sc, sh, tab, w_main, w_kv, *[wt for wt, _ in cast])
    return outs[0], outs[1], outs[2], outs[3:]


def _repack_kernel(w_ref, o_ref):
    o_ref[...] = w_ref[...].astype(BF16)


def _repack_w_in(w):
    kv_blk = 5 * D_MODEL // KV_W
    return pl.pallas_call(
        _repack_kernel,
        grid=(MAIN_W // KV_W,),
        in_specs=[pl.BlockSpec((D_MODEL, KV_W), lambda n: (0, jnp.where(n >= kv_blk, n + 1, n)))],
        out_specs=pl.BlockSpec((D_MODEL, KV_W), lambda n: (0, n)),
        out_shape=jax.ShapeDtypeStruct((D_MODEL, MAIN_W), BF16),
        compiler_params=_params(("arbitrary",)),
        name="repack_w_in",
    )(w)


def _ret_kernel(lg_ref, q_ref, k_ref, v_ref, gs_ref, sg_ref, gn_ref, s0_ref, o_ref, sout_ref,
                s_ref, dm_ref, qd_ref, kd_ref, *, chunk, n_chunks, heads):
    hg = pl.program_id(1)
    t = pl.program_id(2)
    nt = (((1,), (1,)), ((), ()))
    tn = (((0,), (0,)), ((), ()))
    for hl in range(heads):
        lg = lg_ref[hg * heads + hl]
        cols = slice(hl * RET_DIM, (hl + 1) * RET_DIM)

        @pl.when(t == 0)
        def _():
            s_ref[hl] = s0_ref[hl]
            i = lax.broadcasted_iota(jnp.int32, (chunk, chunk), 0)
            j = lax.broadcasted_iota(jnp.int32, (chunk, chunk), 1)
            diff = (i - j).astype(F32)
            dm_ref[hl] = jnp.where(diff >= 0, jnp.exp(lg * jnp.maximum(diff, 0.0)), 0.0)
            r = lax.broadcasted_iota(jnp.int32, (chunk, RET_DIM), 0).astype(F32)
            qd_ref[hl] = jnp.exp(lg * (r + 1.0))
            kd_ref[hl] = jnp.exp(lg * (chunk - 1.0 - r))

        g_chunk = jnp.exp(jnp.full((1, RET_DIM), lg * chunk, F32))
        for c in range(n_chunks):
            rows = pl.ds(c * chunk, chunk)
            q, k, v = q_ref[rows, cols], k_ref[rows, cols], v_ref[rows, cols]
            inner = (lax.dot_general(q, k, nt, preferred_element_type=F32) * dm_ref[hl]).astype(BF16)
            s_prev = s_ref[hl]
            if chunk % LANES == 0:
                q_dec = (q.astype(F32) * qd_ref[hl]).astype(BF16)
                o = jnp.dot(jnp.concatenate([inner, q_dec], axis=1),
                            jnp.concatenate([v, s_prev.astype(BF16)], axis=0), preferred_element_type=F32)
            else:
                o = jnp.dot(inner, v, preferred_element_type=F32)
                o = o + jnp.dot(q, s_prev.astype(BF16), preferred_element_type=F32) * qd_ref[hl]
            kdec = (k.astype(F32) * kd_ref[hl]).astype(BF16)
            s_ref[hl] = s_prev * g_chunk + lax.dot_general(kdec, v, tn, preferred_element_type=F32)
            mu = jnp.mean(o, axis=-1, keepdims=True)
            xc = o - mu
            var = jnp.mean(xc * xc, axis=-1, keepdims=True)
            gn = xc * lax.rsqrt(var + LN_EPS) * gn_ref[:, cols]
            out = sg_ref[rows, cols].astype(F32) * (gs_ref[rows, cols].astype(F32) * gn)
            o_ref[rows, cols] = out.astype(BF16)

    @pl.when(t == pl.num_programs(2) - 1)
    def _():
        sout_ref[...] = s_ref[...]


def _retention(proj, gn_g, s0, chunk, tt, heads):
    B, T, _ = proj.shape
    n_chunks = tt // chunk
    width = heads * RET_DIM
    log_gamma = jnp.log1p(-jnp.exp2(-5.0 - jnp.arange(RET_HEADS, dtype=F32)))

    def col(off):
        base = off // width
        return pl.BlockSpec((None, tt, width), lambda b, h, t: (b, t, base + h))

    state = pl.BlockSpec((None, heads, RET_DIM, RET_DIM), lambda b, h, t: (b, h, 0, 0))
    return pl.pallas_call(
        functools.partial(_ret_kernel, chunk=chunk, n_chunks=n_chunks, heads=heads),
        grid=(B, RET_HEADS // heads, T // tt),
        in_specs=[pl.BlockSpec(memory_space=pltpu.SMEM),
                  col(COL_QR), col(COL_KR), col(COL_VR), col(COL_GR), col(COL_SGR),
                  pl.BlockSpec((1, width), lambda b, h, t: (0, h)), state],
        out_specs=[pl.BlockSpec((None, tt, width), lambda b, h, t: (b, t, h)), state],
        out_shape=[jax.ShapeDtypeStruct((B, T, D_MODEL), BF16),
                   jax.ShapeDtypeStruct((B, RET_HEADS, RET_DIM, RET_DIM), F32)],
        scratch_shapes=[pltpu.VMEM((heads, RET_DIM, RET_DIM), F32),
                        pltpu.VMEM((heads, chunk, chunk), F32),
                        pltpu.VMEM((heads, chunk, RET_DIM), F32),
                        pltpu.VMEM((heads, chunk, RET_DIM), F32)],
        compiler_params=_params(("arbitrary", "arbitrary", "arbitrary")),
        name="retention",
    )(log_gamma, proj, proj, proj, proj, proj, gn_g.reshape(1, D_MODEL), s0)


def _att_kernel(sink_ref, q_ref, halo_ref, cur_ref, sg_ref, o_ref, *, cq, n_chunks, own_valid, mask_first):
    i = pl.program_id(1)
    kv = jnp.concatenate([halo_ref[...], cur_ref[...]], axis=0).astype(F32)
    n_keys = kv.shape[0]
    n_kc = n_keys // CHUNK
    n_tiles = (n_kc + 1) // 2
    lane = lax.broadcasted_iota(jnp.int32, (1, LANES), 1)
    lo = lane < ATT_HEAD_DIM
    ones_lo = jnp.broadcast_to(jnp.where(lo, 1.0, 0.0), (n_keys, LANES)).astype(BF16)
    ones_hi = jnp.broadcast_to(jnp.where(lo, 0.0, 1.0), (n_keys, LANES)).astype(BF16)
    own_ok = (lane % ATT_HEAD_DIM) < own_valid
    nt = (((1,), (1,)), ((), ()))
    pairs = ATT_GROUP // 2
    rows_q = pairs * cq
    k_off = ATT_KV_HEADS * ATT_HEAD_DIM
    users = [[c for c in range(n_chunks) if c <= 2 * u + 1 and c + 2 >= 2 * u] for u in range(n_tiles)]

    def tile_rows(a_lo, a_hi, u):
        parts = []
        for j in (2 * u, 2 * u + 1):
            for a in (a_lo, a_hi):
                parts.append(a[j * CHUNK:(j + 1) * CHUNK] if j < n_kc else jnp.zeros((CHUNK, a.shape[1]), a.dtype))
        return jnp.concatenate(parts, axis=0)

    for y in range(ATT_KV_HEADS // 2):
        ks = kv[:, y * LANES:(y + 1) * LANES]
        kr = pltpu.roll(ks, ATT_HEAD_DIM, 1)
        vs = kv[:, k_off + y * LANES:k_off + (y + 1) * LANES]
        vr = pltpu.roll(vs, ATT_HEAD_DIM, 1)
        for par in range(2):
            x = 2 * y + par
            k_even, k_odd = (ks, kr) if par == 0 else (kr, ks)
            v_even, v_odd = (vs, vr) if par == 0 else (vr, vs)
            klo = jnp.where(lo, k_even, 0.0).astype(BF16)
            khi = jnp.where(lo, 0.0, k_odd).astype(BF16)
            vlo = jnp.concatenate([jnp.where(lo, v_even, 0.0).astype(BF16), ones_lo], axis=1)
            vhi = jnp.concatenate([jnp.where(lo, 0.0, v_odd).astype(BF16), ones_hi], axis=1)
            sink_b = jnp.concatenate(
                [jnp.broadcast_to(jnp.where(lo, sink_ref[x * ATT_GROUP + 2 * p],
                                            sink_ref[x * ATT_GROUP + 2 * p + 1]), (cq, LANES))
                 for p in range(pairs)], axis=0)
            col0 = x * ATT_GROUP * ATT_HEAD_DIM
            qp = [jnp.concatenate([q_ref[c * cq:(c + 1) * cq, col0 + p * LANES:col0 + (p + 1) * LANES]
                                   for p in range(pairs)], axis=0) for c in range(n_chunks)]
            s_tiles = [lax.dot_general(jnp.concatenate([qp[c] for c in users[u]], axis=0), tile_rows(klo, khi, u),
                                       nt, preferred_element_type=F32) for u in range(n_tiles)]
            p_blocks, m_all = {}, {}
            for c in range(n_chunks):
                sj = []
                for jj in range(3):
                    u, half = (c + jj) // 2, (c + jj) % 2
                    r = users[u].index(c) * rows_q
                    sj.append(s_tiles[u][r:r + rows_q, half * LANES:(half + 1) * LANES])
                if mask_first:
                    for jj in range(2):
                        if c + jj < 2:
                            sj[jj] = jnp.where(i * n_chunks + (c + jj - 2) < 0, NEG_INF, sj[jj])
                if own_valid < CHUNK:
                    sj[2] = jnp.where(own_ok, sj[2], NEG_INF)
                m3 = jnp.maximum(jnp.maximum(sj[0], sj[1]), sj[2])
                m_even = jnp.max(jnp.where(lo, m3, NEG_INF), axis=1, keepdims=True)
                m_odd = jnp.max(jnp.where(lo, NEG_INF, m3), axis=1, keepdims=True)
                m_b = jnp.maximum(jnp.where(lo, m_even, m_odd), sink_b)
                m_all[c] = m_b
                for jj in range(3):
                    p_blocks[(c, c + jj)] = jnp.exp(sj[jj] - m_b).astype(BF16)
            zero_p = jnp.zeros((rows_q, LANES), BF16)
            o_tiles = []
            for u in range(n_tiles):
                p_u = jnp.concatenate(
                    [jnp.concatenate([p_blocks.get((c, j), zero_p) for j in (2 * u, 2 * u + 1)], axis=1)
                     for c in users[u]], axis=0)
                o_tiles.append(jnp.dot(p_u, tile_rows(vlo, vhi, u), preferred_element_type=F32))
            for c in range(n_chunks):
                oe = None
                for u in sorted({c // 2, (c + 2) // 2}):
                    r = users[u].index(c) * rows_q
                    part = o_tiles[u][r:r + rows_q]
                    oe = part if oe is None else oe + part
                den = oe[:, LANES:] + jnp.exp(sink_b - m_all[c])
                res = oe[:, :LANES] / den
                for p in range(pairs):
                    cols = slice(col0 + p * LANES, col0 + (p + 1) * LANES)
                    rows = slice(c * cq, (c + 1) * cq)
                    gate = sg_ref[rows, cols].astype(F32)
                    o_ref[rows, cols] = (gate * res[p * cq:(p + 1) * cq]).astype(BF16)


def _attention(proj, halo_src, cur_src, sinks, cq, n_chunks, own_valid, mask_first):
    B, T, _ = proj.shape
    tq = cq * n_chunks
    cur_rows = CHUNK * n_chunks
    q_blk, sg_blk = COL_QA // D_MODEL, COL_SGA // D_MODEL
    halo_per_cur = cur_rows // WINDOW if mask_first else 0

    def halo_map(b, i):
        return (b, jnp.maximum(i * halo_per_cur - 1, 0), 0) if mask_first else (b, 0, 0)

    return pl.pallas_call(
        functools.partial(_att_kernel, cq=cq, n_chunks=n_chunks, own_valid=own_valid, mask_first=mask_first),
        grid=(B, T // tq),
        in_specs=[pl.BlockSpec(memory_space=pltpu.SMEM),
                  pl.BlockSpec((None, tq, D_MODEL), lambda b, i: (b, i, q_blk)),
                  pl.BlockSpec((None, WINDOW, KV_W), halo_map),
                  pl.BlockSpec((None, cur_rows, KV_W), lambda b, i: (b, i, 0)),
                  pl.BlockSpec((None, tq, D_MODEL), lambda b, i: (b, i, sg_blk))],
        out_specs=pl.BlockSpec((None, tq, D_MODEL), lambda b, i: (b, i, 0)),
        out_shape=jax.ShapeDtypeStruct((B, T, D_MODEL), BF16),
        compiler_params=_params(("arbitrary", "arbitrary")),
        name="attention",
    )(sinks, proj, halo_src, cur_src, proj)


def _layernorm(z, g, b):
    mu = jnp.mean(z, axis=-1, keepdims=True)
    zc = z - mu
    var = jnp.mean(zc * zc, axis=-1, keepdims=True)
    return zc * lax.rsqrt(var + LN_EPS) * g + b


def _out_kernel(r_ref, a_ref, x_ref, gt_ref, sc_ref, sh_ref, w_ref, g_ref, b_ref, x1_ref, h2_ref):
    n_split = max(x_ref.shape[0] // OUT_SPLIT_ROWS, 1)
    rows_per = x_ref.shape[0] // n_split

    def mod_rows(ref, rows):
        return ref[...] if ref.shape[0] == 1 else ref[rows, :]

    for r in range(n_split):
        rows = slice(r * rows_per, (r + 1) * rows_per)
        merged = r_ref[rows, :] + a_ref[rows, :]
        mix = jnp.dot(merged, w_ref[...], preferred_element_type=F32)
        z = ALPHA * x_ref[rows, :] + mod_rows(gt_ref, rows) * mix
        x1 = _layernorm(z, g_ref[...], b_ref[...])
        x1_ref[rows, :] = x1
        h2_ref[rows, :] = (x1 * (1.0 + mod_rows(sc_ref, rows)) + mod_rows(sh_ref, rows)).astype(BF16)


def _out_proj(ret_m, att_m, x, gt, sc, sh, w_o, g, b, tm):
    B, T, _ = x.shape
    tok = pl.BlockSpec((None, tm, D_MODEL), lambda bb, t: (bb, t, 0))
    vec = pl.BlockSpec((1, D_MODEL), lambda bb, t: (0, 0))
    return pl.pallas_call(
        _out_kernel,
        grid=(B, T // tm),
        in_specs=[tok, tok, tok, _mod_spec(gt, tm), _mod_spec(sc, tm), _mod_spec(sh, tm),
                  pl.BlockSpec((D_MODEL, D_MODEL), lambda bb, t: (0, 0)), vec, vec],
        out_specs=[tok, tok],
        out_shape=[jax.ShapeDtypeStruct((B, T, D_MODEL), F32),
                   jax.ShapeDtypeStruct((B, T, D_MODEL), BF16)],
        compiler_params=_params(("arbitrary", "arbitrary")),
        name="out_proj",
    )(ret_m, att_m, x, gt, sc, sh, w_o, g.reshape(1, D_MODEL), b.reshape(1, D_MODEL))


def _ffn_kernel(h_ref, x1_ref, gt_ref, wg_ref, wu_ref, wd_ref, g_ref, b_ref, o_ref):
    f = pl.program_id(2)
    last = pl.num_programs(2) - 1
    tm = h_ref.shape[0]

    def row_blocks(rows_per):
        n_split = max(tm // rows_per, 1)
        return [slice(r * (tm // n_split), (r + 1) * (tm // n_split)) for r in range(n_split)]

    @pl.when(f == 0)
    def _():
        o_ref[...] = jnp.zeros_like(o_ref)

    def step(rows_per, finalize):
        h = h_ref[...]
        a = jnp.dot(h, wg_ref[...], preferred_element_type=F32)
        u = jnp.dot(h, wu_ref[...], preferred_element_type=F32)
        act = (a * jax.nn.sigmoid(a) * u).astype(BF16)
        for rows in row_blocks(rows_per):
            acc = o_ref[rows, :] + jnp.dot(act[rows, :], wd_ref[...], preferred_element_type=F32)
            if finalize:
                gt = gt_ref[...] if gt_ref.shape[0] == 1 else gt_ref[rows, :]
                acc = _layernorm(ALPHA * x1_ref[rows, :] + gt * acc, g_ref[...], b_ref[...])
            o_ref[rows, :] = acc

    @pl.when(f < last)
    def _():
        step(FFN_SPLIT_ROWS, False)

    @pl.when(f == last)
    def _():
        step(FFN_LAST_SPLIT_ROWS, True)


def _ffn(h2, x1, gt, wg, wu, wd, g, b, tm, tf):
    B, T, _ = x1.shape
    tok = pl.BlockSpec((None, tm, D_MODEL), lambda bb, t, f: (bb, t, 0))
    vec = pl.BlockSpec((1, D_MODEL), lambda bb, t, f: (0, 0))
    return pl.pallas_call(
        _ffn_kernel,
        grid=(B, T // tm, D_FF // tf),
        in_specs=[tok, tok, _mod_spec(gt, tm),
                  pl.BlockSpec((D_MODEL, tf), lambda bb, t, f: (0, f)),
                  pl.BlockSpec((D_MODEL, tf), lambda bb, t, f: (0, f)),
                  pl.BlockSpec((tf, D_MODEL), lambda bb, t, f: (f, 0)), vec, vec],
        out_specs=tok,
        out_shape=jax.ShapeDtypeStruct((B, T, D_MODEL), F32),
        compiler_params=_params(("arbitrary", "arbitrary", "arbitrary")),
        name="ffn",
    )(h2, x1, gt, wg, wu, wd, g.reshape(1, D_MODEL), b.reshape(1, D_MODEL))


def _largest_tile(total, cap):
    t = min(total, cap)
    while total % t:
        t //= 2
    return t


def kernel(x_prompt, x_sample, c_prompt, c_sample, cache_attn_k, cache_attn_v, state_ret, w_ada, b_ada, w_in,
           gn_g, attn_sinks, w_o, ln1_g, ln1_b, w_ffn_gate, w_ffn_up, w_ffn_down, ln2_g, ln2_b):
    B, T, _ = x_prompt.shape
    Bs, Ls, _ = x_sample.shape
    l = 0

    n_c = B + Bs
    pad = (-n_c) % 8
    c_all = jnp.concatenate([c_prompt, c_sample, jnp.zeros((pad, D_MODEL), F32)], axis=0)
    mods = _mods(c_all, w_ada[l], b_ada[l])
    mods_p = [m[:, None, :] for m in jnp.split(mods[:B], 6, axis=-1)]
    mods_s = [jnp.repeat(m, Ls, axis=0)[None] for m in jnp.split(mods[B:n_c], 6, axis=-1)]

    w = w_in[l]
    o_ka = 5 * D_MODEL
    o_gate = o_ka + KV_W
    w_main = _repack_w_in(w)
    w_kv = w[:, o_ka:o_gate].astype(BF16)

    sh_a, sc_a, gt_a, sh_f, sc_f, gt_f = mods_p
    tab_p = _rope_tables(jnp.arange(T))
    tm_proj = _largest_tile(T, 1024)
    later = ((w_ffn_gate[l], 1), (w_ffn_up[l], 1), (w_ffn_down[l], 0), (w_o[l], 1))
    if sum(wt.shape[axis] // CAST_SLAB for wt, axis in later) <= B * (T // tm_proj) * (MAIN_W // PROJ_TN):
        proj, kvb, kv_p, (wg, wu, wd, wo) = _proj(x_prompt, sc_a, sh_a, tab_p, w_main, w_kv, tm_proj, WINDOW, cast=later)
    else:
        proj, kvb, kv_p, _ = _proj(x_prompt, sc_a, sh_a, tab_p, w_main, w_kv, tm_proj, WINDOW)
        wg, wu, wd, wo = (wt.astype(BF16) for wt, _ in later)
    ret_m, s_p = _retention(proj, gn_g[l], jnp.zeros((B, RET_HEADS, RET_DIM, RET_DIM), F32),
                            _largest_tile(T, 256), _largest_tile(T, 8192), 1)
    att_m = _attention(proj, kvb, kvb, attn_sinks[l], CHUNK, _largest_tile(T // CHUNK, 32), CHUNK, True)
    x1, h2 = _out_proj(ret_m, att_m, x_prompt, gt_a, sc_f, sh_f, wo, ln1_g[l], ln1_b[l], _largest_tile(T, 512))
    y_p = _ffn(h2, x1, gt_f, wg, wu, wd, ln2_g[l], ln2_b[l], _largest_tile(T, 1024), FFN_TF)
    k_p = kv_p[..., :KV_W // 2].reshape(B, WINDOW, ATT_KV_HEADS, ATT_HEAD_DIM)
    v_p = kv_p[..., KV_W // 2:].reshape(B, WINDOW, ATT_KV_HEADS, ATT_HEAD_DIM)

    R = Bs * Ls
    xs = x_sample.reshape(1, R, D_MODEL)
    pos_s = jnp.tile(PAST_LEN + jnp.arange(Ls), Bs)
    cache = jnp.concatenate([cache_attn_k[l].reshape(Bs, WINDOW, KV_W // 2),
                             cache_attn_v[l].reshape(Bs, WINDOW, KV_W // 2)], axis=-1).astype(BF16)

    def att_sample(proj, kvb):
        new = jnp.pad(kvb.reshape(Bs, Ls, KV_W), ((0, 0), (0, CHUNK - Ls), (0, 0)))
        o = _attention(proj.reshape(Bs, Ls, MAIN_W), cache, new, attn_sinks[l], Ls, 1, Ls, False)
        return o.reshape(1, R, D_MODEL)

    def ret_sample(proj, s0):
        return _retention(proj.reshape(Bs, Ls, MAIN_W), gn_g[l], s0, Ls, Ls, RET_HEADS)

    sh_a, sc_a, gt_a, sh_f, sc_f, gt_f = mods_s
    tab_s = _rope_tables(pos_s)
    proj_s, kvb_s, kv_s, _ = _proj(xs, sc_a, sh_a, tab_s, w_main, w_kv, R, R, tn=D_MODEL)
    ret_s, s_s = ret_sample(proj_s, state_ret[l])
    att_s = att_sample(proj_s, kvb_s)
    x1_s, h2_s = _out_proj(ret_s.reshape(1, R, D_MODEL), att_s, xs, gt_a, sc_f, sh_f, wo, ln1_g[l], ln1_b[l], R)
    y_s = _ffn(h2_s, x1_s, gt_f, wg, wu, wd, ln2_g[l], ln2_b[l], R, FFN_TF).reshape(Bs, Ls, D_MODEL)
    kv_s = kv_s.reshape(Bs, Ls, KV_W)
    k_s = kv_s[..., :KV_W // 2].reshape(Bs, Ls, ATT_KV_HEADS, ATT_HEAD_DIM)
    v_s = kv_s[..., KV_W // 2:].reshape(Bs, Ls, ATT_KV_HEADS, ATT_HEAD_DIM)

    return (y_p, y_s, k_p[None], v_p[None], s_p[None], k_s[None], v_s[None], s_s[None])
```
